```python
import math
import jax
import jax.numpy as jnp
from jax import lax
import numpy as np

D_MODEL = 1024
BATCH = 32
SEQ = 256
DEPTH = 2
DEC_BATCH = 4
DEC_SEQ = 4096
PAST_LEN = 512

GRID_W = 64
D_MIX = D_MODEL
GROUP_W = D_MIX // 4
ATT_HEADS = 4
ATT_KV_HEADS = 2
ATT_HEAD_DIM = GROUP_W // ATT_HEADS
WINDOW = 128
BLOCK = 128
DIF_HEADS = 4
DIF_V_DIM = GROUP_W // DIF_HEADS
DIF_QK_DIM = DIF_V_DIM // 2
HY_CH = GROUP_W
HY_STREAMS = 3
FILT_BANDS = 16
FILT_EMB = 1 + 2 * FILT_BANDS
FILT_HIDDEN = 64
HY_DECAY_TARGET = 1e-2
HY_FAST_DECAY = 0.3
HY_SLOW_DECAY = 1.5
HY_MIN_DECAY = math.log(HY_DECAY_TARGET) / HY_SLOW_DECAY
HY_MAX_DECAY = math.log(HY_DECAY_TARGET) / HY_FAST_DECAY
FN_GROUPS = 4
FN_GROUP_CH = GROUP_W // FN_GROUPS
ROPE_BASE = 10000.0
EPS = 1e-6
NEG_INF = -1e30

SPLIT_SIZES = (ATT_HEADS * ATT_HEAD_DIM, ATT_KV_HEADS * ATT_HEAD_DIM, ATT_KV_HEADS * ATT_HEAD_DIM, GROUP_W,
               2 * DIF_HEADS * DIF_QK_DIM, 2 * DIF_HEADS * DIF_QK_DIM, DIF_HEADS * DIF_V_DIM, GROUP_W,
               HY_STREAMS * HY_CH, GROUP_W,
               GROUP_W, GROUP_W)
D_IN = sum(SPLIT_SIZES)

kernel_name = 'hybrid_diffusion_parallel_heads_step'

F32 = jnp.float32


def rms_norm(x, g):
    xf = x.astype(F32)
    y = xf * lax.rsqrt(jnp.mean(xf * xf, axis=-1, keepdims=True) + EPS)
    return (y * g.astype(F32)).astype(x.dtype)


def axial_rope_tables(n_tokens, head_dim):
    pos = jnp.arange(n_tokens)
    row = (pos // GRID_W).astype(F32)
    col = (pos % GRID_W).astype(F32)
    n_freq = head_dim // 4
    inv = ROPE_BASE ** (-jnp.arange(n_freq, dtype=F32) / n_freq)
    ang = jnp.concatenate([row[:, None] * inv, col[:, None] * inv], axis=-1)
    return jnp.cos(ang), jnp.sin(ang)


def apply_rope(x, cos, sin):
    shape = (cos.shape[0],) + (1,) * (x.ndim - 3) + (cos.shape[1],)
    cs, sn = cos.reshape(shape), sin.reshape(shape)
    x1, x2 = jnp.split(x.astype(F32), 2, axis=-1)
    return jnp.concatenate([x1 * cs - x2 * sn, x1 * sn + x2 * cs], axis=-1).astype(x.dtype)


def split_projection(proj):
    points, acc = [], 0
    for s in SPLIT_SIZES[:-1]:
        acc += s
        points.append(acc)
    return jnp.split(proj, points, axis=-1)


def sink_softmax_values(s, vals, sink):
    sk = sink.astype(F32)[None, :, :, None, None]
    m = jnp.maximum(jnp.max(s, axis=-1, keepdims=True), sk)
    p = jnp.exp(s - m)
    denom = jnp.sum(p, axis=-1, keepdims=True) + jnp.exp(sk - m)
    return jnp.einsum('bkgqn,bnkd->bqkgd', p / denom, vals.astype(F32))


def gqa_context(q, k, v, sink):
    bsz, n = q.shape[:2]
    grp = ATT_HEADS // ATT_KV_HEADS
    qb = q.reshape(bsz, n // BLOCK, BLOCK, ATT_KV_HEADS, grp, ATT_HEAD_DIM).swapaxes(0, 1)
    sk = sink.reshape(ATT_KV_HEADS, grp)
    scale = ATT_HEAD_DIM ** -0.5

    def block(qblk):
        s = jnp.einsum('bqkgd,bnkd->bkgqn', qblk, k).astype(F32) * scale
        return sink_softmax_values(s, v, sk)

    o = lax.map(block, qb)
    return o.swapaxes(0, 1).reshape(bsz, n, ATT_HEADS * ATT_HEAD_DIM)


def gqa_latent(q, k, v, ctx_k, ctx_v, sink):
    bsz, n = q.shape[:2]
    grp = ATT_HEADS // ATT_KV_HEADS
    nb = n // BLOCK
    qb = q.reshape(bsz, nb, BLOCK, ATT_KV_HEADS, grp, ATT_HEAD_DIM).swapaxes(0, 1)
    sk = sink.reshape(ATT_KV_HEADS, grp)
    scale = ATT_HEAD_DIM ** -0.5
    pad = ((0, 0), (BLOCK, BLOCK), (0, 0), (0, 0))
    kp, vp = jnp.pad(k, pad), jnp.pad(v, pad)
    offs = jnp.arange(3 * BLOCK)
    qoffs = jnp.arange(BLOCK)

    def block(args):
        i, qblk = args
        start = i * BLOCK
        kw = lax.dynamic_slice_in_dim(kp, start, 3 * BLOCK, axis=1)
        vw = lax.dynamic_slice_in_dim(vp, start, 3 * BLOCK, axis=1)
        kpos = start - BLOCK + offs
        qpos = start + qoffs
        valid = ((jnp.abs(qpos[:, None] - kpos[None, :]) <= WINDOW)
                 & (kpos >= 0)[None, :] & (kpos < n)[None, :])
        s_win = jnp.einsum('bqkgd,bnkd->bkgqn', qblk, kw).astype(F32) * scale
        s_win = jnp.where(valid, s_win, NEG_INF)
        s_ctx = jnp.einsum('bqkgd,bnkd->bkgqn', qblk, ctx_k).astype(F32) * scale
        s = jnp.concatenate([s_win, s_ctx], axis=-1)
        vals = jnp.concatenate([vw, ctx_v.astype(vw.dtype)], axis=1)
        return sink_softmax_values(s, vals, sk)

    o = lax.map(block, (jnp.arange(nb), qb))
    return o.swapaxes(0, 1).reshape(bsz, n, ATT_HEADS * ATT_HEAD_DIM)


def diff_attention(q, keys, vals, lam, subln, lam_init):
    bsz, n = q.shape[:2]
    qb = q.reshape(bsz, n // BLOCK, BLOCK, 2, DIF_HEADS, DIF_QK_DIM).swapaxes(0, 1)
    scale = DIF_QK_DIM ** -0.5
    vf = vals.astype(F32)

    def block(qblk):
        s = jnp.einsum('bqmhd,bnmhd->bmhqn', qblk, keys).astype(F32) * scale
        p = jax.nn.softmax(s, axis=-1)
        a = p[:, 0] - lam * p[:, 1]
        return jnp.einsum('bhqn,bnhd->bqhd', a, vf)

    o = lax.map(block, qb).swapaxes(0, 1).reshape(bsz, n, DIF_HEADS, DIF_V_DIM)
    o = rms_norm(o, subln) * (1.0 - lam_init)
    return o.reshape(bsz, n, DIF_HEADS * DIF_V_DIM)


def hyena_filters(n, w1, b1, w2, b2, w3, freq):
    t = jnp.linspace(0.0, 1.0, n, dtype=F32)[:, None]
    w = (2.0 * math.pi / n) * jnp.arange(n, dtype=F32)[:, None]
    f = jnp.linspace(1e-4, FILT_BANDS - 1, FILT_BANDS, dtype=F32)[None, :]
    feats = jnp.concatenate([t, jnp.cos(f * w), -jnp.sin(f * w)], axis=-1)
    fr = freq.astype(F32)
    h = jnp.sin(fr * (feats @ w1.astype(F32) + b1.astype(F32)))
    h = jnp.sin(fr * (h @ w2.astype(F32) + b2.astype(F32)))
    h = (h @ w3.astype(F32)).reshape(n, 2, HY_CH)
    deltas = jnp.abs(jnp.linspace(HY_MIN_DECAY, HY_MAX_DECAY, HY_CH, dtype=F32))
    h = h * jnp.exp(-t * deltas)[:, None, :]
    h = h / (jnp.sum(jnp.abs(h), axis=(0, 1), keepdims=True) + EPS)
    return h[:, 0], h[:, 1]


def bidir_long_conv(z, hf, hb, skip):
    n = z.shape[1]
    kern = jnp.concatenate([hf, jnp.zeros((1, HY_CH), F32), hb[:0:-1]], axis=0)
    zf = z.astype(F32)
    y = jnp.fft.irfft(jnp.fft.rfft(zf, n=2 * n, axis=1) * jnp.fft.rfft(kern, axis=0)[None],
                      n=2 * n, axis=1)[:, :n]
    return y + zf * skip.astype(F32)


def hyena_mixer(u, conv_w, conv_b, w1, b1, w2, b2, w3, freq, skip):
    n = u.shape[1]
    up = jnp.pad(u, ((0, 0), (1, 1), (0, 0)))
    uc = up[:, :-2] * conv_w[0] + up[:, 1:-1] * conv_w[1] + up[:, 2:] * conv_w[2] + conv_b
    x0, x1, v = jnp.split(uc, 3, axis=-1)
    hf, hb = hyena_filters(n, w1, b1, w2, b2, w3, freq)
    return x0.astype(F32) * bidir_long_conv(x1 * v, hf, hb, skip)


def fnet_mixer(u, w, b):
    bsz, n, _ = u.shape
    ug = u.astype(F32).reshape(bsz, n, FN_GROUPS, FN_GROUP_CH)
    f = jnp.fft.fft2(ug, axes=(1, 3), norm='ortho').real
    return f.reshape(bsz, n, GROUP_W) @ w.astype(F32) + b.astype(F32)


def layer_entry(x, cond, w_ada_l, b_ada_l, g_pre_l, w_in_l):
    mod = jax.nn.silu(cond) @ w_ada_l + b_ada_l
    shift, scale, gate = jnp.split(mod[:, None, :], 3, axis=-1)
    h = rms_norm(x, g_pre_l) * (1.0 + scale) + shift
    return split_projection(h @ w_in_l), gate


def layer_exit(x, outs, gates, gate, w_out_l, g_post_l):
    mixed = jnp.concatenate([o.astype(x.dtype) * jax.nn.silu(g) for o, g in zip(outs, gates)], axis=-1)
    return x + gate * rms_norm(mixed @ w_out_l, g_post_l)


def setup_inputs(seed: int = 0) -> dict:
    key = jax.random.key(seed)
    ks = jax.random.split(key, 28)

    def nrm(k, shape, s):
        return s * jax.random.normal(k, shape, F32)

    return {
        'x_prompt': nrm(ks[0], (BATCH, SEQ, D_MODEL), 1.0),
        'x_sample': nrm(ks[1], (DEC_BATCH, DEC_SEQ, D_MODEL), 1.0),
        'cache_attn_k': nrm(ks[2], (DEC_BATCH, DEPTH, PAST_LEN, ATT_KV_HEADS, ATT_HEAD_DIM), 1.0),
        'cache_attn_v': nrm(ks[3], (DEC_BATCH, DEPTH, PAST_LEN, ATT_KV_HEADS, ATT_HEAD_DIM), 1.0),
        'cache_diff_k': nrm(ks[4], (DEC_BATCH, DEPTH, PAST_LEN, 2, DIF_HEADS, DIF_QK_DIM), 1.0),
        'cache_diff_v': nrm(ks[5], (DEC_BATCH, DEPTH, PAST_LEN, DIF_HEADS, DIF_V_DIM), 1.0),
        'c': nrm(ks[6], (DEC_BATCH, D_MODEL), 1.0),
        'c_ctx': nrm(ks[7], (D_MODEL,), 1.0),
        'w_ada': nrm(ks[8], (DEPTH, D_MODEL, 3 * D_MODEL), D_MODEL ** -0.5),
        'b_ada': nrm(ks[9], (DEPTH, 3 * D_MODEL), 0.01),
        'norm_pre': 1.0 + nrm(ks[10], (DEPTH, D_MODEL), 0.05),
        'norm_post': 1.0 + nrm(ks[11], (DEPTH, D_MODEL), 0.05),
        'w_in': nrm(ks[12], (DEPTH, D_MODEL, D_IN), D_MODEL ** -0.5),
        'w_out': nrm(ks[13], (DEPTH, D_MIX, D_MODEL), D_MIX ** -0.5),
        'attn_sink': nrm(ks[14], (DEPTH, ATT_HEADS), 0.5),
        'diff_lambda': nrm(ks[15], (DEPTH, 4, DIF_QK_DIM), 0.1),
        'diff_subln': 1.0 + nrm(ks[16], (DEPTH, DIF_V_DIM), 0.05),
        'hy_conv_w': nrm(ks[17], (DEPTH, 3, HY_STREAMS * HY_CH), 3.0 ** -0.5),
        'hy_conv_b': nrm(ks[18], (DEPTH, HY_STREAMS * HY_CH), 0.01),
        'hy_filt_w1': nrm(ks[19], (DEPTH, FILT_EMB, FILT_HIDDEN), FILT_EMB ** -0.5),
        'hy_filt_b1': nrm(ks[20], (DEPTH, FILT_HIDDEN), 0.1),
        'hy_filt_w2': nrm(ks[21], (DEPTH, FILT_HIDDEN, FILT_HIDDEN), FILT_HIDDEN ** -0.5),
        'hy_filt_b2': nrm(ks[22], (DEPTH, FILT_HIDDEN), 0.1),
        'hy_filt_w3': nrm(ks[23], (DEPTH, FILT_HIDDEN, 2 * HY_CH), FILT_HIDDEN ** -0.5),
        'hy_filt_freq': 1.0 + nrm(ks[24], (DEPTH, FILT_HIDDEN), 0.05),
        'hy_skip': nrm(ks[25], (DEPTH, HY_CH), 0.5),
        'fn_w': nrm(ks[26], (DEPTH, GROUP_W, GROUP_W), GROUP_W ** -0.5),
        'fn_b': nrm(ks[27], (DEPTH, GROUP_W), 0.01),
    }


def reference(x_prompt, x_sample, cache_attn_k, cache_attn_v, cache_diff_k, cache_diff_v, c, c_ctx,
              w_ada, b_ada, norm_pre, norm_post, w_in, w_out, attn_sink, diff_lambda, diff_subln,
              hy_conv_w, hy_conv_b, hy_filt_w1, hy_filt_b1, hy_filt_w2, hy_filt_b2, hy_filt_w3,
              hy_filt_freq, hy_skip, fn_w, fn_b):
    bp, lp, _ = x_prompt.shape
    bs, ls, _ = x_sample.shape
    cos_a, sin_a = axial_rope_tables(ls, ATT_HEAD_DIM)
    cos_d, sin_d = axial_rope_tables(ls, DIF_QK_DIM)
    xp, xs = x_prompt, x_sample
    st_ak, st_av, st_dk, st_dv = [], [], [], []
    for l in range(DEPTH):
        lam_init = 0.8 - 0.6 * math.exp(-0.3 * l)
        lam_par = diff_lambda[l].astype(F32)
        lam = jnp.exp(jnp.sum(lam_par[0] * lam_par[1])) - jnp.exp(jnp.sum(lam_par[2] * lam_par[3])) + lam_init
        hy_args = (hy_conv_w[l], hy_conv_b[l], hy_filt_w1[l], hy_filt_b1[l], hy_filt_w2[l], hy_filt_b2[l],
                   hy_filt_w3[l], hy_filt_freq[l], hy_skip[l])
        entry = (w_ada[l], b_ada[l], norm_pre[l], w_in[l])

        (aq, ak, av, ag, dq, dk, dv, dg, hu, hg, fu, fg), gate = layer_entry(xp, c_ctx[None, :], *entry)
        aq = aq.reshape(bp, lp, ATT_HEADS, ATT_HEAD_DIM)
        ak = ak.reshape(bp, lp, ATT_KV_HEADS, ATT_HEAD_DIM)
        av = av.reshape(bp, lp, ATT_KV_HEADS, ATT_HEAD_DIM)
        dq = dq.reshape(bp, lp, 2, DIF_HEADS, DIF_QK_DIM)
        dk = dk.reshape(bp, lp, 2, DIF_HEADS, DIF_QK_DIM)
        dv = dv.reshape(bp, lp, DIF_HEADS, DIF_V_DIM)
        outs = (gqa_context(aq, ak, av, attn_sink[l]),
                diff_attention(dq, dk, dv, lam, diff_subln[l], lam_init),
                hyena_mixer(hu, *hy_args),
                fnet_mixer(fu, fn_w[l], fn_b[l]))
        st_ak.append(ak)
        st_av.append(av)
        st_dk.append(dk)
        st_dv.append(dv)
        xp = layer_exit(xp, outs, (ag, dg, hg, fg), gate, w_out[l], norm_post[l])

        (aq, ak, av, ag, dq, dk, dv, dg, hu, hg, fu, fg), gate = layer_entry(xs, c, *entry)
        aq = apply_rope(aq.reshape(bs, ls, ATT_HEADS, ATT_HEAD_DIM), cos_a, sin_a)
        ak = apply_rope(ak.reshape(bs, ls, ATT_KV_HEADS, ATT_HEAD_DIM), cos_a, sin_a)
        av = av.reshape(bs, ls, ATT_KV_HEADS, ATT_HEAD_DIM)
        dq = apply_rope(dq.reshape(bs, ls, 2, DIF_HEADS, DIF_QK_DIM), cos_d, sin_d)
        dk = apply_rope(dk.reshape(bs, ls, 2, DIF_HEADS, DIF_QK_DIM), cos_d, sin_d)
        dv = dv.reshape(bs, ls, DIF_HEADS, DIF_V_DIM)
        keys = jnp.concatenate([dk, cache_diff_k[:, l].astype(dk.dtype)], axis=1)
        vals = jnp.concatenate([dv, cache_diff_v[:, l].astype(dv.dtype)], axis=1)
        outs = (gqa_latent(aq, ak, av, cache_attn_k[:, l], cache_attn_v[:, l], attn_sink[l]),
                diff_attention(dq, keys, vals, lam, diff_subln[l], lam_init),
                hyena_mixer(hu, *hy_args),
                fnet_mixer(fu, fn_w[l], fn_b[l]))
        xs = layer_exit(xs, outs, (ag, dg, hg, fg), gate, w_out[l], norm_post[l])

    return (xp, xs, jnp.stack(st_ak, axis=1), jnp.stack(st_av, axis=1), jnp.stack(st_dk, axis=1), jnp.stack(st_dv, axis=1))
```

```python
import functools
import math

import numpy as np
import jax
import jax.numpy as jnp
from jax import lax
from jax.experimental import pallas as pl
from jax.experimental.pallas import tpu as pltpu

F32 = jnp.float32
BF16 = jnp.bfloat16

D_MODEL = 1024
DEPTH = 2
GRID_W = 64
GROUP_W = 256
ATT_HEADS = 4
ATT_KV_HEADS = 2
ATT_HEAD_DIM = 64
WINDOW = 128
BLOCK = 128
DIF_HEADS = 4
DIF_V_DIM = 64
DIF_QK_DIM = 32
HY_CH = 256
FILT_BANDS = 16
FILT_EMB = 1 + 2 * FILT_BANDS
FILT_HIDDEN = 64
HY_MIN_DECAY = math.log(1e-2) / 1.5
HY_MAX_DECAY = math.log(1e-2) / 0.3
FN_GROUP_CH = 64
ROPE_BASE = 10000.0
EPS = 1e-6
NEG_INF = -1e30

C_AQ, C_AK, C_AV, C_AG = 0, 256, 384, 512
C_DQ, C_DK, C_DV, C_DG = 768, 1024, 1280, 1536
C_HU, C_HG, C_FU, C_FG, D_IN = 1792, 2560, 2816, 3072, 3328
C_DVE = D_IN
D_IN_EXT = D_IN + 2 * GROUP_W
LOG2_E = math.log2(math.e)

LANES = 128
COND_ROWS = 8
CTX_ROW = 4
VMEM_LIMIT = 56 * 1024 * 1024

HY_N2 = 64
FN_N2 = 64


def _cparams(*sem):
    return pltpu.CompilerParams(dimension_semantics=sem, vmem_limit_bytes=VMEM_LIMIT)


def _dot(a, b):
    return jnp.dot(a, b, preferred_element_type=F32)


def _bdot(a, b):
    return jnp.dot(a.astype(BF16), b.astype(BF16), preferred_element_type=F32)


def _dot3(a, b):
    ah = a.astype(BF16)
    al = (a - ah.astype(F32)).astype(BF16)
    bh = b.astype(BF16)
    bl = (b - bh.astype(F32)).astype(BF16)
    return _dot(ah, bh) + _dot(ah, bl) + _dot(al, bh)


def _bf16_table(a):
    return jnp.asarray(a, F32).astype(BF16)


def _rope_tables(n, head_dim):
    pos = np.arange(n)
    row = (pos // GRID_W).astype(np.float64)
    col = (pos % GRID_W).astype(np.float64)
    n_freq = head_dim // 4
    inv = ROPE_BASE ** (-np.arange(n_freq, dtype=np.float64) / n_freq)
    ang = np.concatenate([row[:, None] * inv, col[:, None] * inv], axis=-1)
    reps = LANES // head_dim
    cos = np.tile(np.concatenate([np.cos(ang), np.cos(ang)], axis=-1), (1, reps))
    sin = np.tile(np.concatenate([-np.sin(ang), np.sin(ang)], axis=-1), (1, reps))
    return jnp.asarray(cos, F32), jnp.asarray(sin, F32)


def _filter_feats(n):
    t = np.linspace(0.0, 1.0, n)[:, None]
    w = (2.0 * math.pi / n) * np.arange(n)[:, None]
    f = np.linspace(1e-4, FILT_BANDS - 1, FILT_BANDS)[None, :]
    feats = np.concatenate([t, np.cos(f * w), -np.sin(f * w)], axis=-1)
    feats = np.pad(feats, ((0, 0), (0, LANES - FILT_EMB)))
    deltas = np.abs(np.linspace(HY_MIN_DECAY, HY_MAX_DECAY, HY_CH))
    decay = np.exp(-t * deltas[None, :])
    return jnp.asarray(feats, F32), jnp.asarray(decay, F32)


def _dense_conv_tables(n):
    big_l = 2 * n
    k = np.arange(big_l)[:, None]
    t = np.arange(n)[None, :]
    th = 2.0 * math.pi * k * t / big_l
    fwd = np.concatenate([np.cos(th), -np.sin(th)], axis=0)
    inv = np.concatenate([np.cos(th).T, -np.sin(th).T], axis=1) / big_l
    return (jnp.asarray(np.cos(th), F32), jnp.asarray(np.sin(th), F32), _bf16_table(fwd), _bf16_table(inv))


def _hyena_fft_tables(n):
    big_l = 2 * n
    n2 = HY_N2
    n1 = big_l // n2
    n1h = n1 // 2
    k1 = np.arange(n1)
    j1 = np.arange(n1h)
    j2 = np.arange(n2)
    k2 = np.arange(n2)
    th = 2.0 * math.pi * k1[None, :, None] * (j1[None, None, :] * n2 + j2[:, None, None]) / big_l
    m1 = np.concatenate([np.cos(th), -np.sin(th)], axis=1)
    th2 = 2.0 * math.pi * k2[:, None] * j2[None, :] / n2
    c2, s2 = np.cos(th2), np.sin(th2)
    w2 = np.block([[c2, s2], [-s2, c2]])
    tha = 2.0 * math.pi * j2[None, :, None] * (k1[:, None, None] + n1 * k2[None, None, :]) / big_l
    ca, sa = np.cos(tha), np.sin(tha)
    ma = np.concatenate([np.concatenate([ca, -sa], axis=2), np.concatenate([sa, ca], axis=2)], axis=1)
    thb = 2.0 * math.pi * j1[:, None] * k1[None, :] / n1
    mb = np.concatenate([np.cos(thb), -np.sin(thb)], axis=1) / big_l
    return _bf16_table(m1), _bf16_table(w2), _bf16_table(ma), _bf16_table(mb)


def _fnet_channel_table(n):
    m = np.arange(FN_GROUP_CH)
    th = 2.0 * math.pi * m[:, None] * m[None, :] / FN_GROUP_CH
    eye = np.eye(GROUP_W // FN_GROUP_CH)
    sc = 1.0 / math.sqrt(n * FN_GROUP_CH)
    cs = np.concatenate([np.kron(eye, np.cos(th)), np.kron(eye, np.sin(th))], axis=1) * sc
    return _bf16_table(cs)


def _fnet_dense_table(n):
    k = np.arange(n)
    th = 2.0 * math.pi * k[:, None] * k[None, :] / n
    return _bf16_table(np.concatenate([np.cos(th), -np.sin(th)], axis=1))


def _fnet_fft_tables(n):
    n2 = FN_N2
    n1 = n // n2
    k1 = np.arange(n1)
    j1 = np.arange(n1)
    j2 = np.arange(n2)
    k2 = np.arange(n2)
    th = 2.0 * math.pi * k1[None, :, None] * (j1[None, None, :] * n2 + j2[:, None, None]) / n
    c, s = np.cos(th), np.sin(th)
    f1 = np.concatenate([np.concatenate([c, -s], axis=2), np.concatenate([s, c], axis=2)], axis=1)
    th2 = 2.0 * math.pi * k2[:, None] * j2[None, :] / n2
    f2 = np.concatenate([np.cos(th2), -np.sin(th2)], axis=1)
    return _bf16_table(f1), _bf16_table(f2)


def _mod_kernel(cond_ref, w_ref, b_ref, o_ref):
    c = cond_ref[...]
    s = c * jax.nn.sigmoid(c)
    o_ref[...] = _dot3(s, w_ref[...]) + b_ref[...]


def _modulation(cond, w_ada, b_ada):
    tn = 1024
    return pl.pallas_call(
        _mod_kernel,
        grid=(DEPTH, 3 * D_MODEL // tn),
        in_specs=[pl.BlockSpec((COND_ROWS, D_MODEL), lambda l, j: (0, 0)),
                  pl.BlockSpec((None, D_MODEL, tn), lambda l, j: (l, 0, j)),
                  pl.BlockSpec((None, 1, tn), lambda l, j: (l, 0, j))],
        out_specs=pl.BlockSpec((None, COND_ROWS, tn), lambda l, j: (l, 0, j)),
        out_shape=jax.ShapeDtypeStruct((DEPTH, COND_ROWS, 3 * D_MODEL), F32),
        compiler_params=_cparams("arbitrary", "arbitrary"),
        name="modulation",
    )(cond, w_ada, b_ada.reshape(DEPTH, 1, 3 * D_MODEL))


def _rope(x, cos, sin, half):
    lane = lax.broadcasted_iota(jnp.int32, x.shape, 1)
    first = (lane % (2 * half)) < half
    partner = jnp.where(first, pltpu.roll(x, LANES - half, 1), pltpu.roll(x, half, 1))
    return x * cos + partner * sin


def _entry_kernel(*refs, latent, tiles_per_b):
    if latent:
        (x_ref, mod_ref, g_ref, w_ref, ca_ref, sa_ref, cd_ref, sd_ref,
         aq_ref, akt_ref, av_ref, dq_ref, dkt_ref, dv_ref, hu_ref, fu_ref, sg_ref) = refs
        row = pl.program_id(0) // tiles_per_b
    else:
        (x_ref, mod_ref, g_ref, w_ref,
         aq_ref, akt_ref, av_ref, dq_ref, dkt_ref, dv_ref, hu_ref, fu_ref, sg_ref,
         ak32_ref, av32_ref, dk32_ref, dv32_ref) = refs
        row = CTX_ROW
    m = mod_ref[pl.ds(row, 1), :]
    shift, scale = m[:, :D_MODEL], m[:, D_MODEL:2 * D_MODEL]
    x = x_ref[...]
    h = x * lax.rsqrt(jnp.mean(x * x, axis=-1, keepdims=True) + EPS) * g_ref[...]
    hb = (h * (1.0 + scale) + shift).astype(BF16)

    def proj(c0, c1):
        return _dot(hb, w_ref[:, c0:c1])

    def roped(p, cos_ref, sin_ref, half):
        if not latent:
            return p
        cos, sin = cos_ref[...], sin_ref[...]
        chunks = [_rope(p[:, j:j + LANES], cos, sin, half) for j in range(0, p.shape[1], LANES)]
        return chunks[0] if len(chunks) == 1 else jnp.concatenate(chunks, axis=1)

    ca = sa = cd = sd = None
    if latent:
        ca, sa, cd, sd = ca_ref, sa_ref, cd_ref, sd_ref

    aq_ref[...] = (roped(proj(C_AQ, C_AK), ca, sa, ATT_HEAD_DIM // 2) * (ATT_HEAD_DIM ** -0.5)).astype(BF16)
    ak = roped(proj(C_AK, C_AV), ca, sa, ATT_HEAD_DIM // 2)
    akt_ref[...] = ak.T.astype(BF16)
    av = proj(C_AV, C_AG)
    av_ref[...] = av.astype(BF16)
    dq_ref[...] = (roped(proj(C_DQ, C_DK), cd, sd, DIF_QK_DIM // 2) * (DIF_QK_DIM ** -0.5 * LOG2_E)).astype(BF16)
    dk = roped(proj(C_DK, C_DV), cd, sd, DIF_QK_DIM // 2)
    dkt_ref[...] = dk.T.astype(BF16)
    dve = proj(C_DVE, C_DVE + 2 * GROUP_W)
    ones_lane = (lax.broadcasted_iota(jnp.int32, dve.shape, 1) % (2 * DIF_V_DIM)) >= DIF_V_DIM
    dv_ref[...] = jnp.where(ones_lane, 1.0, dve).astype(BF16)
    if not latent:
        ak32_ref[...] = ak
        av32_ref[...] = av
        dk32_ref[...] = dk
        dv32_ref[...] = proj(C_DV, C_DG)
    hu_ref[...] = proj(C_HU, C_HG).astype(BF16)
    fu_ref[...] = proj(C_FU, C_FG).astype(BF16)
    for j, c0 in enumerate((C_AG, C_DG, C_HG, C_FG)):
        g = proj(c0, c0 + GROUP_W)
        sg_ref[:, j * GROUP_W:(j + 1) * GROUP_W] = (g * jax.nn.sigmoid(g)).astype(BF16)


def _layer_entry(x2d, bsz, n, mod_all, g_pre, w_in_bf, layer, rope):
    latent = rope is not None
    t = bsz * n
    tm = 512 if latent else n
    tiles_per_b = n // tm
    grid = (t // tm,)
    row = lambda i: (i, 0)
    kt_map = lambda i: (i // tiles_per_b, 0, i % tiles_per_b)
    in_specs = [pl.BlockSpec((tm, D_MODEL), row),
                pl.BlockSpec((None, COND_ROWS, 3 * D_MODEL), lambda i: (layer, 0, 0)),
                pl.BlockSpec((None, 1, D_MODEL), lambda i: (layer, 0, 0)),
                pl.BlockSpec((None, D_MODEL, D_IN_EXT), lambda i: (layer, 0, 0))]
    args = [x2d, mod_all, g_pre.reshape(DEPTH, 1, D_MODEL), w_in_bf]
    if latent:
        pos = lambda i: (i % tiles_per_b, 0)
        in_specs += [pl.BlockSpec((tm, LANES), pos)] * 4
        args += list(rope)
    bf = lambda c: jax.ShapeDtypeStruct((t, c), BF16)
    out_shape = [bf(256), jax.ShapeDtypeStruct((bsz, 128, n), BF16), bf(128),
                 bf(256), jax.ShapeDtypeStruct((bsz, 256, n), BF16), bf(512),
                 bf(768), bf(256), bf(1024)]
    rs = lambda c: pl.BlockSpec((tm, c), row)
    out_specs = [rs(256), pl.BlockSpec((None, 128, tm), kt_map), rs(128),
                 rs(256), pl.BlockSpec((None, 256, tm), kt_map), rs(512),
                 rs(768), rs(256), rs(1024)]
    if not latent:
        out_shape += [jax.ShapeDtypeStruct((t, c), F32) for c in (128, 128, 256, 256)]
        out_specs += [rs(128), rs(128), rs(256), rs(256)]
    return pl.pallas_call(
        functools.partial(_entry_kernel, latent=latent, tiles_per_b=tiles_per_b),
        grid=grid, in_specs=in_specs, out_specs=out_specs, out_shape=out_shape,
        compiler_params=_cparams("arbitrary"),
        name="entry_latent" if latent else "entry_ctx",
    )(*args)


def _softmax_pieces(scores, sink_col=None):
    m = functools.reduce(jnp.maximum, [jnp.max(s, axis=-1, keepdims=True) for s in scores])
    if sink_col is not None:
        m = jnp.maximum(m, sink_col)
    ps = [jnp.exp(s - m) for s in scores]
    l = functools.reduce(lambda a, b: a + b, [jnp.sum(p, axis=-1, keepdims=True) for p in ps])
    if sink_col is not None:
        l = l + jnp.exp(sink_col - m)
    return ps, l


def _attn_a_kernel(sink_ref, *refs, layer, window, nb):
    if window:
        (q_ref, kp_ref, kc_ref, kn_ref, vp_ref, vc_ref, vn_ref, ck_ref, cv_ref, sg_ref, o_ref) = refs
    else:
        (q_ref, ck_ref, cv_ref, sg_ref, o_ref) = refs
    tq = q_ref.shape[0]
    grp = ATT_HEADS // ATT_KV_HEADS
    rows = grp * tq
    if window:
        i = pl.program_id(1)
        r = lax.broadcasted_iota(jnp.int32, (rows, 3 * BLOCK), 0) % tq
        c = lax.broadcasted_iota(jnp.int32, (rows, 3 * BLOCK), 1)
        valid = ((c >= r) & (c <= r + 2 * WINDOW)
                 & ((c >= BLOCK) | (i > 0)) & ((c < 2 * BLOCK) | (i < nb - 1)))
        kw = jnp.concatenate([kp_ref[...], kc_ref[...], kn_ref[...]], axis=1)
        vw = jnp.concatenate([vp_ref[...], vc_ref[...], vn_ref[...]], axis=0)
    head_row = lax.broadcasted_iota(jnp.int32, (rows, 1), 0) // tq
    q = q_ref[...]
    ck = ck_ref[...]
    cv = cv_ref[...]
    outs = []
    for g in range(ATT_KV_HEADS):
        d0 = g * ATT_HEAD_DIM
        q2 = jnp.concatenate([q[:, (g * grp + j) * ATT_HEAD_DIM:(g * grp + j + 1) * ATT_HEAD_DIM]
                              for j in range(grp)], axis=0)
        sink_col = jnp.zeros((rows, 1), F32)
        for j in range(grp):
            sink_col = jnp.where(head_row == j, sink_ref[layer * ATT_HEADS + g * grp + j], sink_col)
        scores, vals = [], []
        if window:
            s_w = _dot(q2, kw[d0:d0 + ATT_HEAD_DIM, :])
            scores.append(jnp.where(valid, s_w, NEG_INF))
            vals.append(vw[:, d0:d0 + ATT_HEAD_DIM])
        scores.append(_dot(q2, ck[d0:d0 + ATT_HEAD_DIM, :]))
        vals.append(cv[:, d0:d0 + ATT_HEAD_DIM])
        ps, l = _softmax_pieces(scores, sink_col)
        o = functools.reduce(lambda a, b: a + b, [_dot(p.astype(BF16), v) for p, v in zip(ps, vals)]) / l
        outs += [o[j * tq:(j + 1) * tq] for j in range(grp)]
    o_full = jnp.concatenate(outs, axis=1)
    o_ref[...] = (o_full * sg_ref[...].astype(F32)).astype(BF16)


def _attn_a_latent(aq, akt, av, ck_t, cv, sg, sink, bsz, n, layer):
    nb = n // BLOCK
    qmap = lambda b, i: (b * nb + i, 0)
    prev = lambda i: jnp.maximum(i - 1, 0)
    nxt = lambda i: jnp.minimum(i + 1, nb - 1)
    kspec = lambda f: pl.BlockSpec((None, 128, BLOCK), lambda b, i: (b, 0, f(i)))
    vspec = lambda f: pl.BlockSpec((BLOCK, 128), lambda b, i: (b * nb + f(i), 0))
    same = lambda i: i
    return pl.pallas_call(
        functools.partial(_attn_a_kernel, layer=layer, window=True, nb=nb),
        grid=(bsz, nb),
        in_specs=[pl.BlockSpec(memory_space=pltpu.SMEM),
                  pl.BlockSpec((BLOCK, 256), qmap),
                  kspec(prev), kspec(same), kspec(nxt),
                  vspec(prev), vspec(same), vspec(nxt),
                  pl.BlockSpec((None, 128, ck_t.shape[2]), lambda b, i: (b, 0, 0)),
                  pl.BlockSpec((None, cv.shape[1], 128), lambda b, i: (b, 0, 0)),
                  pl.BlockSpec((BLOCK, 256), lambda b, i: (b * nb + i, 0))],
        out_specs=pl.BlockSpec((BLOCK, 256), qmap),
        out_shape=jax.ShapeDtypeStruct((bsz * n, 256), BF16),
        compiler_params=_cparams("arbitrary", "arbitrary"),
        name="attn_a_latent",
    )(sink, aq, akt, akt, akt, av, av, av, ck_t, cv, sg)


def _attn_a_ctx(aq, akt, av, sg, sink, bsz, n, layer):
    return pl.pallas_call(
        functools.partial(_attn_a_kernel, layer=layer, window=False, nb=1),
        grid=(bsz,),
        in_specs=[pl.BlockSpec(memory_space=pltpu.SMEM),
                  pl.BlockSpec((n, 256), lambda b: (b, 0)),
                  pl.BlockSpec((None, 128, n), lambda b: (b, 0, 0)),
                  pl.BlockSpec((n, 128), lambda b: (b, 0)),
                  pl.BlockSpec((n, 256), lambda b: (b, 0))],
        out_specs=pl.BlockSpec((n, 256), lambda b: (b, 0)),
        out_shape=jax.ShapeDtypeStruct((bsz * n, 256), BF16),
        compiler_params=_cparams("arbitrary"),
        name="attn_a_ctx",
    )(sink, aq, akt, av, sg)


def _diff_kernel(lam_ref, subln_ref, q_ref, *refs, lam_init, npieces, tq_sub, kc):
    kts = refs[0:2 * npieces:2]
    vs = refs[1:2 * npieces:2]
    sg_ref, o_ref, s_scr, p_scr = refs[2 * npieces:2 * npieces + 4]
    lp = lam_ref[...]
    lam = (jnp.exp(jnp.sum(lp[0:1] * lp[1:2], axis=-1, keepdims=True))
           - jnp.exp(jnp.sum(lp[2:3] * lp[3:4], axis=-1, keepdims=True)) + lam_init)
    subln = subln_ref[...]
    tq = q_ref.shape[0]
    chunks, off = [], 0
    for kt, v in zip(kts, vs):
        for c0 in range(0, kt.shape[1], kc):
            w = min(kc, kt.shape[1] - c0)
            chunks.append((kt, v, c0, w, off))
            off += w
    items = [(r0, h, mp) for r0 in range(0, tq, tq_sub) for h in range(DIF_HEADS) for mp in range(2)]
    n_items = len(items)
    mx, acc, om, outs = {}, {}, {}, {}

    def fold(s):
        return functools.reduce(jnp.maximum, [s[:, j:j + LANES] for j in range(0, s.shape[1], LANES)])

    for step in range(n_items + 2):
        a = step if step < n_items else None
        b = step - 1 if 0 <= step - 1 < n_items else None
        c = step - 2 if 0 <= step - 2 < n_items else None
        if a is not None:
            r0, h, mp = items[a]
            f0 = (mp * DIF_HEADS + h) * DIF_QK_DIM
            q_a = q_ref[r0:r0 + tq_sub, f0:f0 + DIF_QK_DIM]
            mx[a] = jnp.full((tq_sub, LANES), NEG_INF, F32)
        if b is not None:
            m_b = jnp.max(mx.pop(b), axis=-1, keepdims=True)
        if c is not None:
            acc[c] = jnp.zeros((tq_sub, 2 * DIF_V_DIM), F32)
            hc = items[c][1]
        for kt, v, c0, w, o0 in chunks:
            if a is not None:
                s = _dot(q_a, kt[f0:f0 + DIF_QK_DIM, c0:c0 + w])
                s_scr[a % 2, :, o0:o0 + w] = s
                mx[a] = jnp.maximum(mx[a], fold(s))
            if b is not None:
                p_scr[b % 2, :, o0:o0 + w] = jnp.exp2(s_scr[b % 2, :, o0:o0 + w] - m_b).astype(BF16)
            if c is not None:
                acc[c] = acc[c] + _dot(p_scr[c % 2, :, o0:o0 + w],
                                       v[c0:c0 + w, 2 * DIF_V_DIM * hc:2 * DIF_V_DIM * (hc + 1)])
        if c is not None:
            r0, h, mp = items[c]
            o = acc.pop(c)
            om[(r0, h, mp)] = o[:, :DIF_V_DIM] / o[:, DIF_V_DIM:DIF_V_DIM + 1]
            if mp == 1:
                av = om.pop((r0, h, 0)) - lam * om.pop((r0, h, 1))
                y = av * lax.rsqrt(jnp.mean(av * av, axis=-1, keepdims=True) + EPS) * subln * (1.0 - lam_init)
                outs.setdefault(r0, []).append(y)
                if h == DIF_HEADS - 1:
                    o_full = jnp.concatenate(outs.pop(r0), axis=1)
                    o_ref[r0:r0 + tq_sub, :] = (o_full * sg_ref[r0:r0 + tq_sub, :].astype(F32)).astype(BF16)


def _diff_attention(dq, pieces, sg, diff_lambda, diff_subln, bsz, n, layer, tq, kc=512):
    lam_init = 0.8 - 0.6 * math.exp(-0.3 * layer)
    nt = n // tq
    nk_all = sum(kt.shape[2] for kt, _ in pieces)
    in_specs = [pl.BlockSpec((None, 4, DIF_QK_DIM), lambda b, i: (layer, 0, 0)),
                pl.BlockSpec((None, 1, DIF_V_DIM), lambda b, i: (layer, 0, 0)),
                pl.BlockSpec((tq, 256), lambda b, i: (b * nt + i, 0))]
    args = [diff_lambda, diff_subln.reshape(DEPTH, 1, DIF_V_DIM), dq]
    for kt, v in pieces:
        nk = kt.shape[2]
        in_specs.append(pl.BlockSpec((None, 256, nk), lambda b, i: (b, 0, 0)))
        if v.ndim == 3:
            in_specs.append(pl.BlockSpec((None, nk, 512), lambda b, i: (b, 0, 0)))
        else:
            in_specs.append(pl.BlockSpec((nk, 512), lambda b, i: (b, 0)))
        args += [kt, v]
    in_specs.append(pl.BlockSpec((tq, 256), lambda b, i: (b * nt + i, 1)))
    args.append(sg)
    return pl.pallas_call(
        functools.partial(_diff_kernel, lam_init=lam_init, npieces=len(pieces), tq_sub=tq, kc=kc),
        grid=(bsz, nt), in_specs=in_specs,
        out_specs=pl.BlockSpec((tq, 256), lambda b, i: (b * nt + i, 0)),
        out_shape=jax.ShapeDtypeStruct((bsz * n, 256), BF16),
        scratch_shapes=[pltpu.VMEM((2, tq, nk_all), F32), pltpu.VMEM((2, tq, nk_all), BF16)],
        compiler_params=_cparams("arbitrary", "arbitrary"),
        name="diff_attn_%d" % len(pieces),
    )(*args)


def _filter_core(feats_ref, w1_ref, b1_ref, w2_ref, b2_ref, w3_ref, fr_ref, decay_ref):
    fr = fr_ref[...]
    h = jnp.sin(fr * (_dot3(feats_ref[...], w1_ref[...]) + b1_ref[...]))
    h = jnp.sin(fr * (_dot3(h, w2_ref[...]) + b2_ref[...]))
    h = _dot3(h, w3_ref[...])
    decay = decay_ref[...]
    hf = h[:, :HY_CH] * decay
    hb = h[:, HY_CH:] * decay
    tot = (jnp.sum(jnp.abs(hf), axis=0, keepdims=True) + jnp.sum(jnp.abs(hb), axis=0, keepdims=True) + EPS)
    hf = hf / tot
    hb = hb / tot
    rowi = lax.broadcasted_iota(jnp.int32, hb.shape, 0)
    return hf, jnp.where(rowi == 0, 0.0, hb)


def _filter_ctx_kernel(feats_ref, w1_ref, b1_ref, w2_ref, b2_ref, w3_ref, fr_ref, decay_ref,
                       cos_ref, sin_ref, kr_ref, ki_ref):
    hf, hb0 = _filter_core(feats_ref, w1_ref, b1_ref, w2_ref, b2_ref, w3_ref, fr_ref, decay_ref)
    kr_ref[...] = _dot3(cos_ref[...], hf + hb0)
    ki_ref[...] = _dot3(sin_ref[...], hb0 - hf)


def _filter_lat_kernel(feats_ref, w1_ref, b1_ref, w2_ref, b2_ref, w3_ref, fr_ref, decay_ref, hf_ref, hb_ref):
    hf, hb0 = _filter_core(feats_ref, w1_ref, b1_ref, w2_ref, b2_ref, w3_ref, fr_ref, decay_ref)
    hf_ref[...] = hf
    hb_ref[...] = hb0


def _filter_specs(n, w1p, b1, w2, b2, w3, freq, feats, decay):
    const = lambda shape: pl.BlockSpec(shape, lambda l: (0,) * len(shape))
    lay = lambda a, b: pl.BlockSpec((None, a, b), lambda l: (l, 0, 0))
    in_specs = [const((n, LANES)), lay(LANES, FILT_HIDDEN), lay(1, FILT_HIDDEN),
                lay(FILT_HIDDEN, FILT_HIDDEN), lay(1, FILT_HIDDEN), lay(FILT_HIDDEN, 2 * HY_CH),
                lay(1, FILT_HIDDEN), const((n, HY_CH))]
    args = [feats, w1p, b1.reshape(DEPTH, 1, FILT_HIDDEN), w2, b2.reshape(DEPTH, 1, FILT_HIDDEN), w3,
            freq.reshape(DEPTH, 1, FILT_HIDDEN), decay]
    return in_specs, args


def _hyena_filters_ctx(n, filt, cos_t, sin_t):
    feats, decay = _filter_feats(n)
    in_specs, args = _filter_specs(n, *filt, feats, decay)
    big_l = 2 * n
    in_specs += [pl.BlockSpec((big_l, n), lambda l: (0, 0))] * 2
    out = pl.BlockSpec((None, big_l, HY_CH), lambda l: (l, 0, 0))
    return pl.pallas_call(
        _filter_ctx_kernel, grid=(DEPTH,), in_specs=in_specs, out_specs=[out, out],
        out_shape=[jax.ShapeDtypeStruct((DEPTH, big_l, HY_CH), F32)] * 2,
        compiler_params=_cparams("arbitrary"), name="hyena_filter_ctx",
    )(*args, cos_t, sin_t)


def _hyena_filters_lat(n, filt):
    feats, decay = _filter_feats(n)
    in_specs, args = _filter_specs(n, *filt, feats, decay)
    out = pl.BlockSpec((None, n, HY_CH), lambda l: (l, 0, 0))
    return pl.pallas_call(
        _filter_lat_kernel, grid=(DEPTH,), in_specs=in_specs, out_specs=[out, out],
        out_shape=[jax.ShapeDtypeStruct((DEPTH, n, HY_CH), F32)] * 2,
        compiler_params=_cparams("arbitrary"), name="hyena_filter_lat",
    )(*args)


def _short_conv(u_ref, cw_ref, cb_ref):
    u = u_ref[...].astype(F32)
    n = u.shape[0]
    rowi = lax.broadcasted_iota(jnp.int32, u.shape, 0)
    up = jnp.where(rowi == 0, 0.0, pltpu.roll(u, 1, 0))
    un = jnp.where(rowi == n - 1, 0.0, pltpu.roll(u, n - 1, 0))
    w = cw_ref[...]
    return up * w[0:1] + u * w[1:2] + un * w[2:3] + cb_ref[...]


def _hyena_ctx_kernel(hu_ref, cw_ref, cb_ref, kr_ref, ki_ref, fwd_ref, inv_ref, skip_ref, sg_ref, o_ref):
    uc = _short_conv(hu_ref, cw_ref, cb_ref)
    x0, x1, v = uc[:, :HY_CH], uc[:, HY_CH:2 * HY_CH], uc[:, 2 * HY_CH:]
    z = x1 * v
    zf = _dot(fwd_ref[...], z.astype(BF16))
    big_l = zf.shape[0] // 2
    zr, zi = zf[:big_l], zf[big_l:]
    kr, ki = kr_ref[...], ki_ref[...]
    p = jnp.concatenate([zr * kr - zi * ki, zr * ki + zi * kr], axis=0).astype(BF16)
    y = _dot(inv_ref[...], p)
    out = x0 * (y + z * skip_ref[...])
    o_ref[...] = (out * sg_ref[...].astype(F32)).astype(BF16)


def _hyena_ctx(hu, sg, conv_w, conv_b, skip, kr, ki, fwd_t, inv_t, bsz, n, layer):
    big_l = 2 * n
    lay = lambda a, b: pl.BlockSpec((None, a, b), lambda i: (layer, 0, 0))
    return pl.pallas_call(
        _hyena_ctx_kernel, grid=(bsz,),
        in_specs=[pl.BlockSpec((n, 768), lambda i: (i, 0)),
                  lay(3, 768), lay(1, 768), lay(big_l, HY_CH), lay(big_l, HY_CH),
                  pl.BlockSpec((2 * big_l, n), lambda i: (0, 0)),
                  pl.BlockSpec((n, 2 * big_l), lambda i: (0, 0)),
                  lay(1, HY_CH),
                  pl.BlockSpec((n, 256), lambda i: (i, 2))],
        out_specs=pl.BlockSpec((n, 256), lambda i: (i, 0)),
        out_shape=jax.ShapeDtypeStruct((bsz * n, 256), BF16),
        compiler_params=_cparams("arbitrary"), name="hyena_ctx",
    )(hu, conv_w, conv_b.reshape(DEPTH, 1, 768), kr, ki, fwd_t, inv_t, skip.reshape(DEPTH, 1, HY_CH), sg)


def _fft_stage1(load_rows, m1_ref, s_re, s_im, n1, n2):
    for j2 in range(n2):
        a = _dot(m1_ref[j2], load_rows(j2).astype(BF16))
        s_re[j2 * n1:(j2 + 1) * n1, :] = a[:n1]
        s_im[j2 * n1:(j2 + 1) * n1, :] = a[n1:]


def _fft_stage2(w2_ref, s_re, s_im, k1, n1, n2):
    re = s_re[pl.ds(k1, n2, stride=n1), :]
    im = s_im[pl.ds(k1, n2, stride=n1), :]
    x = _dot(w2_ref[...], jnp.concatenate([re, im], axis=0).astype(BF16))
    return x[:n2], x[n2:]


def _filter_fft_kernel(hf_ref, hb_ref, m1_ref, w2_ref, kr_ref, ki_ref, s_re, s_im, *, n1, n2):
    n1h = n1 // 2
    _fft_stage1(lambda j2: hf_ref[pl.ds(j2, n1h, stride=n2), :], m1_ref, s_re, s_im, n1, n2)
    for k1 in range(n1):
        xr, xi = _fft_stage2(w2_ref, s_re, s_im, k1, n1, n2)
        kr_ref[k1 * n2:(k1 + 1) * n2, :] = xr
        ki_ref[k1 * n2:(k1 + 1) * n2, :] = xi
    _fft_stage1(lambda j2: hb_ref[pl.ds(j2, n1h, stride=n2), :], m1_ref, s_re, s_im, n1, n2)
    for k1 in range(n1):
        xr, xi = _fft_stage2(w2_ref, s_re, s_im, k1, n1, n2)
        kr_ref[k1 * n2:(k1 + 1) * n2, :] += xr
        ki_ref[k1 * n2:(k1 + 1) * n2, :] -= xi


def _filter_fft(hf, hb0, m1, w2, n):
    big_l = 2 * n
    n2 = HY_N2
    n1 = big_l // n2
    inp = pl.BlockSpec((None, n, LANES), lambda l, c: (l, 0, c))
    out = pl.BlockSpec((None, big_l, LANES), lambda l, c: (l, 0, c))
    return pl.pallas_call(
        functools.partial(_filter_fft_kernel, n1=n1, n2=n2),
        grid=(DEPTH, HY_CH // LANES),
        in_specs=[inp, inp,
                  pl.BlockSpec(m1.shape, lambda l, c: (0, 0, 0)),
                  pl.BlockSpec(w2.shape, lambda l, c: (0, 0))],
        out_specs=[out, out],
        out_shape=[jax.ShapeDtypeStruct((DEPTH, big_l, HY_CH), F32)] * 2,
        scratch_shapes=[pltpu.VMEM((big_l, LANES), F32)] * 2,
        compiler_params=_cparams("arbitrary", "arbitrary"), name="hyena_filter_fft",
    )(hf, hb0, m1, w2)


def _hyena_lat_kernel(x0_ref, x1_ref, v_ref, cw0_ref, cw1_ref, cw2_ref, cb0_ref, cb1_ref, cb2_ref,
                      kr_ref, ki_ref, m1_ref, w2_ref, ma_ref, mb_ref, skip_ref, sg_ref, o_ref,
                      z_scr, y_scr, s_re, s_im, *, n1, n2):
    n1h = n1 // 2
    z_scr[...] = _short_conv(x1_ref, cw1_ref, cb1_ref) * _short_conv(v_ref, cw2_ref, cb2_ref)
    _fft_stage1(lambda j2: z_scr[pl.ds(j2, n1h, stride=n2), :], m1_ref, s_re, s_im, n1, n2)
    for k1 in range(n1):
        xr, xi = _fft_stage2(w2_ref, s_re, s_im, k1, n1, n2)
        kr = kr_ref[k1 * n2:(k1 + 1) * n2, :]
        ki = ki_ref[k1 * n2:(k1 + 1) * n2, :]
        p = jnp.concatenate([xr * kr - xi * ki, xr * ki + xi * kr], axis=0).astype(BF16)
        b = _dot(ma_ref[k1], p)
        s_re[pl.ds(k1, n2, stride=n1), :] = b[:n2]
        s_im[pl.ds(k1, n2, stride=n1), :] = b[n2:]
    for j2 in range(n2):
        blk = jnp.concatenate([s_re[j2 * n1:(j2 + 1) * n1, :], s_im[j2 * n1:(j2 + 1) * n1, :]], axis=0)
        y_scr[pl.ds(j2, n1h, stride=n2), :] = _dot(mb_ref[...], blk.astype(BF16))
    z = z_scr[...]
    out = _short_conv(x0_ref, cw0_ref, cb0_ref) * (y_scr[...] + z * skip_ref[...])
    o_ref[...] = (out * sg_ref[...].astype(F32)).astype(BF16)


def _hyena_lat(hu, sg, conv_w, conv_b, skip, kr, ki, tables, bsz, n, layer):
    m1, w2, ma, mb = tables
    big_l = 2 * n
    n2 = HY_N2
    n1 = big_l // n2
    nch = HY_CH // LANES
    ucol = lambda s: pl.BlockSpec((n, LANES), lambda c, b: (b, s * nch + c))
    wcol = lambda s: pl.BlockSpec((None, 3, LANES), lambda c, b: (layer, 0, s * nch + c))
    bcol = lambda s: pl.BlockSpec((None, 1, LANES), lambda c, b: (layer, 0, s * nch + c))
    kspec = pl.BlockSpec((None, big_l, LANES), lambda c, b: (layer, 0, c))
    const = lambda a: pl.BlockSpec(a.shape, lambda c, b: (0,) * a.ndim)
    cb3 = conv_b.reshape(DEPTH, 1, 768)
    return pl.pallas_call(
        functools.partial(_hyena_lat_kernel, n1=n1, n2=n2),
        grid=(nch, bsz),
        in_specs=[ucol(0), ucol(1), ucol(2), wcol(0), wcol(1), wcol(2), bcol(0), bcol(1), bcol(2),
                  kspec, kspec, const(m1), const(w2), const(ma), const(mb),
                  pl.BlockSpec((None, 1, LANES), lambda c, b: (layer, 0, c)),
                  pl.BlockSpec((n, LANES), lambda c, b: (b, 2 * nch + c))],
        out_specs=pl.BlockSpec((n, LANES), lambda c, b: (b, c)),
        out_shape=jax.ShapeDtypeStruct((bsz * n, 256), BF16),
        scratch_shapes=[pltpu.VMEM((n, LANES), F32), pltpu.VMEM((n, LANES), F32),
                        pltpu.VMEM((big_l, LANES), F32), pltpu.VMEM((big_l, LANES), F32)],
        compiler_params=_cparams("arbitrary", "arbitrary"), name="hyena_lat",
    )(hu, hu, hu, conv_w, conv_w, conv_w, cb3, cb3, cb3, kr, ki, m1, w2, ma, mb,
      skip.reshape(DEPTH, 1, HY_CH), sg)


def _fnet_ctx_kernel(u_ref, cs_ref, dn_ref, fw_ref, fb_ref, sg_ref, o_ref):
    ab = _dot(u_ref[...], cs_ref[...])
    stack = jnp.concatenate([ab[:, :GROUP_W], ab[:, GROUP_W:]], axis=0).astype(BF16)
    f = _dot(dn_ref[...], stack)
    out = _bdot(f, fw_ref[...]) + fb_ref[...]
    o_ref[...] = (out * sg_ref[...].astype(F32)).astype(BF16)


def _fnet_ctx(fu, sg, fn_w, fn_b, cs, dn, bsz, n, layer):
    return pl.pallas_call(
        _fnet_ctx_kernel, grid=(bsz,),
        in_specs=[pl.BlockSpec((n, 256), lambda i: (i, 0)),
                  pl.BlockSpec(cs.shape, lambda i: (0, 0)),
                  pl.BlockSpec(dn.shape, lambda i: (0, 0)),
                  pl.BlockSpec((None, GROUP_W, GROUP_W), lambda i: (layer, 0, 0)),
                  pl.BlockSpec((None, 1, GROUP_W), lambda i: (layer, 0, 0)),
                  pl.BlockSpec((n, 256), lambda i: (i, 3))],
        out_specs=pl.BlockSpec((n, 256), lambda i: (i, 0)),
        out_shape=jax.ShapeDtypeStruct((bsz * n, 256), BF16),
        compiler_params=_cparams("arbitrary"), name="fnet_ctx",
    )(fu, cs, dn, fn_w, fn_b.reshape(DEPTH, 1, GROUP_W), sg)


def _fnet_lat_kernel(u_ref, cs_ref, f1_ref, f2_ref, fw_ref, fb_ref, sg_ref, o_ref,
                     a_scr, b_scr, s_re, s_im, o_scr, *, n1, n2):
    halves = GROUP_W // LANES

    def put(scr, rows, val):
        for hh in range(halves):
            scr[hh, rows, :] = val[:, hh * LANES:(hh + 1) * LANES]

    def get(scr, rows):
        return jnp.concatenate([scr[hh, rows, :] for hh in range(halves)], axis=1)

    ab = _dot(u_ref[...], cs_ref[...])
    put(a_scr, slice(None), ab[:, :GROUP_W])
    put(b_scr, slice(None), ab[:, GROUP_W:])
    for j2 in range(n2):
        rows = pl.ds(j2, n1, stride=n2)
        g = jnp.concatenate([get(a_scr, rows), get(b_scr, rows)], axis=0)
        t = _dot(f1_ref[j2], g.astype(BF16))
        put(s_re, slice(j2 * n1, (j2 + 1) * n1), t[:n1])
        put(s_im, slice(j2 * n1, (j2 + 1) * n1), t[n1:])
    for k1 in range(n1):
        rows = pl.ds(k1, n2, stride=n1)
        g = jnp.concatenate([get(s_re, rows), get(s_im, rows)], axis=0)
        put(o_scr, rows, _dot(f2_ref[...], g.astype(BF16)))
    out = _bdot(get(o_scr, slice(None)), fw_ref[...]) + fb_ref[...]
    o_ref[...] = (out * sg_ref[...].astype(F32)).astype(BF16)


def _fnet_lat(fu, sg, fn_w, fn_b, cs, f1, f2, bsz, n, layer):
    n2 = FN_N2
    n1 = n // n2
    return pl.pallas_call(
        functools.partial(_fnet_lat_kernel, n1=n1, n2=n2), grid=(bsz,),
        in_specs=[pl.BlockSpec((n, 256), lambda i: (i, 0)),
                  pl.BlockSpec(cs.shape, lambda i: (0, 0)),
                  pl.BlockSpec(f1.shape, lambda i: (0, 0, 0)),
                  pl.BlockSpec(f2.shape, lambda i: (0, 0)),
                  pl.BlockSpec((None, GROUP_W, GROUP_W), lambda i: (layer, 0, 0)),
                  pl.BlockSpec((None, 1, GROUP_W), lambda i: (layer, 0, 0)),
                  pl.BlockSpec((n, 256), lambda i: (i, 3))],
        out_specs=pl.BlockSpec((n, 256), lambda i: (i, 0)),
        out_shape=jax.ShapeDtypeStruct((bsz * n, 256), BF16),
        scratch_shapes=[pltpu.VMEM((GROUP_W // LANES, n, LANES), F32)] * 5,
        compiler_params=_cparams("arbitrary"), name="fnet_lat",
    )(fu, cs, f1, f2, fn_w, fn_b.reshape(DEPTH, 1, GROUP_W), sg)


def _exit_kernel(x_ref, oa_ref, ob_ref, oc_ref, od_ref, w_ref, g_ref, mod_ref, o_ref, *, latent, tiles_per_b):
    row = (pl.program_id(0) // tiles_per_b) if latent else CTX_ROW
    gate = mod_ref[pl.ds(row, 1), :][:, 2 * D_MODEL:]
    mixed = jnp.concatenate([oa_ref[...], ob_ref[...], oc_ref[...], od_ref[...]], axis=1)
    y = _dot(mixed, w_ref[...])
    y = y * lax.rsqrt(jnp.mean(y * y, axis=-1, keepdims=True) + EPS) * g_ref[...]
    o_ref[...] = x_ref[...] + gate * y


def _layer_exit(x2d, outs, w_out_bf, g_post, mod_all, n, layer, latent):
    t = x2d.shape[0]
    tm = 512
    tiles_per_b = n // tm if latent else 1
    row = lambda i: (i, 0)
    return pl.pallas_call(
        functools.partial(_exit_kernel, latent=latent, tiles_per_b=tiles_per_b),
        grid=(t // tm,),
        in_specs=[pl.BlockSpec((tm, D_MODEL), row)] + [pl.BlockSpec((tm, 256), row)] * 4
                 + [pl.BlockSpec((None, D_MODEL, D_MODEL), lambda i: (layer, 0, 0)),
                    pl.BlockSpec((None, 1, D_MODEL), lambda i: (layer, 0, 0)),
                    pl.BlockSpec((None, COND_ROWS, 3 * D_MODEL), lambda i: (layer, 0, 0))],
        out_specs=pl.BlockSpec((tm, D_MODEL), row),
        out_shape=jax.ShapeDtypeStruct((t, D_MODEL), F32),
        compiler_params=_cparams("arbitrary"),
        name="exit_latent" if latent else "exit_ctx",
    )(x2d, *outs, w_out_bf, g_post.reshape(DEPTH, 1, D_MODEL), mod_all)


def kernel(x_prompt, x_sample, cache_attn_k, cache_attn_v, cache_diff_k, cache_diff_v, c, c_ctx, w_ada, b_ada, norm_pre, norm_post, w_in, w_out, attn_sink, diff_lambda, diff_subln, hy_conv_w, hy_conv_b, hy_filt_w1, hy_filt_b1, hy_filt_w2, hy_filt_b2, hy_filt_w3, hy_filt_freq, hy_skip, fn_w, fn_b):
    bp, lp, _ = x_prompt.shape
    bs, ls, _ = x_sample.shape
    past = cache_attn_k.shape[2]
    assert bs < CTX_ROW + 1 <= COND_ROWS

    cond = jnp.concatenate([c, c_ctx[None, :], jnp.zeros((COND_ROWS - bs - 1, D_MODEL), F32)], axis=0)
    mod_all = _modulation(cond, w_ada, b_ada)
    w_dv = w_in[:, :, C_DV:C_DG].reshape(DEPTH, D_MODEL, DIF_HEADS, DIF_V_DIM)
    w_dve = jnp.pad(w_dv, ((0, 0), (0, 0), (0, 0), (0, DIF_V_DIM))).reshape(DEPTH, D_MODEL, 2 * GROUP_W)
    w_in_bf = jnp.concatenate([w_in, w_dve], axis=-1).astype(BF16)
    w_out_bf = w_out.astype(BF16)
    sink = attn_sink.reshape(DEPTH * ATT_HEADS)

    rope = _rope_tables(ls, ATT_HEAD_DIM) + _rope_tables(ls, DIF_QK_DIM)
    filt = (jnp.pad(hy_filt_w1, ((0, 0), (0, LANES - FILT_EMB), (0, 0))), hy_filt_b1, hy_filt_w2, hy_filt_b2,
            hy_filt_w3, hy_filt_freq)
    cos_c, sin_c, fwd_c, inv_c = _dense_conv_tables(lp)
    kr_c, ki_c = _hyena_filters_ctx(lp, filt, cos_c, sin_c)
    hy_tables = _hyena_fft_tables(ls)
    hf_l, hb_l = _hyena_filters_lat(ls, filt)
    kr_l, ki_l = _filter_fft(hf_l, hb_l, hy_tables[0], hy_tables[1], ls)
    cs_c, dn_c = _fnet_channel_table(lp), _fnet_dense_table(lp)
    cs_l = _fnet_channel_table(ls)
    f1_l, f2_l = _fnet_fft_tables(ls)

    cak = cache_attn_k.reshape(bs, DEPTH, past, 128).transpose(0, 1, 3, 2).astype(BF16)
    cav = cache_attn_v.reshape(bs, DEPTH, past, 128).astype(BF16)
    cdk = cache_diff_k.reshape(bs, DEPTH, past, 256).transpose(0, 1, 3, 2).astype(BF16)
    cdv = jnp.concatenate([cache_diff_v, jnp.ones_like(cache_diff_v)], axis=-1)
    cdv = cdv.reshape(bs, DEPTH, past, 2 * GROUP_W).astype(BF16)

    xp = x_prompt.reshape(bp * lp, D_MODEL)
    xs = x_sample.reshape(bs * ls, D_MODEL)
    st_ak, st_av, st_dk, st_dv = [], [], [], []
    for l in range(DEPTH):
        (aq, akt, av, dq, dkt, dv, hu, fu, sg, ak32, av32, dk32, dv32) = _layer_entry(
            xp, bp, lp, mod_all, norm_pre, w_in_bf, l, None)
        outs = (_attn_a_ctx(aq, akt, av, sg, sink, bp, lp, l),
                _diff_attention(dq, [(dkt, dv)], sg, diff_lambda, diff_subln, bp, lp, l, lp),
                _hyena_ctx(hu, sg, hy_conv_w, hy_conv_b, hy_skip, kr_c, ki_c, fwd_c, inv_c, bp, lp, l),
                _fnet_ctx(fu, sg, fn_w, fn_b, cs_c, dn_c, bp, lp, l))
        st_ak.append(ak32.reshape(bp, lp, ATT_KV_HEADS, ATT_HEAD_DIM))
        st_av.append(av32.reshape(bp, lp, ATT_KV_HEADS, ATT_HEAD_DIM))
        st_dk.append(dk32.reshape(bp, lp, 2, DIF_HEADS, DIF_QK_DIM))
        st_dv.append(dv32.reshape(bp, lp, DIF_HEADS, DIF_V_DIM))
        xp = _layer_exit(xp, outs, w_out_bf, norm_post, mod_all, lp, l, False)

        (aq, akt, av, dq, dkt, dv, hu, fu, sg) = _layer_entry(xs, bs, ls, mod_all, norm_pre, w_in_bf, l, rope)
        outs = (_attn_a_latent(aq, akt, av, cak[:, l], cav[:, l], sg, sink, bs, ls, l),
                _diff_attention(dq, [(dkt, dv), (cdk[:, l], cdv[:, l])], sg, diff_lambda, diff_subln,
                                bs, ls, l, 256),
                _hyena_lat(hu, sg, hy_conv_w, hy_conv_b, hy_skip, kr_l, ki_l, hy_tables, bs, ls, l),
                _fnet_lat(fu, sg, fn_w, fn_b, cs_l, f1_l, f2_l, bs, ls, l))
        xs = _layer_exit(xs, outs, w_out_bf, norm_post, mod_all, ls, l, True)

    return (xp.reshape(bp, lp, D_MODEL), xs.reshape(bs, ls, D_MODEL),
            jnp.stack(st_ak, axis=1), jnp.stack(st_av, axis=1), jnp.stack(st_dk, axis=1), jnp.stack(st_dv, axis=1))
```

```python
import functools
import math

import numpy as np
import jax
import jax.numpy as jnp
from jax import lax
from jax.experimental import pallas as pl
from jax.experimental.pallas import tpu as pltpu

F32 = jnp.float32
BF16 = jnp.bfloat16

D_MODEL = 1024
DEPTH = 2
GRID_W = 64
GROUP_W = 256
ATT_HEADS = 4
ATT_KV_HEADS = 2
ATT_HEAD_DIM = 64
WINDOW = 128
BLOCK = 128
DIF_HEADS = 4
DIF_V_DIM = 64
DIF_QK_DIM = 32
HY_CH = 256
FILT_BANDS = 16
FILT_EMB = 1 + 2 * FILT_BANDS
FILT_HIDDEN = 64
HY_MIN_DECAY = math.log(1e-2) / 1.5
HY_MAX_DECAY = math.log(1e-2) / 0.3
FN_GROUP_CH = 64
ROPE_BASE = 10000.0
EPS = 1e-6
NEG_INF = -1e30

C_AQ, C_AK, C_AV, C_AG = 0, 256, 384, 512
C_DQ, C_DK, C_DV, C_DG = 768, 1024, 1280, 1536
C_HU, C_HG, C_FU, C_FG, D_IN = 1792, 2560, 2816, 3072, 3328
C_DVE = D_IN
D_IN_EXT = D_IN + 2 * GROUP_W
LOG2_E = math.log2(math.e)

LANES = 128
COND_ROWS = 8
CTX_ROW = 4
VMEM_LIMIT = 56 * 1024 * 1024
PITCH_PAD = 8

HY_N2 = 64
FN_N2 = 64


def _cparams(*sem):
    return pltpu.CompilerParams(dimension_semantics=sem, vmem_limit_bytes=VMEM_LIMIT)


def _dot(a, b):
    return jnp.dot(a, b, preferred_element_type=F32)


def _bdot(a, b):
    return jnp.dot(a.astype(BF16), b.astype(BF16), preferred_element_type=F32)


def _dot3(a, b):
    ah = a.astype(BF16)
    al = (a - ah.astype(F32)).astype(BF16)
    bh = b.astype(BF16)
    bl = (b - bh.astype(F32)).astype(BF16)
    return _dot(ah, bh) + _dot(ah, bl) + _dot(al, bh)


def _bf16_table(a):
    return jnp.asarray(a, F32).astype(BF16)


def _rope_tables(n, head_dim):
    pos = np.arange(n)
    row = (pos // GRID_W).astype(np.float64)
    col = (pos % GRID_W).astype(np.float64)
    n_freq = head_dim // 4
    inv = ROPE_BASE ** (-np.arange(n_freq, dtype=np.float64) / n_freq)
    ang = np.concatenate([row[:, None] * inv, col[:, None] * inv], axis=-1)
    reps = LANES // head_dim
    cos = np.tile(np.concatenate([np.cos(ang), np.cos(ang)], axis=-1), (1, reps))
    sin = np.tile(np.concatenate([-np.sin(ang), np.sin(ang)], axis=-1), (1, reps))
    return jnp.asarray(cos, F32), jnp.asarray(sin, F32)


def _filter_feats(n):
    t = np.linspace(0.0, 1.0, n)[:, None]
    w = (2.0 * math.pi / n) * np.arange(n)[:, None]
    f = np.linspace(1e-4, FILT_BANDS - 1, FILT_BANDS)[None, :]
    feats = np.concatenate([t, np.cos(f * w), -np.sin(f * w)], axis=-1)
    feats = np.pad(feats, ((0, 0), (0, LANES - FILT_EMB)))
    deltas = np.abs(np.linspace(HY_MIN_DECAY, HY_MAX_DECAY, HY_CH))
    decay = np.exp(-t * deltas[None, :])
    return jnp.asarray(feats, F32), jnp.asarray(decay, F32)


def _dense_conv_tables(n):
    big_l = 2 * n
    k = np.arange(big_l)[:, None]
    t = np.arange(n)[None, :]
    th = 2.0 * math.pi * k * t / big_l
    fwd = np.concatenate([np.cos(th), -np.sin(th)], axis=0)
    inv = np.concatenate([np.cos(th).T, -np.sin(th).T], axis=1) / big_l
    return (jnp.asarray(np.cos(th), F32), jnp.asarray(np.sin(th), F32), _bf16_table(fwd), _bf16_table(inv))


def _hyena_fft_tables(n):
    big_l = 2 * n
    n2 = HY_N2
    n1 = big_l // n2
    n1h = n1 // 2
    k1 = np.arange(n1)
    j1 = np.arange(n1h)
    j2 = np.arange(n2)
    k2 = np.arange(n2)
    th = 2.0 * math.pi * k1[None, :, None] * (j1[None, None, :] * n2 + j2[:, None, None]) / big_l
    m1 = np.concatenate([np.cos(th), -np.sin(th)], axis=1)
    th2 = 2.0 * math.pi * k2[:, None] * j2[None, :] / n2
    c2, s2 = np.cos(th2), np.sin(th2)
    w2 = np.block([[c2, s2], [-s2, c2]])
    tha = 2.0 * math.pi * j2[None, :, None] * (k1[:, None, None] + n1 * k2[None, None, :]) / big_l
    ca, sa = np.cos(tha), np.sin(tha)
    ma = np.concatenate([np.concatenate([ca, -sa], axis=2), np.concatenate([sa, ca], axis=2)], axis=1)
    thb = 2.0 * math.pi * j1[:, None] * k1[None, :] / n1
    mb = np.concatenate([np.cos(thb), -np.sin(thb)], axis=1) / big_l
    return _bf16_table(m1), _bf16_table(w2), _bf16_table(ma), _bf16_table(mb)


def _fnet_channel_table(n):
    m = np.arange(FN_GROUP_CH)
    th = 2.0 * math.pi * m[:, None] * m[None, :] / FN_GROUP_CH
    eye = np.eye(GROUP_W // FN_GROUP_CH)
    sc = 1.0 / math.sqrt(n * FN_GROUP_CH)
    cs = np.concatenate([np.kron(eye, np.cos(th)), np.kron(eye, np.sin(th))], axis=1) * sc
    return _bf16_table(cs)


def _fnet_dense_table(n):
    k = np.arange(n)
    th = 2.0 * math.pi * k[:, None] * k[None, :] / n
    return _bf16_table(np.concatenate([np.cos(th), -np.sin(th)], axis=1))


def _fnet_fft_tables(n):
    n2 = FN_N2
    n1 = n // n2
    k1 = np.arange(n1)
    j1 = np.arange(n1)
    j2 = np.arange(n2)
    k2 = np.arange(n2)
    th = 2.0 * math.pi * k1[None, :, None] * (j1[None, None, :] * n2 + j2[:, None, None]) / n
    c, s = np.cos(th), np.sin(th)
    f1 = np.concatenate([np.concatenate([c, -s], axis=2), np.concatenate([s, c], axis=2)], axis=1)
    th2 = 2.0 * math.pi * k2[:, None] * j2[None, :] / n2
    f2 = np.concatenate([np.cos(th2), -np.sin(th2)], axis=1)
    return _bf16_table(f1), _bf16_table(f2)


def _mod_kernel(cond_ref, w_ref, b_ref, o_ref):
    c = cond_ref[...]
    s = c * jax.nn.sigmoid(c)
    o_ref[...] = _dot3(s, w_ref[...]) + b_ref[...]


def _modulation(cond, w_ada, b_ada):
    tn = 1024
    return pl.pallas_call(
        _mod_kernel,
        grid=(DEPTH, 3 * D_MODEL // tn),
        in_specs=[pl.BlockSpec((COND_ROWS, D_MODEL), lambda l, j: (0, 0)),
                  pl.BlockSpec((None, D_MODEL, tn), lambda l, j: (l, 0, j)),
                  pl.BlockSpec((None, 1, tn), lambda l, j: (l, 0, j))],
        out_specs=pl.BlockSpec((None, COND_ROWS, tn), lambda l, j: (l, 0, j)),
        out_shape=jax.ShapeDtypeStruct((DEPTH, COND_ROWS, 3 * D_MODEL), F32),
        compiler_params=_cparams("arbitrary", "arbitrary"),
        name="modulation",
    )(cond, w_ada, b_ada.reshape(DEPTH, 1, 3 * D_MODEL))


def _rope(x, cos, sin, half):
    lane = lax.broadcasted_iota(jnp.int32, x.shape, 1)
    first = (lane % (2 * half)) < half
    partner = jnp.where(first, pltpu.roll(x, LANES - half, 1), pltpu.roll(x, half, 1))
    return x * cos + partner * sin


def _entry_kernel(*refs, latent, tiles_per_b):
    if latent:
        (x_ref, mod_ref, g_ref, w_ref, ca_ref, sa_ref, cd_ref, sd_ref,
         aq_ref, akt_ref, av_ref, dq_ref, dkt_ref, dv_ref, hu_ref, fu_ref, sg_ref) = refs
        row = pl.program_id(0) // tiles_per_b
    else:
        (x_ref, mod_ref, g_ref, w_ref,
         aq_ref, akt_ref, av_ref, dq_ref, dkt_ref, dv_ref, hu_ref, fu_ref, sg_ref,
         ak32_ref, av32_ref, dk32_ref, dv32_ref) = refs
        row = CTX_ROW
    m = mod_ref[pl.ds(row, 1), :]
    shift, scale = m[:, :D_MODEL], m[:, D_MODEL:2 * D_MODEL]
    x = x_ref[...]
    h = x * lax.rsqrt(jnp.mean(x * x, axis=-1, keepdims=True) + EPS) * g_ref[...]
    hb = (h * (1.0 + scale) + shift).astype(BF16)

    def proj(c0, c1):
        return _dot(hb, w_ref[:, c0:c1])

    def roped(p, cos_ref, sin_ref, half):
        if not latent:
            return p
        cos, sin = cos_ref[...], sin_ref[...]
        chunks = [_rope(p[:, j:j + LANES], cos, sin, half) for j in range(0, p.shape[1], LANES)]
        return chunks[0] if len(chunks) == 1 else jnp.concatenate(chunks, axis=1)

    ca = sa = cd = sd = None
    if latent:
        ca, sa, cd, sd = ca_ref, sa_ref, cd_ref, sd_ref

    aq_ref[...] = (roped(proj(C_AQ, C_AK), ca, sa, ATT_HEAD_DIM // 2) * (ATT_HEAD_DIM ** -0.5)).astype(BF16)
    ak = roped(proj(C_AK, C_AV), ca, sa, ATT_HEAD_DIM // 2)
    akt_ref[...] = ak.T.astype(BF16)
    av = proj(C_AV, C_AG)
    av_ref[...] = av.astype(BF16)
    dq_ref[...] = (roped(proj(C_DQ, C_DK), cd, sd, DIF_QK_DIM // 2) * (DIF_QK_DIM ** -0.5 * LOG2_E)).astype(BF16)
    dk = roped(proj(C_DK, C_DV), cd, sd, DIF_QK_DIM // 2)
    dkt_ref[...] = dk.T.astype(BF16)
    dve = proj(C_DVE, C_DVE + 2 * GROUP_W)
    ones_lane = (lax.broadcasted_iota(jnp.int32, dve.shape, 1) % (2 * DIF_V_DIM)) >= DIF_V_DIM
    dv_ref[...] = jnp.where(ones_lane, 1.0, dve).astype(BF16)
    if not latent:
        ak32_ref[...] = ak
        av32_ref[...] = av
        dk32_ref[...] = dk
        dv32_ref[...] = proj(C_DV, C_DG)
    hu_ref[...] = proj(C_HU, C_HG).astype(BF16)
    fu_ref[...] = proj(C_FU, C_FG).astype(BF16)
    for j, c0 in enumerate((C_AG, C_DG, C_HG, C_FG)):
        g = proj(c0, c0 + GROUP_W)
        sg_ref[:, j * GROUP_W:(j + 1) * GROUP_W] = (g * jax.nn.sigmoid(g)).astype(BF16)


def _layer_entry(x2d, bsz, n, mod_all, g_pre, w_in_bf, layer, rope):
    latent = rope is not None
    t = bsz * n
    tm = 512 if latent else n
    tiles_per_b = n // tm
    grid = (t // tm,)
    row = lambda i: (i, 0)
    kt_map = lambda i: (i // tiles_per_b, 0, i % tiles_per_b)
    in_specs = [pl.BlockSpec((tm, D_MODEL), row),
                pl.BlockSpec((None, COND_ROWS, 3 * D_MODEL), lambda i: (layer, 0, 0)),
                pl.BlockSpec((None, 1, D_MODEL), lambda i: (layer, 0, 0)),
                pl.BlockSpec((None, D_MODEL, D_IN_EXT), lambda i: (layer, 0, 0))]
    args = [x2d, mod_all, g_pre.reshape(DEPTH, 1, D_MODEL), w_in_bf]
    if latent:
        pos = lambda i: (i % tiles_per_b, 0)
        in_specs += [pl.BlockSpec((tm, LANES), pos)] * 4
        args += list(rope)
    bf = lambda c: jax.ShapeDtypeStruct((t, c), BF16)
    out_shape = [bf(256), jax.ShapeDtypeStruct((bsz, 128, n), BF16), bf(128),
                 bf(256), jax.ShapeDtypeStruct((bsz, 256, n), BF16), bf(512),
                 bf(768), bf(256), bf(1024)]
    rs = lambda c: pl.BlockSpec((tm, c), row)
    out_specs = [rs(256), pl.BlockSpec((None, 128, tm), kt_map), rs(128),
                 rs(256), pl.BlockSpec((None, 256, tm), kt_map), rs(512),
                 rs(768), rs(256), rs(1024)]
    if not latent:
        out_shape += [jax.ShapeDtypeStruct((t, c), F32) for c in (128, 128, 256, 256)]
        out_specs += [rs(128), rs(128), rs(256), rs(256)]
    return pl.pallas_call(
        functools.partial(_entry_kernel, latent=latent, tiles_per_b=tiles_per_b),
        grid=grid, in_specs=in_specs, out_specs=out_specs, out_shape=out_shape,
        compiler_params=_cparams("arbitrary"),
        name="entry_latent" if latent else "entry_ctx",
    )(*args)


def _softmax_pieces(scores, sink_col=None):
    m = functools.reduce(jnp.maximum, [jnp.max(s, axis=-1, keepdims=True) for s in scores])
    if sink_col is not None:
        m = jnp.maximum(m, sink_col)
    ps = [jnp.exp(s - m) for s in scores]
    l = functools.reduce(lambda a, b: a + b, [jnp.sum(p, axis=-1, keepdims=True) for p in ps])
    if sink_col is not None:
        l = l + jnp.exp(sink_col - m)
    return ps, l


def _attn_a_kernel(sink_ref, *refs, layer, window, nb):
    if window:
        (q_ref, kp_ref, kc_ref, kn_ref, vp_ref, vc_ref, vn_ref, ck_ref, cv_ref, sg_ref, o_ref) = refs
    else:
        (q_ref, ck_ref, cv_ref, sg_ref, o_ref) = refs
    tq = q_ref.shape[0]
    grp = ATT_HEADS // ATT_KV_HEADS
    rows = grp * tq
    if window:
        i = pl.program_id(1)
        r = lax.broadcasted_iota(jnp.int32, (rows, 3 * BLOCK), 0) % tq
        c = lax.broadcasted_iota(jnp.int32, (rows, 3 * BLOCK), 1)
        valid = ((c >= r) & (c <= r + 2 * WINDOW)
                 & ((c >= BLOCK) | (i > 0)) & ((c < 2 * BLOCK) | (i < nb - 1)))
        kw = jnp.concatenate([kp_ref[...], kc_ref[...], kn_ref[...]], axis=1)
        vw = jnp.concatenate([vp_ref[...], vc_ref[...], vn_ref[...]], axis=0)
    head_row = lax.broadcasted_iota(jnp.int32, (rows, 1), 0) // tq
    q = q_ref[...]
    ck = ck_ref[...]
    cv = cv_ref[...]
    outs = []
    for g in range(ATT_KV_HEADS):
        d0 = g * ATT_HEAD_DIM
        q2 = jnp.concatenate([q[:, (g * grp + j) * ATT_HEAD_DIM:(g * grp + j + 1) * ATT_HEAD_DIM]
                              for j in range(grp)], axis=0)
        sink_col = jnp.zeros((rows, 1), F32)
        for j in range(grp):
            sink_col = jnp.where(head_row == j, sink_ref[layer * ATT_HEADS + g * grp + j], sink_col)
        scores, vals = [], []
        if window:
            s_w = _dot(q2, kw[d0:d0 + ATT_HEAD_DIM, :])
            scores.append(jnp.where(valid, s_w, NEG_INF))
            vals.append(vw[:, d0:d0 + ATT_HEAD_DIM])
        scores.append(_dot(q2, ck[d0:d0 + ATT_HEAD_DIM, :]))
        vals.append(cv[:, d0:d0 + ATT_HEAD_DIM])
        ps, l = _softmax_pieces(scores, sink_col)
        o = functools.reduce(lambda a, b: a + b, [_dot(p.astype(BF16), v) for p, v in zip(ps, vals)]) / l
        outs += [o[j * tq:(j + 1) * tq] for j in range(grp)]
    o_full = jnp.concatenate(outs, axis=1)
    o_ref[...] = (o_full * sg_ref[...].astype(F32)).astype(BF16)


def _attn_a_win_kernel(sink_ref, q_ref, kp_ref, kc_ref, kn_ref, vp_ref, vc_ref, vn_ref, ck_ref, cv_ref, sg_ref,
                       o_ref, *, layer, nsteps, qb):
    i = pl.program_id(1)
    grp = ATT_HEADS // ATT_KV_HEADS
    rows = grp * BLOCK
    kw = jnp.concatenate([kp_ref[...], kc_ref[...], kn_ref[...]], axis=1)
    vw = jnp.concatenate([vp_ref[...], vc_ref[...], vn_ref[...]], axis=0)
    r = lax.broadcasted_iota(jnp.int32, (rows, 3 * BLOCK), 0) % BLOCK
    c = lax.broadcasted_iota(jnp.int32, (rows, 3 * BLOCK), 1)
    band = (c >= r) & (c <= r + 2 * WINDOW)
    masks = {0: band & ((c >= BLOCK) | (i > 0)), qb - 1: band & ((c < 2 * BLOCK) | (i < nsteps - 1))}
    head_row = lax.broadcasted_iota(jnp.int32, (rows, 1), 0) // BLOCK
    sink_cols = []
    for g in range(ATT_KV_HEADS):
        col = jnp.zeros((rows, 1), F32)
        for j in range(grp):
            col = jnp.where(head_row == j, sink_ref[layer * ATT_HEADS + g * grp + j], col)
        sink_cols.append(col)
    ck = ck_ref[...]
    cv = cv_ref[...]
    items = [(j, g) for j in range(qb) for g in range(ATT_KV_HEADS)]
    st_a, st_b, outs = {}, {}, {}
    for step in range(len(items) + 2):
        a, b, cc = step, step - 1, step - 2
        if a < len(items):
            j, g = items[a]
            d0 = g * ATT_HEAD_DIM
            q2 = jnp.concatenate([q_ref[j * BLOCK:(j + 1) * BLOCK, (g * grp + t) * ATT_HEAD_DIM:
                                        (g * grp + t + 1) * ATT_HEAD_DIM] for t in range(grp)], axis=0)
            s_w = _dot(q2, kw[d0:d0 + ATT_HEAD_DIM, j * BLOCK:(j + 3) * BLOCK])
            s_w = jnp.where(masks.get(j, band), s_w, NEG_INF)
            s_c = _dot(q2, ck[d0:d0 + ATT_HEAD_DIM, :])
            m = jnp.maximum(jnp.maximum(jnp.max(s_w, axis=-1, keepdims=True),
                                        jnp.max(s_c, axis=-1, keepdims=True)), sink_cols[g])
            st_a[a] = (s_w, s_c, m)
        if 0 <= b < len(items):
            s_w, s_c, m = st_a.pop(b)
            p_w = jnp.exp(s_w - m)
            p_c = jnp.exp(s_c - m)
            l = (jnp.sum(p_w, axis=-1, keepdims=True) + jnp.sum(p_c, axis=-1, keepdims=True)
                 + jnp.exp(sink_cols[items[b][1]] - m))
            st_b[b] = (p_w.astype(BF16), p_c.astype(BF16), l)
        if 0 <= cc < len(items):
            j, g = items[cc]
            d0 = g * ATT_HEAD_DIM
            p_w, p_c, l = st_b.pop(cc)
            o = (_dot(p_w, vw[j * BLOCK:(j + 3) * BLOCK, d0:d0 + ATT_HEAD_DIM])
                 + _dot(p_c, cv[:, d0:d0 + ATT_HEAD_DIM])) / l
            outs.setdefault(j, []).extend([o[t * BLOCK:(t + 1) * BLOCK] for t in range(grp)])
            if g == ATT_KV_HEADS - 1:
                rs = slice(j * BLOCK, (j + 1) * BLOCK)
                o_full = jnp.concatenate(outs.pop(j), axis=1)
                o_ref[rs, :] = (o_full * sg_ref[rs, :].astype(F32)).astype(BF16)


def _attn_a_latent(aq, akt, av, ck_t, cv, sg, sink, bsz, n, layer, qb=4):
    nb = n // BLOCK
    nsteps = nb // qb
    qmap = lambda b, i: (b * nsteps + i, 0)
    prev = lambda i: jnp.maximum(qb * i - 1, 0)
    nxt = lambda i: jnp.minimum(qb * i + qb, nb - 1)
    kedge = lambda f: pl.BlockSpec((None, 128, BLOCK), lambda b, i: (b, 0, f(i)))
    vedge = lambda f: pl.BlockSpec((BLOCK, 128), lambda b, i: (b * nb + f(i), 0))
    return pl.pallas_call(
        functools.partial(_attn_a_win_kernel, layer=layer, nsteps=nsteps, qb=qb),
        grid=(bsz, nsteps),
        in_specs=[pl.BlockSpec(memory_space=pltpu.SMEM),
                  pl.BlockSpec((qb * BLOCK, 256), qmap),
                  kedge(prev), pl.BlockSpec((None, 128, qb * BLOCK), lambda b, i: (b, 0, i)), kedge(nxt),
                  vedge(prev), pl.BlockSpec((qb * BLOCK, 128), qmap), vedge(nxt),
                  pl.BlockSpec((None, 128, ck_t.shape[2]), lambda b, i: (b, 0, 0)),
                  pl.BlockSpec((None, cv.shape[1], 128), lambda b, i: (b, 0, 0)),
                  pl.BlockSpec((qb * BLOCK, 256), qmap)],
        out_specs=pl.BlockSpec((qb * BLOCK, 256), qmap),
        out_shape=jax.ShapeDtypeStruct((bsz * n, 256), BF16),
        compiler_params=_cparams("arbitrary", "arbitrary"),
        name="attn_a_latent",
    )(sink, aq, akt, akt, akt, av, av, av, ck_t, cv, sg)


def _attn_a_ctx(aq, akt, av, sg, sink, bsz, n, layer):
    return pl.pallas_call(
        functools.partial(_attn_a_kernel, layer=layer, window=False, nb=1),
        grid=(bsz,),
        in_specs=[pl.BlockSpec(memory_space=pltpu.SMEM),
                  pl.BlockSpec((n, 256), lambda b: (b, 0)),
                  pl.BlockSpec((None, 128, n), lambda b: (b, 0, 0)),
                  pl.BlockSpec((n, 128), lambda b: (b, 0)),
                  pl.BlockSpec((n, 256), lambda b: (b, 0))],
        out_specs=pl.BlockSpec((n, 256), lambda b: (b, 0)),
        out_shape=jax.ShapeDtypeStruct((bsz * n, 256), BF16),
        compiler_params=_cparams("arbitrary"),
        name="attn_a_ctx",
    )(sink, aq, akt, av, sg)


def _diff_kernel(lam_ref, subln_ref, q_ref, *refs, lam_init, npieces, tq_sub, kc):
    kts = refs[0:2 * npieces:2]
    vs = refs[1:2 * npieces:2]
    sg_ref, o_ref, s_scr, p_scr = refs[2 * npieces:2 * npieces + 4]
    lp = lam_ref[...]
    lam = (jnp.exp(jnp.sum(lp[0:1] * lp[1:2], axis=-1, keepdims=True))
           - jnp.exp(jnp.sum(lp[2:3] * lp[3:4], axis=-1, keepdims=True)) + lam_init)
    subln = subln_ref[...]
    tq = q_ref.shape[0]
    chunks, off = [], 0
    for kt, v in zip(kts, vs):
        for c0 in range(0, kt.shape[1], kc):
            w = min(kc, kt.shape[1] - c0)
            chunks.append((kt, v, c0, w, off))
            off += w
    items = [(r0, h, mp) for r0 in range(0, tq, tq_sub) for h in range(DIF_HEADS) for mp in range(2)]
    n_items = len(items)
    mx, acc, om, outs = {}, {}, {}, {}

    def fold(s):
        return functools.reduce(jnp.maximum, [s[:, j:j + LANES] for j in range(0, s.shape[1], LANES)])

    nslot = s_scr.shape[0]
    lag = nslot - 1
    for step in range(n_items + 2 * lag):
        a = step if step < n_items else None
        b = step - lag if 0 <= step - lag < n_items else None
        c = step - 2 * lag if 0 <= step - 2 * lag < n_items else None
        if a is not None:
            r0, h, mp = items[a]
            f0 = (mp * DIF_HEADS + h) * DIF_QK_DIM
            q_a = q_ref[r0:r0 + tq_sub, f0:f0 + DIF_QK_DIM]
            mx[a] = jnp.full((tq_sub, LANES), NEG_INF, F32)
        if b is not None:
            m_b = jnp.max(mx.pop(b), axis=-1, keepdims=True)
        if c is not None:
            acc[c] = jnp.zeros((tq_sub, 2 * DIF_V_DIM), F32)
            hc = items[c][1]
        for kt, v, c0, w, o0 in chunks:
            if a is not None:
                s = _dot(q_a, kt[f0:f0 + DIF_QK_DIM, c0:c0 + w])
                s_scr[a % nslot, :, o0:o0 + w] = s
                mx[a] = jnp.maximum(mx[a], fold(s))
            if b is not None:
                p_scr[b % nslot, :, o0:o0 + w] = jnp.exp2(s_scr[b % nslot, :, o0:o0 + w] - m_b).astype(BF16)
            if c is not None:
                acc[c] = acc[c] + _dot(p_scr[c % nslot, :, o0:o0 + w],
                                       v[c0:c0 + w, 2 * DIF_V_DIM * hc:2 * DIF_V_DIM * (hc + 1)])
        if c is not None:
            r0, h, mp = items[c]
            o = acc.pop(c)
            om[(r0, h, mp)] = o[:, :DIF_V_DIM] / o[:, DIF_V_DIM:DIF_V_DIM + 1]
            if mp == 1:
                av = om.pop((r0, h, 0)) - lam * om.pop((r0, h, 1))
                y = av * lax.rsqrt(jnp.mean(av * av, axis=-1, keepdims=True) + EPS) * subln * (1.0 - lam_init)
                outs.setdefault(r0, []).append(y)
                if h == DIF_HEADS - 1:
                    o_full = jnp.concatenate(outs.pop(r0), axis=1)
                    o_ref[r0:r0 + tq_sub, :] = (o_full * sg_ref[r0:r0 + tq_sub, :].astype(F32)).astype(BF16)


def _diff_attention(dq, pieces, sg, diff_lambda, diff_subln, bsz, n, layer, tq, kc=512):
    lam_init = 0.8 - 0.6 * math.exp(-0.3 * layer)
    nt = n // tq
    nk_all = sum(kt.shape[2] for kt, _ in pieces)
    nslot = 2 if nk_all > kc else 2 * DIF_HEADS + 1
    in_specs = [pl.BlockSpec((None, 4, DIF_QK_DIM), lambda b, i: (layer, 0, 0)),
                pl.BlockSpec((None, 1, DIF_V_DIM), lambda b, i: (layer, 0, 0)),
                pl.BlockSpec((tq, 256), lambda b, i: (b * nt + i, 0))]
    args = [diff_lambda, diff_subln.reshape(DEPTH, 1, DIF_V_DIM), dq]
    for kt, v in pieces:
        nk = kt.shape[2]
        in_specs.append(pl.BlockSpec((None, 256, nk), lambda b, i: (b, 0, 0)))
        if v.ndim == 3:
            in_specs.append(pl.BlockSpec((None, nk, 512), lambda b, i: (b, 0, 0)))
        else:
            in_specs.append(pl.BlockSpec((nk, 512), lambda b, i: (b, 0)))
        args += [kt, v]
    in_specs.append(pl.BlockSpec((tq, 256), lambda b, i: (b * nt + i, 1)))
    args.append(sg)
    return pl.pallas_call(
        functools.partial(_diff_kernel, lam_init=lam_init, npieces=len(pieces), tq_sub=tq, kc=kc),
        grid=(bsz, nt), in_specs=in_specs,
        out_specs=pl.BlockSpec((tq, 256), lambda b, i: (b * nt + i, 0)),
        out_shape=jax.ShapeDtypeStruct((bsz * n, 256), BF16),
        scratch_shapes=[pltpu.VMEM((nslot, tq, nk_all), F32), pltpu.VMEM((nslot, tq, nk_all), BF16)],
        compiler_params=_cparams("arbitrary", "arbitrary"),
        name="diff_attn_%d" % len(pieces),
    )(*args)


def _filter_core(feats_ref, w1_ref, b1_ref, w2_ref, b2_ref, w3_ref, fr_ref, decay_ref):
    fr = fr_ref[...]
    h = jnp.sin(fr * (_dot3(feats_ref[...], w1_ref[...]) + b1_ref[...]))
    h = jnp.sin(fr * (_dot3(h, w2_ref[...]) + b2_ref[...]))
    h = _dot3(h, w3_ref[...])
    decay = decay_ref[...]
    hf = h[:, :HY_CH] * decay
    hb = h[:, HY_CH:] * decay
    tot = (jnp.sum(jnp.abs(hf), axis=0, keepdims=True) + jnp.sum(jnp.abs(hb), axis=0, keepdims=True) + EPS)
    hf = hf / tot
    hb = hb / tot
    rowi = lax.broadcasted_iota(jnp.int32, hb.shape, 0)
    return hf, jnp.where(rowi == 0, 0.0, hb)


def _filter_ctx_kernel(feats_ref, w1_ref, b1_ref, w2_ref, b2_ref, w3_ref, fr_ref, decay_ref,
                       cos_ref, sin_ref, kr_ref, ki_ref):
    hf, hb0 = _filter_core(feats_ref, w1_ref, b1_ref, w2_ref, b2_ref, w3_ref, fr_ref, decay_ref)
    kr_ref[...] = _dot3(cos_ref[...], hf + hb0)
    ki_ref[...] = _dot3(sin_ref[...], hb0 - hf)


def _filter_lat_kernel(feats_ref, w1_ref, b1_ref, w2_ref, b2_ref, w3_ref, fr_ref, decay_ref, hf_ref, hb_ref):
    hf, hb0 = _filter_core(feats_ref, w1_ref, b1_ref, w2_ref, b2_ref, w3_ref, fr_ref, decay_ref)
    hf_ref[...] = hf
    hb_ref[...] = hb0


def _filter_specs(n, w1p, b1, w2, b2, w3, freq, feats, decay):
    const = lambda shape: pl.BlockSpec(shape, lambda l: (0,) * len(shape))
    lay = lambda a, b: pl.BlockSpec((None, a, b), lambda l: (l, 0, 0))
    in_specs = [const((n, LANES)), lay(LANES, FILT_HIDDEN), lay(1, FILT_HIDDEN),
                lay(FILT_HIDDEN, FILT_HIDDEN), lay(1, FILT_HIDDEN), lay(FILT_HIDDEN, 2 * HY_CH),
                lay(1, FILT_HIDDEN), const((n, HY_CH))]
    args = [feats, w1p, b1.reshape(DEPTH, 1, FILT_HIDDEN), w2, b2.reshape(DEPTH, 1, FILT_HIDDEN), w3,
            freq.reshape(DEPTH, 1, FILT_HIDDEN), decay]
    return in_specs, args


def _hyena_filters_ctx(n, filt, cos_t, sin_t):
    feats, decay = _filter_feats(n)
    in_specs, args = _filter_specs(n, *filt, feats, decay)
    big_l = 2 * n
    in_specs += [pl.BlockSpec((big_l, n), lambda l: (0, 0))] * 2
    out = pl.BlockSpec((None, big_l, HY_CH), lambda l: (l, 0, 0))
    return pl.pallas_call(
        _filter_ctx_kernel, grid=(DEPTH,), in_specs=in_specs, out_specs=[out, out],
        out_shape=[jax.ShapeDtypeStruct((DEPTH, big_l, HY_CH), F32)] * 2,
        compiler_params=_cparams("arbitrary"), name="hyena_filter_ctx",
    )(*args, cos_t, sin_t)


def _hyena_filters_lat(n, filt):
    feats, decay = _filter_feats(n)
    in_specs, args = _filter_specs(n, *filt, feats, decay)
    out = pl.BlockSpec((None, n, HY_CH), lambda l: (l, 0, 0))
    return pl.pallas_call(
        _filter_lat_kernel, grid=(DEPTH,), in_specs=in_specs, out_specs=[out, out],
        out_shape=[jax.ShapeDtypeStruct((DEPTH, n, HY_CH), F32)] * 2,
        compiler_params=_cparams("arbitrary"), name="hyena_filter_lat",
    )(*args)


def _short_conv(u_ref, cw_ref, cb_ref):
    u = u_ref[...].astype(F32)
    n = u.shape[0]
    rowi = lax.broadcasted_iota(jnp.int32, u.shape, 0)
    up = jnp.where(rowi == 0, 0.0, pltpu.roll(u, 1, 0))
    un = jnp.where(rowi == n - 1, 0.0, pltpu.roll(u, n - 1, 0))
    w = cw_ref[...]
    return up * w[0:1] + u * w[1:2] + un * w[2:3] + cb_ref[...]


def _hyena_ctx_kernel(hu_ref, cw_ref, cb_ref, kr_ref, ki_ref, fwd_ref, inv_ref, skip_ref, sg_ref, o_ref):
    uc = _short_conv(hu_ref, cw_ref, cb_ref)
    x0, x1, v = uc[:, :HY_CH], uc[:, HY_CH:2 * HY_CH], uc[:, 2 * HY_CH:]
    z = x1 * v
    zf = _dot(fwd_ref[...], z.astype(BF16))
    big_l = zf.shape[0] // 2
    zr, zi = zf[:big_l], zf[big_l:]
    kr, ki = kr_ref[...], ki_ref[...]
    p = jnp.concatenate([zr * kr - zi * ki, zr * ki + zi * kr], axis=0).astype(BF16)
    y = _dot(inv_ref[...], p)
    out = x0 * (y + z * skip_ref[...])
    o_ref[...] = (out * sg_ref[...].astype(F32)).astype(BF16)


def _hyena_ctx(hu, sg, conv_w, conv_b, skip, kr, ki, fwd_t, inv_t, bsz, n, layer):
    big_l = 2 * n
    lay = lambda a, b: pl.BlockSpec((None, a, b), lambda i: (layer, 0, 0))
    return pl.pallas_call(
        _hyena_ctx_kernel, grid=(bsz,),
        in_specs=[pl.BlockSpec((n, 768), lambda i: (i, 0)),
                  lay(3, 768), lay(1, 768), lay(big_l, HY_CH), lay(big_l, HY_CH),
                  pl.BlockSpec((2 * big_l, n), lambda i: (0, 0)),
                  pl.BlockSpec((n, 2 * big_l), lambda i: (0, 0)),
                  lay(1, HY_CH),
                  pl.BlockSpec((n, 256), lambda i: (i, 2))],
        out_specs=pl.BlockSpec((n, 256), lambda i: (i, 0)),
        out_shape=jax.ShapeDtypeStruct((bsz * n, 256), BF16),
        compiler_params=_cparams("arbitrary"), name="hyena_ctx",
    )(hu, conv_w, conv_b.reshape(DEPTH, 1, 768), kr, ki, fwd_t, inv_t, skip.reshape(DEPTH, 1, HY_CH), sg)


def _pitch(rows):
    return rows + PITCH_PAD


def _to_pitched(dst, src, blk):
    for i in range(src.shape[0] // blk):
        dst[i * _pitch(blk):i * _pitch(blk) + blk, :] = src[i * blk:(i + 1) * blk, :]


def _fft_stage1(src, m1_ref, s_re, s_im, n1, n2):
    for j2 in range(n2):
        x = src[pl.ds(j2, n1 // 2, stride=_pitch(n2)), :]
        a = _dot(m1_ref[j2], x.astype(BF16))
        s_re[j2 * _pitch(n1):j2 * _pitch(n1) + n1, :] = a[:n1]
        s_im[j2 * _pitch(n1):j2 * _pitch(n1) + n1, :] = a[n1:]


def _fft_stage2(w2_ref, s_re, s_im, k1, n1, n2):
    re = s_re[pl.ds(k1, n2, stride=_pitch(n1)), :]
    im = s_im[pl.ds(k1, n2, stride=_pitch(n1)), :]
    x = _dot(w2_ref[...], jnp.concatenate([re, im], axis=0).astype(BF16))
    return x[:n2], x[n2:]


def _filter_fft_kernel(hf_ref, hb_ref, m1_ref, w2_ref, kr_ref, ki_ref, x_scr, s_re, s_im, *, n1, n2):
    _to_pitched(x_scr, hf_ref, n2)
    _fft_stage1(x_scr, m1_ref, s_re, s_im, n1, n2)
    for k1 in range(n1):
        xr, xi = _fft_stage2(w2_ref, s_re, s_im, k1, n1, n2)
        kr_ref[k1 * n2:(k1 + 1) * n2, :] = xr
        ki_ref[k1 * n2:(k1 + 1) * n2, :] = xi
    _to_pitched(x_scr, hb_ref, n2)
    _fft_stage1(x_scr, m1_ref, s_re, s_im, n1, n2)
    for k1 in range(n1):
        xr, xi = _fft_stage2(w2_ref, s_re, s_im, k1, n1, n2)
        kr_ref[k1 * n2:(k1 + 1) * n2, :] += xr
        ki_ref[k1 * n2:(k1 + 1) * n2, :] -= xi


def _filter_fft(hf, hb0, m1, w2, n):
    big_l = 2 * n
    n2 = HY_N2
    n1 = big_l // n2
    inp = pl.BlockSpec((None, n, LANES), lambda l, c: (l, 0, c))
    out = pl.BlockSpec((None, big_l, LANES), lambda l, c: (l, 0, c))
    return pl.pallas_call(
        functools.partial(_filter_fft_kernel, n1=n1, n2=n2),
        grid=(DEPTH, HY_CH // LANES),
        in_specs=[inp, inp,
                  pl.BlockSpec(m1.shape, lambda l, c: (0, 0, 0)),
                  pl.BlockSpec(w2.shape, lambda l, c: (0, 0))],
        out_specs=[out, out],
        out_shape=[jax.ShapeDtypeStruct((DEPTH, big_l, HY_CH), F32)] * 2,
        scratch_shapes=[pltpu.VMEM((n1 // 2 * _pitch(n2), LANES), F32)]
                       + [pltpu.VMEM((n2 * _pitch(n1), LANES), F32)] * 2,
        compiler_params=_cparams("arbitrary", "arbitrary"), name="hyena_filter_fft",
    )(hf, hb0, m1, w2)


def _hyena_lat_kernel(x0_ref, x1_ref, v_ref, cw0_ref, cw1_ref, cw2_ref, cb0_ref, cb1_ref, cb2_ref,
                      kr_ref, ki_ref, m1_ref, w2_ref, ma_ref, mb_ref, skip_ref, sg_ref, o_ref,
                      z_scr, y_scr, s_re, s_im, *, n1, n2):
    n1h = n1 // 2
    z = _short_conv(x1_ref, cw1_ref, cb1_ref) * _short_conv(v_ref, cw2_ref, cb2_ref)
    _to_pitched(z_scr, z, n2)
    _fft_stage1(z_scr, m1_ref, s_re, s_im, n1, n2)
    for k1 in range(n1):
        xr, xi = _fft_stage2(w2_ref, s_re, s_im, k1, n1, n2)
        kr = kr_ref[k1 * n2:(k1 + 1) * n2, :]
        ki = ki_ref[k1 * n2:(k1 + 1) * n2, :]
        p = jnp.concatenate([xr * kr - xi * ki, xr * ki + xi * kr], axis=0).astype(BF16)
        b = _dot(ma_ref[k1], p)
        s_re[pl.ds(k1, n2, stride=_pitch(n1)), :] = b[:n2]
        s_im[pl.ds(k1, n2, stride=_pitch(n1)), :] = b[n2:]
    for j2 in range(n2):
        r0 = j2 * _pitch(n1)
        blk = jnp.concatenate([s_re[r0:r0 + n1, :], s_im[r0:r0 + n1, :]], axis=0)
        y_scr[pl.ds(j2, n1h, stride=_pitch(n2)), :] = _dot(mb_ref[...], blk.astype(BF16))
    x0 = _short_conv(x0_ref, cw0_ref, cb0_ref)
    skip = skip_ref[...]
    for j1 in range(n1h):
        rows = slice(j1 * n2, (j1 + 1) * n2)
        prow = slice(j1 * _pitch(n2), j1 * _pitch(n2) + n2)
        out = x0[rows] * (y_scr[prow, :] + z_scr[prow, :] * skip)
        o_ref[rows, :] = (out * sg_ref[rows, :].astype(F32)).astype(BF16)


def _hyena_lat(hu, sg, conv_w, conv_b, skip, kr, ki, tables, bsz, n, layer):
    m1, w2, ma, mb = tables
    big_l = 2 * n
    n2 = HY_N2
    n1 = big_l // n2
    nch = HY_CH // LANES
    ucol = lambda s: pl.BlockSpec((n, LANES), lambda c, b: (b, s * nch + c))
    wcol = lambda s: pl.BlockSpec((None, 3, LANES), lambda c, b: (layer, 0, s * nch + c))
    bcol = lambda s: pl.BlockSpec((None, 1, LANES), lambda c, b: (layer, 0, s * nch + c))
    kspec = pl.BlockSpec((None, big_l, LANES), lambda c, b: (layer, 0, c))
    const = lambda a: pl.BlockSpec(a.shape, lambda c, b: (0,) * a.ndim)
    cb3 = conv_b.reshape(DEPTH, 1, 768)
    return pl.pallas_call(
        functools.partial(_hyena_lat_kernel, n1=n1, n2=n2),
        grid=(nch, bsz),
        in_specs=[ucol(0), ucol(1), ucol(2), wcol(0), wcol(1), wcol(2), bcol(0), bcol(1), bcol(2),
                  kspec, kspec, const(m1), const(w2), const(ma), const(mb),
                  pl.BlockSpec((None, 1, LANES), lambda c, b: (layer, 0, c)),
                  pl.BlockSpec((n, LANES), lambda c, b: (b, 2 * nch + c))],
        out_specs=pl.BlockSpec((n, LANES), lambda c, b: (b, c)),
        out_shape=jax.ShapeDtypeStruct((bsz * n, 256), BF16),
        scratch_shapes=[pltpu.VMEM((n1 // 2 * _pitch(n2), LANES), F32)] * 2
                       + [pltpu.VMEM((n2 * _pitch(n1), LANES), F32)] * 2,
        compiler_params=_cparams("arbitrary", "arbitrary"), name="hyena_lat",
    )(hu, hu, hu, conv_w, conv_w, conv_w, cb3, cb3, cb3, kr, ki, m1, w2, ma, mb,
      skip.reshape(DEPTH, 1, HY_CH), sg)


def _fnet_ctx_kernel(u_ref, cs_ref, dn_ref, fw_ref, fb_ref, sg_ref, o_ref):
    ab = _dot(u_ref[...], cs_ref[...])
    stack = jnp.concatenate([ab[:, :GROUP_W], ab[:, GROUP_W:]], axis=0).astype(BF16)
    f = _dot(dn_ref[...], stack)
    out = _bdot(f, fw_ref[...]) + fb_ref[...]
    o_ref[...] = (out * sg_ref[...].astype(F32)).astype(BF16)


def _fnet_ctx(fu, sg, fn_w, fn_b, cs, dn, bsz, n, layer):
    return pl.pallas_call(
        _fnet_ctx_kernel, grid=(bsz,),
        in_specs=[pl.BlockSpec((n, 256), lambda i: (i, 0)),
                  pl.BlockSpec(cs.shape, lambda i: (0, 0)),
                  pl.BlockSpec(dn.shape, lambda i: (0, 0)),
                  pl.BlockSpec((None, GROUP_W, GROUP_W), lambda i: (layer, 0, 0)),
                  pl.BlockSpec((None, 1, GROUP_W), lambda i: (layer, 0, 0)),
                  pl.BlockSpec((n, 256), lambda i: (i, 3))],
        out_specs=pl.BlockSpec((n, 256), lambda i: (i, 0)),
        out_shape=jax.ShapeDtypeStruct((bsz * n, 256), BF16),
        compiler_params=_cparams("arbitrary"), name="fnet_ctx",
    )(fu, cs, dn, fn_w, fn_b.reshape(DEPTH, 1, GROUP_W), sg)


def _fnet_lat_kernel(u_ref, cs_ref, f1_ref, f2_ref, fw_ref, fb_ref, sg_ref, o_ref,
                     a_scr, b_scr, s_re, s_im, o_scr, *, n1, n2):
    halves = GROUP_W // LANES

    def put(scr, rows, val):
        for hh in range(halves):
            scr[hh, rows, :] = val[:, hh * LANES:(hh + 1) * LANES]

    def get(scr, rows):
        return jnp.concatenate([scr[hh, rows, :] for hh in range(halves)], axis=1)

    def block(i, blk):
        return slice(i * _pitch(blk), i * _pitch(blk) + blk)

    ab = _dot(u_ref[...], cs_ref[...])
    for j1 in range(n1):
        put(a_scr, block(j1, n2), ab[j1 * n2:(j1 + 1) * n2, :GROUP_W])
        put(b_scr, block(j1, n2), ab[j1 * n2:(j1 + 1) * n2, GROUP_W:])
    for j2 in range(n2):
        rows = pl.ds(j2, n1, stride=_pitch(n2))
        g = jnp.concatenate([get(a_scr, rows), get(b_scr, rows)], axis=0)
        t = _dot(f1_ref[j2], g.astype(BF16))
        put(s_re, block(j2, n1), t[:n1])
        put(s_im, block(j2, n1), t[n1:])
    for k1 in range(n1):
        rows = pl.ds(k1, n2, stride=_pitch(n1))
        g = jnp.concatenate([get(s_re, rows), get(s_im, rows)], axis=0)
        put(o_scr, rows, _dot(f2_ref[...], g.astype(BF16)))
    fw = fw_ref[...].astype(BF16)
    fb = fb_ref[...]
    for k2 in range(n2):
        rows = slice(k2 * n1, (k2 + 1) * n1)
        out = _dot(get(o_scr, block(k2, n1)).astype(BF16), fw) + fb
        o_ref[rows, :] = (out * sg_ref[rows, :].astype(F32)).astype(BF16)


def _fnet_lat(fu, sg, fn_w, fn_b, cs, f1, f2, bsz, n, layer):
    n2 = FN_N2
    n1 = n // n2
    return pl.pallas_call(
        functools.partial(_fnet_lat_kernel, n1=n1, n2=n2), grid=(bsz,),
        in_specs=[pl.BlockSpec((n, 256), lambda i: (i, 0)),
                  pl.BlockSpec(cs.shape, lambda i: (0, 0)),
                  pl.BlockSpec(f1.shape, lambda i: (0, 0, 0)),
                  pl.BlockSpec(f2.shape, lambda i: (0, 0)),
                  pl.BlockSpec((None, GROUP_W, GROUP_W), lambda i: (layer, 0, 0)),
                  pl.BlockSpec((None, 1, GROUP_W), lambda i: (layer, 0, 0)),
                  pl.BlockSpec((n, 256), lambda i: (i, 3))],
        out_specs=pl.BlockSpec((n, 256), lambda i: (i, 0)),
        out_shape=jax.ShapeDtypeStruct((bsz * n, 256), BF16),
        scratch_shapes=[pltpu.VMEM((GROUP_W // LANES, n1 * _pitch(n2), LANES), F32)] * 2
                       + [pltpu.VMEM((GROUP_W // LANES, n2 * _pitch(n1), LANES), F32)] * 3,
        compiler_params=_cparams("arbitrary"), name="fnet_lat",
    )(fu, cs, f1, f2, fn_w, fn_b.reshape(DEPTH, 1, GROUP_W), sg)


def _exit_kernel(x_ref, oa_ref, ob_ref, oc_ref, od_ref, w_ref, g_ref, mod_ref, o_ref, *, latent, tiles_per_b):
    row = (pl.program_id(0) // tiles_per_b) if latent else CTX_ROW
    gate = mod_ref[pl.ds(row, 1), :][:, 2 * D_MODEL:]
    mixed = jnp.concatenate([oa_ref[...], ob_ref[...], oc_ref[...], od_ref[...]], axis=1)
    y = _dot(mixed, w_ref[...])
    y = y * lax.rsqrt(jnp.mean(y * y, axis=-1, keepdims=True) + EPS) * g_ref[...]
    o_ref[...] = x_ref[...] + gate * y


def _layer_exit(x2d, outs, w_out_bf, g_post, mod_all, n, layer, latent):
    t = x2d.shape[0]
    tm = 512
    tiles_per_b = n // tm if latent else 1
    row = lambda i: (i, 0)
    return pl.pallas_call(
        functools.partial(_exit_kernel, latent=latent, tiles_per_b=tiles_per_b),
        grid=(t // tm,),
        in_specs=[pl.BlockSpec((tm, D_MODEL), row)] + [pl.BlockSpec((tm, 256), row)] * 4
                 + [pl.BlockSpec((None, D_MODEL, D_MODEL), lambda i: (layer, 0, 0)),
                    pl.BlockSpec((None, 1, D_MODEL), lambda i: (layer, 0, 0)),
                    pl.BlockSpec((None, COND_ROWS, 3 * D_MODEL), lambda i: (layer, 0, 0))],
        out_specs=pl.BlockSpec((tm, D_MODEL), row),
        out_shape=jax.ShapeDtypeStruct((t, D_MODEL), F32),
        compiler_params=_cparams("arbitrary"),
        name="exit_latent" if latent else "exit_ctx",
    )(x2d, *outs, w_out_bf, g_post.reshape(DEPTH, 1, D_MODEL), mod_all)


def kernel(x_prompt, x_sample, cache_attn_k, cache_attn_v, cache_diff_k, cache_diff_v, c, c_ctx, w_ada, b_ada, norm_pre, norm_post, w_in, w_out, attn_sink, diff_lambda, diff_subln, hy_conv_w, hy_conv_b, hy_filt_w1, hy_filt_b1, hy_filt_w2, hy_filt_b2, hy_filt_w3, hy_filt_freq, hy_skip, fn_w, fn_b):
    bp, lp, _ = x_prompt.shape
    bs, ls, _ = x_sample.shape
    past = cache_attn_k.shape[2]
    assert bs < CTX_ROW + 1 <= COND_ROWS

    cond = jnp.concatenate([c, c_ctx[None, :], jnp.zeros((COND_ROWS - bs - 1, D_MODEL), F32)], axis=0)
    mod_all = _modulation(cond, w_ada, b_ada)
    w_dv = w_in[:, :, C_DV:C_DG].reshape(DEPTH, D_MODEL, DIF_HEADS, DIF_V_DIM)
    w_dve = jnp.pad(w_dv, ((0, 0), (0, 0), (0, 0), (0, DIF_V_DIM))).reshape(DEPTH, D_MODEL, 2 * GROUP_W)
    w_in_bf = jnp.concatenate([w_in, w_dve], axis=-1).astype(BF16)
    w_out_bf = w_out.astype(BF16)
    sink = attn_sink.reshape(DEPTH * ATT_HEADS)

    rope = _rope_tables(ls, ATT_HEAD_DIM) + _rope_tables(ls, DIF_QK_DIM)
    filt = (jnp.pad(hy_filt_w1, ((0, 0), (0, LANES - FILT_EMB), (0, 0))), hy_filt_b1, hy_filt_w2, hy_filt_b2,
            hy_filt_w3, hy_filt_freq)
    cos_c, sin_c, fwd_c, inv_c = _dense_conv_tables(lp)
    kr_c, ki_c = _hyena_filters_ctx(lp, filt, cos_c, sin_c)
    hy_tables = _hyena_fft_tables(ls)
    hf_l, hb_l = _hyena_filters_lat(ls, filt)
    kr_l, ki_l = _filter_fft(hf_l, hb_l, hy_tables[0], hy_tables[1], ls)
    cs_c, dn_c = _fnet_channel_table(lp), _fnet_dense_table(lp)
    cs_l = _fnet_channel_table(ls)
    f1_l, f2_l = _fnet_fft_tables(ls)

    cak = cache_attn_k.reshape(bs, DEPTH, past, 128).transpose(0, 1, 3, 2).astype(BF16)
    cav = cache_attn_v.reshape(bs, DEPTH, past, 128).astype(BF16)
    cdk = cache_diff_k.reshape(bs, DEPTH, past, 256).transpose(0, 1, 3, 2).astype(BF16)
    cdv = jnp.concatenate([cache_diff_v, jnp.ones_like(cache_diff_v)], axis=-1)
    cdv = cdv.reshape(bs, DEPTH, past, 2 * GROUP_W).astype(BF16)

    xp = x_prompt.reshape(bp * lp, D_MODEL)
    xs = x_sample.reshape(bs * ls, D_MODEL)
    st_ak, st_av, st_dk, st_dv = [], [], [], []
    for l in range(DEPTH):
        (aq, akt, av, dq, dkt, dv, hu, fu, sg, ak32, av32, dk32, dv32) = _layer_entry(
            xp, bp, lp, mod_all, norm_pre, w_in_bf, l, None)
        outs = (_attn_a_ctx(aq, akt, av, sg, sink, bp, lp, l),
                _diff_attention(dq, [(dkt, dv)], sg, diff_lambda, diff_subln, bp, lp, l, lp),
                _hyena_ctx(hu, sg, hy_conv_w, hy_conv_b, hy_skip, kr_c, ki_c, fwd_c, inv_c, bp, lp, l),
                _fnet_ctx(fu, sg, fn_w, fn_b, cs_c, dn_c, bp, lp, l))
        st_ak.append(ak32.reshape(bp, lp, ATT_KV_HEADS, ATT_HEAD_DIM))
        st_av.append(av32.reshape(bp, lp, ATT_KV_HEADS, ATT_HEAD_DIM))
        st_dk.append(dk32.reshape(bp, lp, 2, DIF_HEADS, DIF_QK_DIM))
        st_dv.append(dv32.reshape(bp, lp, DIF_HEADS, DIF_V_DIM))
        xp = _layer_exit(xp, outs, w_out_bf, norm_post, mod_all, lp, l, False)

        (aq, akt, av, dq, dkt, dv, hu, fu, sg) = _layer_entry(xs, bs, ls, mod_all, norm_pre, w_in_bf, l, rope)
        outs = (_attn_a_latent(aq, akt, av, cak[:, l], cav[:, l], sg, sink, bs, ls, l),
                _diff_attention(dq, [(dkt, dv), (cdk[:, l], cdv[:, l])], sg, diff_lambda, diff_subln,
                                bs, ls, l, 256),
                _hyena_lat(hu, sg, hy_conv_w, hy_conv_b, hy_skip, kr_l, ki_l, hy_tables, bs, ls, l),
                _fnet_lat(fu, sg, fn_w, fn_b, cs_l, f1_l, f2_l, bs, ls, l))
        xs = _layer_exit(xs, outs, w_out_bf, norm_post, mod_all, ls, l, True)

    return (xp.reshape(bp, lp, D_MODEL), xs.reshape(bs, ls, D_MODEL),
            jnp.stack(st_ak, axis=1), jnp.stack(st_av, axis=1), jnp.stack(st_dk, axis=1), jnp.stack(st_dv, axis=1))
```

```python
import functools
import math

import numpy as np
import jax
import jax.numpy as jnp
from jax import lax
from jax.experimental import pallas as pl
from jax.experimental.pallas import tpu as pltpu

F32 = jnp.float32
BF16 = jnp.bfloat16

D_MODEL = 1024
DEPTH = 2
GRID_W = 64
GROUP_W = 256
ATT_HEADS = 4
ATT_KV_HEADS = 2
ATT_HEAD_DIM = 64
WINDOW = 128
BLOCK = 128
DIF_HEADS = 4
DIF_V_DIM = 64
DIF_QK_DIM = 32
HY_CH = 256
FILT_BANDS = 16
FILT_EMB = 1 + 2 * FILT_BANDS
FILT_HIDDEN = 64
HY_MIN_DECAY = math.log(1e-2) / 1.5
HY_MAX_DECAY = math.log(1e-2) / 0.3
FN_GROUP_CH = 64
ROPE_BASE = 10000.0
EPS = 1e-6
NEG_INF = -1e30

C_AQ, C_AK, C_AV, C_AG = 0, 256, 384, 512
C_DQ, C_DK, C_DV, C_DG = 768, 1024, 1280, 1536
C_HU, C_HG, C_FU, C_FG, D_IN = 1792, 2560, 2816, 3072, 3328
LOG2_E = math.log2(math.e)

LANES = 128
COND_ROWS = 8
CTX_ROW = 4
VMEM_LIMIT = 56 * 1024 * 1024
PITCH_PAD = 8

HY_N2 = 64
FN_N2 = 64


def _cparams(*sem):
    return pltpu.CompilerParams(dimension_semantics=sem, vmem_limit_bytes=VMEM_LIMIT)


def _dot(a, b):
    return jnp.dot(a, b, preferred_element_type=F32)


def _bdot(a, b):
    return jnp.dot(a.astype(BF16), b.astype(BF16), preferred_element_type=F32)


def _dot3(a, b):
    ah = a.astype(BF16)
    al = (a - ah.astype(F32)).astype(BF16)
    bh = b.astype(BF16)
    bl = (b - bh.astype(F32)).astype(BF16)
    return _dot(ah, bh) + _dot(ah, bl) + _dot(al, bh)


def _bf16_table(a):
    return jnp.asarray(a, F32).astype(BF16)


def _rope_tables(n, head_dim):
    pos = np.arange(n)
    row = (pos // GRID_W).astype(np.float64)
    col = (pos % GRID_W).astype(np.float64)
    n_freq = head_dim // 4
    inv = ROPE_BASE ** (-np.arange(n_freq, dtype=np.float64) / n_freq)
    ang = np.concatenate([row[:, None] * inv, col[:, None] * inv], axis=-1)
    reps = LANES // head_dim
    cos = np.tile(np.concatenate([np.cos(ang), np.cos(ang)], axis=-1), (1, reps))
    sin = np.tile(np.concatenate([-np.sin(ang), np.sin(ang)], axis=-1), (1, reps))
    return jnp.asarray(cos, F32), jnp.asarray(sin, F32)


def _filter_feats(n):
    t = np.linspace(0.0, 1.0, n)[:, None]
    w = (2.0 * math.pi / n) * np.arange(n)[:, None]
    f = np.linspace(1e-4, FILT_BANDS - 1, FILT_BANDS)[None, :]
    feats = np.concatenate([t, np.cos(f * w), -np.sin(f * w)], axis=-1)
    feats = np.pad(feats, ((0, 0), (0, LANES - FILT_EMB)))
    feats = np.concatenate([feats[:n // 2], feats[n // 2:]], axis=1)
    deltas = np.abs(np.linspace(HY_MIN_DECAY, HY_MAX_DECAY, HY_CH))
    decay = np.exp(-t * deltas[None, :])
    return jnp.asarray(feats, F32), jnp.asarray(decay, F32)


def _dense_conv_tables(n):
    big_l = 2 * n
    k = np.arange(big_l)[:, None]
    t = np.arange(n)[None, :]
    th = 2.0 * math.pi * k * t / big_l
    fwd = np.concatenate([np.cos(th), -np.sin(th)], axis=0)
    inv = np.concatenate([np.cos(th).T, -np.sin(th).T], axis=1) / big_l
    return (jnp.asarray(np.cos(th), F32), jnp.asarray(np.sin(th), F32), _bf16_table(fwd), _bf16_table(inv))


def _hyena_fft_tables(n):
    big_l = 2 * n
    n2 = HY_N2
    n1 = big_l // n2
    n1h = n1 // 2
    k1 = np.arange(n1)
    j1 = np.arange(n1h)
    j2 = np.arange(n2)
    k2 = np.arange(n2)
    th = 2.0 * math.pi * k1[None, :, None] * (j1[None, None, :] * n2 + j2[:, None, None]) / big_l
    m1 = np.concatenate([np.cos(th), -np.sin(th)], axis=1)
    th2 = 2.0 * math.pi * k2[:, None] * j2[None, :] / n2
    c2, s2 = np.cos(th2), np.sin(th2)
    w2 = np.block([[c2, s2], [-s2, c2]])
    tha = 2.0 * math.pi * j2[None, :, None] * (k1[:, None, None] + n1 * k2[None, None, :]) / big_l
    ca, sa = np.cos(tha), np.sin(tha)
    ma = np.concatenate([np.concatenate([ca, -sa], axis=2), np.concatenate([sa, ca], axis=2)], axis=1)
    thb = 2.0 * math.pi * j1[:, None] * k1[None, :] / n1
    mb = np.concatenate([np.cos(thb), -np.sin(thb)], axis=1) / big_l
    return _bf16_table(m1), _bf16_table(w2), _bf16_table(ma), _bf16_table(mb)


def _fnet_channel_table(n):
    m = np.arange(FN_GROUP_CH)
    th = 2.0 * math.pi * m[:, None] * m[None, :] / FN_GROUP_CH
    eye = np.eye(GROUP_W // FN_GROUP_CH)
    sc = 1.0 / math.sqrt(n * FN_GROUP_CH)
    cs = np.concatenate([np.kron(eye, np.cos(th)), np.kron(eye, np.sin(th))], axis=1) * sc
    return _bf16_table(cs)


def _fnet_dense_table(n):
    k = np.arange(n)
    th = 2.0 * math.pi * k[:, None] * k[None, :] / n
    return _bf16_table(np.concatenate([np.cos(th), -np.sin(th)], axis=1))


def _fnet_fft_tables(n):
    n2 = FN_N2
    n1 = n // n2
    k1 = np.arange(n1)
    j1 = np.arange(n1)
    j2 = np.arange(n2)
    k2 = np.arange(n2)
    th = 2.0 * math.pi * k1[None, :, None] * (j1[None, None, :] * n2 + j2[:, None, None]) / n
    c, s = np.cos(th), np.sin(th)
    f1 = np.concatenate([np.concatenate([c, -s], axis=2), np.concatenate([s, c], axis=2)], axis=1)
    th2 = 2.0 * math.pi * k2[:, None] * j2[None, :] / n2
    f2 = np.concatenate([np.cos(th2), -np.sin(th2)], axis=1)
    return _bf16_table(f1), _bf16_table(f2)


def _mod_kernel(cond_ref, w_ref, b_ref, o_ref):
    c = cond_ref[...]
    s = c * jax.nn.sigmoid(c)
    o_ref[...] = _dot3(s, w_ref[...]) + b_ref[...]


def _modulation(cond, w_ada, b_ada):
    tn = 1024
    return pl.pallas_call(
        _mod_kernel,
        grid=(DEPTH, 3 * D_MODEL // tn),
        in_specs=[pl.BlockSpec((COND_ROWS, D_MODEL), lambda l, j: (0, 0)),
                  pl.BlockSpec((None, D_MODEL, tn), lambda l, j: (l, 0, j)),
                  pl.BlockSpec((None, 1, tn), lambda l, j: (l, 0, j))],
        out_specs=pl.BlockSpec((None, COND_ROWS, tn), lambda l, j: (l, 0, j)),
        out_shape=jax.ShapeDtypeStruct((DEPTH, COND_ROWS, 3 * D_MODEL), F32),
        compiler_params=_cparams("arbitrary", "arbitrary"),
        name="modulation",
    )(cond, w_ada, b_ada.reshape(DEPTH, 1, 3 * D_MODEL))


def _rope(x, cos, sin, half):
    lane = lax.broadcasted_iota(jnp.int32, x.shape, 1)
    first = (lane % (2 * half)) < half
    partner = jnp.where(first, pltpu.roll(x, LANES - half, 1), pltpu.roll(x, half, 1))
    return x * cos + partner * sin


def _entry_kernel(*refs, latent, tiles_per_b):
    if latent:
        (x_ref, mod_ref, g_ref, w_ref, ca_ref, sa_ref, cd_ref, sd_ref,
         aq_ref, akt_ref, av_ref, dq_ref, dkt_ref, dv_ref, hu_ref, fu_ref, sg_ref) = refs
        row = pl.program_id(0) // tiles_per_b
    else:
        (x_ref, mod_ref, g_ref, w_ref,
         aq_ref, akt_ref, av_ref, dq_ref, dkt_ref, dv_ref, hu_ref, fu_ref, sg_ref,
         ak32_ref, av32_ref, dk32_ref, dv32_ref) = refs
        row = CTX_ROW
    m = mod_ref[pl.ds(row, 1), :]
    shift, scale = m[:, :D_MODEL], m[:, D_MODEL:2 * D_MODEL]
    x = x_ref[...]
    h = x * lax.rsqrt(jnp.mean(x * x, axis=-1, keepdims=True) + EPS) * g_ref[...]
    hb = (h * (1.0 + scale) + shift).astype(BF16)

    def proj(c0, c1):
        return _dot(hb, w_ref[:, c0:c1])

    def roped(p, cos_ref, sin_ref, half):
        if not latent:
            return p
        cos, sin = cos_ref[...], sin_ref[...]
        chunks = [_rope(p[:, j:j + LANES], cos, sin, half) for j in range(0, p.shape[1], LANES)]
        return chunks[0] if len(chunks) == 1 else jnp.concatenate(chunks, axis=1)

    ca = sa = cd = sd = None
    if latent:
        ca, sa, cd, sd = ca_ref, sa_ref, cd_ref, sd_ref

    aq_ref[...] = (roped(proj(C_AQ, C_AK), ca, sa, ATT_HEAD_DIM // 2) * (ATT_HEAD_DIM ** -0.5 * LOG2_E)).astype(BF16)
    ak = roped(proj(C_AK, C_AV), ca, sa, ATT_HEAD_DIM // 2)
    akt_ref[...] = ak.T.astype(BF16)
    av = proj(C_AV, C_AG)
    av_ref[...] = av.astype(BF16)
    dq_ref[...] = (roped(proj(C_DQ, C_DK), cd, sd, DIF_QK_DIM // 2) * (DIF_QK_DIM ** -0.5 * LOG2_E)).astype(BF16)
    dk = roped(proj(C_DK, C_DV), cd, sd, DIF_QK_DIM // 2)
    dkt_ref[...] = dk.T.astype(BF16)
    dv = proj(C_DV, C_DG)
    value_lane = lax.broadcasted_iota(jnp.int32, (dv.shape[0], LANES), 1) < DIF_V_DIM
    for h in range(DIF_HEADS):
        pair = dv[:, (h // 2) * LANES:(h // 2 + 1) * LANES]
        if h % 2:
            pair = pltpu.roll(pair, DIF_V_DIM, 1)
        dv_ref[:, h * LANES:(h + 1) * LANES] = jnp.where(value_lane, pair, 1.0).astype(BF16)
    if not latent:
        ak32_ref[...] = ak
        av32_ref[...] = av
        dk32_ref[...] = dk
        dv32_ref[...] = dv
    hu_ref[...] = proj(C_HU, C_HG).astype(BF16)
    fu_ref[...] = proj(C_FU, C_FG).astype(BF16)
    for j, c0 in enumerate((C_AG, C_DG, C_HG, C_FG)):
        g = proj(c0, c0 + GROUP_W)
        sg_ref[:, j * GROUP_W:(j + 1) * GROUP_W] = (g * jax.nn.sigmoid(g)).astype(BF16)


def _layer_entry(x2d, bsz, n, mod_all, g_pre, w_in_bf, layer, rope):
    latent = rope is not None
    t = bsz * n
    tm = 512 if latent else n
    tiles_per_b = n // tm
    grid = (t // tm,)
    row = lambda i: (i, 0)
    kt_map = lambda i: (i // tiles_per_b, 0, i % tiles_per_b)
    in_specs = [pl.BlockSpec((tm, D_MODEL), row),
                pl.BlockSpec((None, COND_ROWS, 3 * D_MODEL), lambda i: (layer, 0, 0)),
                pl.BlockSpec((None, 1, D_MODEL), lambda i: (layer, 0, 0)),
                pl.BlockSpec((None, D_MODEL, D_IN), lambda i: (layer, 0, 0))]
    args = [x2d, mod_all, g_pre.reshape(DEPTH, 1, D_MODEL), w_in_bf]
    if latent:
        pos = lambda i: (i % tiles_per_b, 0)
        in_specs += [pl.BlockSpec((tm, LANES), pos)] * 4
        args += list(rope)
    bf = lambda c: jax.ShapeDtypeStruct((t, c), BF16)
    out_shape = [bf(256), jax.ShapeDtypeStruct((bsz, 128, n), BF16), bf(128),
                 bf(256), jax.ShapeDtypeStruct((bsz, 256, n), BF16), bf(512),
                 bf(768), bf(256), bf(1024)]
    rs = lambda c: pl.BlockSpec((tm, c), row)
    out_specs = [rs(256), pl.BlockSpec((None, 128, tm), kt_map), rs(128),
                 rs(256), pl.BlockSpec((None, 256, tm), kt_map), rs(512),
                 rs(768), rs(256), rs(1024)]
    if not latent:
        out_shape += [jax.ShapeDtypeStruct((t, c), F32) for c in (128, 128, 256, 256)]
        out_specs += [rs(128), rs(128), rs(256), rs(256)]
    return pl.pallas_call(
        functools.partial(_entry_kernel, latent=latent, tiles_per_b=tiles_per_b),
        grid=grid, in_specs=in_specs, out_specs=out_specs, out_shape=out_shape,
        compiler_params=_cparams("arbitrary"),
        name="entry_latent" if latent else "entry_ctx",
    )(*args)


def _sink_columns(sink_ref, layer, tq):
    grp = ATT_HEADS // ATT_KV_HEADS
    head_row = lax.broadcasted_iota(jnp.int32, (grp * tq, 1), 0) // tq
    cols = []
    for g in range(ATT_KV_HEADS):
        col = jnp.zeros((grp * tq, 1), F32)
        for j in range(grp):
            col = jnp.where(head_row == j, sink_ref[layer * ATT_HEADS + g * grp + j] * LOG2_E, col)
        cols.append(col)
    return cols


def _stack_heads(q_ref, rows, g):
    grp = ATT_HEADS // ATT_KV_HEADS
    return jnp.concatenate([q_ref[rows, (g * grp + t) * ATT_HEAD_DIM:(g * grp + t + 1) * ATT_HEAD_DIM]
                            for t in range(grp)], axis=0)


def _attn_a_kernel(sink_ref, q_ref, k_ref, v_ref, sg_ref, o_ref, *, layer):
    tq = q_ref.shape[0]
    grp = ATT_HEADS // ATT_KV_HEADS
    sink_cols = _sink_columns(sink_ref, layer, tq)
    outs = []
    for g in range(ATT_KV_HEADS):
        d0 = g * ATT_HEAD_DIM
        s = _dot(_stack_heads(q_ref, slice(None), g), k_ref[d0:d0 + ATT_HEAD_DIM, :])
        m = jnp.maximum(jnp.max(s, axis=-1, keepdims=True), sink_cols[g])
        p = jnp.exp2(s - m)
        l = jnp.sum(p, axis=-1, keepdims=True) + jnp.exp2(sink_cols[g] - m)
        o = _dot(p.astype(BF16), v_ref[:, d0:d0 + ATT_HEAD_DIM]) / l
        outs += [o[j * tq:(j + 1) * tq] for j in range(grp)]
    o_full = jnp.concatenate(outs, axis=1)
    o_ref[...] = (o_full * sg_ref[...].astype(F32)).astype(BF16)


def _attn_a_win_kernel(sink_ref, q_ref, kp_ref, kc_ref, kn_ref, vp_ref, vc_ref, vn_ref, ck_ref, cv_ref, sg_ref,
                       o_ref, *, layer, nsteps, qb):
    i = pl.program_id(1)
    grp = ATT_HEADS // ATT_KV_HEADS
    rows = grp * BLOCK
    kw = jnp.concatenate([kp_ref[...], kc_ref[...], kn_ref[...]], axis=1)
    vw = jnp.concatenate([vp_ref[...], vc_ref[...], vn_ref[...]], axis=0)
    r = lax.broadcasted_iota(jnp.int32, (rows, 3 * BLOCK), 0) % BLOCK
    c = lax.broadcasted_iota(jnp.int32, (rows, 3 * BLOCK), 1)
    band = (c >= r) & (c <= r + 2 * WINDOW)
    masks = {0: band & ((c >= BLOCK) | (i > 0)), qb - 1: band & ((c < 2 * BLOCK) | (i < nsteps - 1))}
    sink_cols = _sink_columns(sink_ref, layer, BLOCK)
    ck = ck_ref[...]
    cv = cv_ref[...]
    items = [(j, g) for j in range(qb) for g in range(ATT_KV_HEADS)]
    st_a, st_b, outs = {}, {}, {}
    for step in range(len(items) + 2):
        a, b, cc = step, step - 1, step - 2
        if a < len(items):
            j, g = items[a]
            d0 = g * ATT_HEAD_DIM
            q2 = _stack_heads(q_ref, slice(j * BLOCK, (j + 1) * BLOCK), g)
            s_w = _dot(q2, kw[d0:d0 + ATT_HEAD_DIM, j * BLOCK:(j + 3) * BLOCK])
            s_w = jnp.where(masks.get(j, band), s_w, NEG_INF)
            s_c = _dot(q2, ck[d0:d0 + ATT_HEAD_DIM, :])
            m = jnp.maximum(jnp.maximum(jnp.max(s_w, axis=-1, keepdims=True),
                                        jnp.max(s_c, axis=-1, keepdims=True)), sink_cols[g])
            st_a[a] = (s_w, s_c, m)
        if 0 <= b < len(items):
            s_w, s_c, m = st_a.pop(b)
            p_w = jnp.exp2(s_w - m)
            p_c = jnp.exp2(s_c - m)
            l = (jnp.sum(p_w, axis=-1, keepdims=True) + jnp.sum(p_c, axis=-1, keepdims=True)
                 + jnp.exp2(sink_cols[items[b][1]] - m))
            st_b[b] = (p_w.astype(BF16), p_c.astype(BF16), l)
        if 0 <= cc < len(items):
            j, g = items[cc]
            d0 = g * ATT_HEAD_DIM
            p_w, p_c, l = st_b.pop(cc)
            o = (_dot(p_w, vw[j * BLOCK:(j + 3) * BLOCK, d0:d0 + ATT_HEAD_DIM])
                 + _dot(p_c, cv[:, d0:d0 + ATT_HEAD_DIM])) / l
            outs.setdefault(j, []).extend([o[t * BLOCK:(t + 1) * BLOCK] for t in range(grp)])
            if g == ATT_KV_HEADS - 1:
                rs = slice(j * BLOCK, (j + 1) * BLOCK)
                o_full = jnp.concatenate(outs.pop(j), axis=1)
                o_ref[rs, :] = (o_full * sg_ref[rs, :].astype(F32)).astype(BF16)


def _attn_a_latent(aq, akt, av, ck_t, cv, sg, sink, bsz, n, layer, qb=4):
    nb = n // BLOCK
    nsteps = nb // qb
    qmap = lambda b, i: (b * nsteps + i, 0)
    prev = lambda i: jnp.maximum(qb * i - 1, 0)
    nxt = lambda i: jnp.minimum(qb * i + qb, nb - 1)
    kedge = lambda f: pl.BlockSpec((None, 128, BLOCK), lambda b, i: (b, 0, f(i)))
    vedge = lambda f: pl.BlockSpec((BLOCK, 128), lambda b, i: (b * nb + f(i), 0))
    return pl.pallas_call(
        functools.partial(_attn_a_win_kernel, layer=layer, nsteps=nsteps, qb=qb),
        grid=(bsz, nsteps),
        in_specs=[pl.BlockSpec(memory_space=pltpu.SMEM),
                  pl.BlockSpec((qb * BLOCK, 256), qmap),
                  kedge(prev), pl.BlockSpec((None, 128, qb * BLOCK), lambda b, i: (b, 0, i)), kedge(nxt),
                  vedge(prev), pl.BlockSpec((qb * BLOCK, 128), qmap), vedge(nxt),
                  pl.BlockSpec((None, 128, ck_t.shape[2]), lambda b, i: (b, 0, 0)),
                  pl.BlockSpec((None, cv.shape[1], 128), lambda b, i: (b, 0, 0)),
                  pl.BlockSpec((qb * BLOCK, 256), qmap)],
        out_specs=pl.BlockSpec((qb * BLOCK, 256), qmap),
        out_shape=jax.ShapeDtypeStruct((bsz * n, 256), BF16),
        compiler_params=_cparams("arbitrary", "arbitrary"),
        name="attn_a_latent",
    )(sink, aq, akt, akt, akt, av, av, av, ck_t, cv, sg)


def _attn_a_ctx(aq, akt, av, sg, sink, bsz, n, layer):
    return pl.pallas_call(
        functools.partial(_attn_a_kernel, layer=layer),
        grid=(bsz,),
        in_specs=[pl.BlockSpec(memory_space=pltpu.SMEM),
                  pl.BlockSpec((n, 256), lambda b: (b, 0)),
                  pl.BlockSpec((None, 128, n), lambda b: (b, 0, 0)),
                  pl.BlockSpec((n, 128), lambda b: (b, 0)),
                  pl.BlockSpec((n, 256), lambda b: (b, 0))],
        out_specs=pl.BlockSpec((n, 256), lambda b: (b, 0)),
        out_shape=jax.ShapeDtypeStruct((bsz * n, 256), BF16),
        compiler_params=_cparams("arbitrary"),
        name="attn_a_ctx",
    )(sink, aq, akt, av, sg)


def _diff_kernel(lam_ref, subln_ref, q_ref, *refs, lam_init, npieces, tq_sub, kc):
    kts = refs[0:2 * npieces:2]
    vs = refs[1:2 * npieces:2]
    sg_ref, o_ref, s_scr, p_scr = refs[2 * npieces:2 * npieces + 4]
    lp = lam_ref[...]
    lam = (jnp.exp(jnp.sum(lp[0:1] * lp[1:2], axis=-1, keepdims=True))
           - jnp.exp(jnp.sum(lp[2:3] * lp[3:4], axis=-1, keepdims=True)) + lam_init)
    subln = subln_ref[...]
    tq = q_ref.shape[0]
    chunks, off = [], 0
    for kt, v in zip(kts, vs):
        for c0 in range(0, kt.shape[1], kc):
            w = min(kc, kt.shape[1] - c0)
            chunks.append((kt, v, c0, w, off))
            off += w
    items = [(r0, h, mp) for r0 in range(0, tq, tq_sub) for h in range(DIF_HEADS) for mp in range(2)]
    n_items = len(items)
    mx, acc, om, outs = {}, {}, {}, {}

    def fold(s):
        return functools.reduce(jnp.maximum, [s[:, j:j + LANES] for j in range(0, s.shape[1], LANES)])

    nslot = s_scr.shape[0]
    lag = nslot - 1
    for step in range(n_items + 2 * lag):
        a = step if step < n_items else None
        b = step - lag if 0 <= step - lag < n_items else None
        c = step - 2 * lag if 0 <= step - 2 * lag < n_items else None
        if a is not None:
            r0, h, mp = items[a]
            f0 = (mp * DIF_HEADS + h) * DIF_QK_DIM
            q_a = q_ref[r0:r0 + tq_sub, f0:f0 + DIF_QK_DIM]
            mx[a] = jnp.full((tq_sub, LANES), NEG_INF, F32)
        if b is not None:
            m_b = jnp.max(mx.pop(b), axis=-1, keepdims=True)
        if c is not None:
            acc[c] = jnp.zeros((tq_sub, 2 * DIF_V_DIM), F32)
            hc = items[c][1]
        for kt, v, c0, w, o0 in chunks:
            if a is not None:
                s = _dot(q_a, kt[f0:f0 + DIF_QK_DIM, c0:c0 + w])
                s_scr[a % nslot, :, o0:o0 + w] = s
                mx[a] = jnp.maximum(mx[a], fold(s))
            if b is not None:
                p_scr[b % nslot, :, o0:o0 + w] = jnp.exp2(s_scr[b % nslot, :, o0:o0 + w] - m_b).astype(BF16)
            if c is not None:
                acc[c] = acc[c] + _dot(p_scr[c % nslot, :, o0:o0 + w],
                                       v[c0:c0 + w, 2 * DIF_V_DIM * hc:2 * DIF_V_DIM * (hc + 1)])
        if c is not None:
            r0, h, mp = items[c]
            o = acc.pop(c)
            om[(r0, h, mp)] = o[:, :DIF_V_DIM] / o[:, DIF_V_DIM:DIF_V_DIM + 1]
            if mp == 1:
                av = om.pop((r0, h, 0)) - lam * om.pop((r0, h, 1))
                y = av * lax.rsqrt(jnp.mean(av * av, axis=-1, keepdims=True) + EPS) * subln * (1.0 - lam_init)
                outs.setdefault(r0, []).append(y)
                if h == DIF_HEADS - 1:
                    o_full = jnp.concatenate(outs.pop(r0), axis=1)
                    o_ref[r0:r0 + tq_sub, :] = (o_full * sg_ref[r0:r0 + tq_sub, :].astype(F32)).astype(BF16)


def _diff_attention(dq, pieces, sg, diff_lambda, diff_subln, bsz, n, layer, tq, kc=512):
    lam_init = 0.8 - 0.6 * math.exp(-0.3 * layer)
    nt = n // tq
    nk_all = sum(kt.shape[2] for kt, _ in pieces)
    nslot = 2 if nk_all > kc else 2 * DIF_HEADS + 1
    in_specs = [pl.BlockSpec((None, 4, DIF_QK_DIM), lambda b, i: (layer, 0, 0)),
                pl.BlockSpec((None, 1, DIF_V_DIM), lambda b, i: (layer, 0, 0)),
                pl.BlockSpec((tq, 256), lambda b, i: (b * nt + i, 0))]
    args = [diff_lambda, diff_subln.reshape(DEPTH, 1, DIF_V_DIM), dq]
    for kt, v in pieces:
        nk = kt.shape[2]
        in_specs.append(pl.BlockSpec((None, 256, nk), lambda b, i: (b, 0, 0)))
        if v.ndim == 3:
            in_specs.append(pl.BlockSpec((None, nk, 512), lambda b, i: (b, 0, 0)))
        else:
            in_specs.append(pl.BlockSpec((nk, 512), lambda b, i: (b, 0)))
        args += [kt, v]
    in_specs.append(pl.BlockSpec((tq, 256), lambda b, i: (b * nt + i, 1)))
    args.append(sg)
    return pl.pallas_call(
        functools.partial(_diff_kernel, lam_init=lam_init, npieces=len(pieces), tq_sub=tq, kc=kc),
        grid=(bsz, nt), in_specs=in_specs,
        out_specs=pl.BlockSpec((tq, 256), lambda b, i: (b * nt + i, 0)),
        out_shape=jax.ShapeDtypeStruct((bsz * n, 256), BF16),
        scratch_shapes=[pltpu.VMEM((nslot, tq, nk_all), F32), pltpu.VMEM((nslot, tq, nk_all), BF16)],
        compiler_params=_cparams("arbitrary", "arbitrary"),
        name="diff_attn_%d" % len(pieces),
    )(*args)


def _filter_core(feats_ref, w1_ref, b1_ref, w2_ref, b2_ref, w3_ref, fr_ref, decay_ref):
    fr = fr_ref[...]
    h = jnp.sin(fr * (_dot3(feats_ref[...], w1_ref[...]) + b1_ref[...]))
    h = jnp.sin(fr * (_dot3(h, w2_ref[...]) + b2_ref[...]))
    h = _dot3(h, w3_ref[...])
    h = jnp.concatenate([h[:, :2 * HY_CH], h[:, 2 * HY_CH:]], axis=0)
    decay = decay_ref[...]
    hf = h[:, :HY_CH] * decay
    hb = h[:, HY_CH:] * decay
    tot = (jnp.sum(jnp.abs(hf), axis=0, keepdims=True) + jnp.sum(jnp.abs(hb), axis=0, keepdims=True) + EPS)
    hf = hf / tot
    hb = hb / tot
    rowi = lax.broadcasted_iota(jnp.int32, hb.shape, 0)
    return hf, jnp.where(rowi == 0, 0.0, hb)


def _filter_ctx_kernel(feats_ref, w1_ref, b1_ref, w2_ref, b2_ref, w3_ref, fr_ref, decay_ref,
                       cos_ref, sin_ref, kr_ref, ki_ref):
    hf, hb0 = _filter_core(feats_ref, w1_ref, b1_ref, w2_ref, b2_ref, w3_ref, fr_ref, decay_ref)
    kr_ref[...] = _dot3(cos_ref[...], hf + hb0)
    ki_ref[...] = _dot3(sin_ref[...], hb0 - hf)


def _filter_lat_kernel(feats_ref, w1_ref, b1_ref, w2_ref, b2_ref, w3_ref, fr_ref, decay_ref, hf_ref, hb_ref):
    hf, hb0 = _filter_core(feats_ref, w1_ref, b1_ref, w2_ref, b2_ref, w3_ref, fr_ref, decay_ref)
    hf_ref[...] = hf
    hb_ref[...] = hb0


def _filter_specs(n, w1p, b1, w2, b2, w3, freq, feats, decay):
    const = lambda shape: pl.BlockSpec(shape, lambda l: (0,) * len(shape))
    lay = lambda a, b: pl.BlockSpec((None, a, b), lambda l: (l, 0, 0))
    hid = 2 * FILT_HIDDEN
    in_specs = [const((n // 2, 2 * LANES)), lay(2 * LANES, hid), lay(1, hid), lay(hid, hid), lay(1, hid),
                lay(hid, 4 * HY_CH), lay(1, hid), const((n, HY_CH))]
    row2 = lambda a: jnp.tile(a.reshape(DEPTH, 1, FILT_HIDDEN), (1, 1, 2))
    args = [feats, _block_diag2(w1p), row2(b1), _block_diag2(w2), row2(b2), _block_diag2(w3), row2(freq), decay]
    return in_specs, args


def _block_diag2(w):
    z = jnp.zeros_like(w)
    return jnp.concatenate([jnp.concatenate([w, z], axis=2), jnp.concatenate([z, w], axis=2)], axis=1)


def _hyena_filters_ctx(n, filt, cos_t, sin_t):
    feats, decay = _filter_feats(n)
    in_specs, args = _filter_specs(n, *filt, feats, decay)
    big_l = 2 * n
    in_specs += [pl.BlockSpec((big_l, n), lambda l: (0, 0))] * 2
    out = pl.BlockSpec((None, big_l, HY_CH), lambda l: (l, 0, 0))
    return pl.pallas_call(
        _filter_ctx_kernel, grid=(DEPTH,), in_specs=in_specs, out_specs=[out, out],
        out_shape=[jax.ShapeDtypeStruct((DEPTH, big_l, HY_CH), F32)] * 2,
        compiler_params=_cparams("arbitrary"), name="hyena_filter_ctx",
    )(*args, cos_t, sin_t)


def _hyena_filters_lat(n, filt):
    feats, decay = _filter_feats(n)
    in_specs, args = _filter_specs(n, *filt, feats, decay)
    out = pl.BlockSpec((None, n, HY_CH), lambda l: (l, 0, 0))
    return pl.pallas_call(
        _filter_lat_kernel, grid=(DEPTH,), in_specs=in_specs, out_specs=[out, out],
        out_shape=[jax.ShapeDtypeStruct((DEPTH, n, HY_CH), F32)] * 2,
        compiler_params=_cparams("arbitrary"), name="hyena_filter_lat",
    )(*args)


def _short_conv(u_ref, cw_ref, cb_ref):
    u = u_ref[...].astype(F32)
    n = u.shape[0]
    rowi = lax.broadcasted_iota(jnp.int32, u.shape, 0)
    up = jnp.where(rowi == 0, 0.0, pltpu.roll(u, 1, 0))
    un = jnp.where(rowi == n - 1, 0.0, pltpu.roll(u, n - 1, 0))
    w = cw_ref[...]
    return up * w[0:1] + u * w[1:2] + un * w[2:3] + cb_ref[...]


def _hyena_ctx_kernel(hu_ref, cw_ref, cb_ref, kr_ref, ki_ref, fwd_ref, inv_ref, skip_ref, sg_ref, o_ref):
    uc = _short_conv(hu_ref, cw_ref, cb_ref)
    x0, x1, v = uc[:, :HY_CH], uc[:, HY_CH:2 * HY_CH], uc[:, 2 * HY_CH:]
    z = x1 * v
    zf = _dot(fwd_ref[...], z.astype(BF16))
    big_l = zf.shape[0] // 2
    zr, zi = zf[:big_l], zf[big_l:]
    kr, ki = kr_ref[...], ki_ref[...]
    p = jnp.concatenate([zr * kr - zi * ki, zr * ki + zi * kr], axis=0).astype(BF16)
    y = _dot(inv_ref[...], p)
    out = x0 * (y + z * skip_ref[...])
    o_ref[...] = (out * sg_ref[...].astype(F32)).astype(BF16)


def _hyena_ctx(hu, sg, conv_w, conv_b, skip, kr, ki, fwd_t, inv_t, bsz, n, layer):
    big_l = 2 * n
    lay = lambda a, b: pl.BlockSpec((None, a, b), lambda i: (layer, 0, 0))
    return pl.pallas_call(
        _hyena_ctx_kernel, grid=(bsz,),
        in_specs=[pl.BlockSpec((n, 768), lambda i: (i, 0)),
                  lay(3, 768), lay(1, 768), lay(big_l, HY_CH), lay(big_l, HY_CH),
                  pl.BlockSpec((2 * big_l, n), lambda i: (0, 0)),
                  pl.BlockSpec((n, 2 * big_l), lambda i: (0, 0)),
                  lay(1, HY_CH),
                  pl.BlockSpec((n, 256), lambda i: (i, 2))],
        out_specs=pl.BlockSpec((n, 256), lambda i: (i, 0)),
        out_shape=jax.ShapeDtypeStruct((bsz * n, 256), BF16),
        compiler_params=_cparams("arbitrary"), name="hyena_ctx",
    )(hu, conv_w, conv_b.reshape(DEPTH, 1, 768), kr, ki, fwd_t, inv_t, skip.reshape(DEPTH, 1, HY_CH), sg)


def _pitch(rows):
    return rows + PITCH_PAD


def _to_pitched(dst, src, blk):
    for i in range(src.shape[0] // blk):
        dst[i * _pitch(blk):i * _pitch(blk) + blk, :] = src[i * blk:(i + 1) * blk, :]


def _fft_stage1(src, m1_ref, s_re, s_im, n1, n2):
    for j2 in range(n2):
        x = src[pl.ds(j2, n1 // 2, stride=_pitch(n2)), :]
        a = _dot(m1_ref[j2], x.astype(BF16))
        s_re[j2 * _pitch(n1):j2 * _pitch(n1) + n1, :] = a[:n1]
        s_im[j2 * _pitch(n1):j2 * _pitch(n1) + n1, :] = a[n1:]


def _fft_stage2(w2_ref, s_re, s_im, k1, n1, n2):
    re = s_re[pl.ds(k1, n2, stride=_pitch(n1)), :]
    im = s_im[pl.ds(k1, n2, stride=_pitch(n1)), :]
    x = _dot(w2_ref[...], jnp.concatenate([re, im], axis=0).astype(BF16))
    return x[:n2], x[n2:]


def _filter_fft_kernel(hf_ref, hb_ref, m1_ref, w2_ref, kr_ref, ki_ref, x_scr, s_re, s_im, *, n1, n2):
    _to_pitched(x_scr, hf_ref, n2)
    _fft_stage1(x_scr, m1_ref, s_re, s_im, n1, n2)
    for k1 in range(n1):
        xr, xi = _fft_stage2(w2_ref, s_re, s_im, k1, n1, n2)
        kr_ref[k1 * n2:(k1 + 1) * n2, :] = xr
        ki_ref[k1 * n2:(k1 + 1) * n2, :] = xi
    _to_pitched(x_scr, hb_ref, n2)
    _fft_stage1(x_scr, m1_ref, s_re, s_im, n1, n2)
    for k1 in range(n1):
        xr, xi = _fft_stage2(w2_ref, s_re, s_im, k1, n1, n2)
        kr_ref[k1 * n2:(k1 + 1) * n2, :] += xr
        ki_ref[k1 * n2:(k1 + 1) * n2, :] -= xi


def _filter_fft(hf, hb0, m1, w2, n):
    big_l = 2 * n
    n2 = HY_N2
    n1 = big_l // n2
    inp = pl.BlockSpec((None, n, LANES), lambda l, c: (l, 0, c))
    out = pl.BlockSpec((None, big_l, LANES), lambda l, c: (l, 0, c))
    return pl.pallas_call(
        functools.partial(_filter_fft_kernel, n1=n1, n2=n2),
        grid=(DEPTH, HY_CH // LANES),
        in_specs=[inp, inp,
                  pl.BlockSpec(m1.shape, lambda l, c: (0, 0, 0)),
                  pl.BlockSpec(w2.shape, lambda l, c: (0, 0))],
        out_specs=[out, out],
        out_shape=[jax.ShapeDtypeStruct((DEPTH, big_l, HY_CH), F32)] * 2,
        scratch_shapes=[pltpu.VMEM((n1 // 2 * _pitch(n2), LANES), F32)]
                       + [pltpu.VMEM((n2 * _pitch(n1), LANES), F32)] * 2,
        compiler_params=_cparams("arbitrary", "arbitrary"), name="hyena_filter_fft",
    )(hf, hb0, m1, w2)


def _hyena_lat_kernel(x0_ref, x1_ref, v_ref, cw0_ref, cw1_ref, cw2_ref, cb0_ref, cb1_ref, cb2_ref,
                      kr_ref, ki_ref, m1_ref, w2_ref, ma_ref, mb_ref, skip_ref, sg_ref, o_ref,
                      z_scr, y_scr, s_re, s_im, *, n1, n2):
    n1h = n1 // 2
    z = _short_conv(x1_ref, cw1_ref, cb1_ref) * _short_conv(v_ref, cw2_ref, cb2_ref)
    _to_pitched(z_scr, z, n2)
    _fft_stage1(z_scr, m1_ref, s_re, s_im, n1, n2)
    for k1 in range(n1):
        xr, xi = _fft_stage2(w2_ref, s_re, s_im, k1, n1, n2)
        kr = kr_ref[k1 * n2:(k1 + 1) * n2, :]
        ki = ki_ref[k1 * n2:(k1 + 1) * n2, :]
        p = jnp.concatenate([xr * kr - xi * ki, xr * ki + xi * kr], axis=0).astype(BF16)
        b = _dot(ma_ref[k1], p)
        s_re[pl.ds(k1, n2, stride=_pitch(n1)), :] = b[:n2]
        s_im[pl.ds(k1, n2, stride=_pitch(n1)), :] = b[n2:]
    for j2 in range(n2):
        r0 = j2 * _pitch(n1)
        blk = jnp.concatenate([s_re[r0:r0 + n1, :], s_im[r0:r0 + n1, :]], axis=0)
        y_scr[pl.ds(j2, n1h, stride=_pitch(n2)), :] = _dot(mb_ref[...], blk.astype(BF16))
    x0 = _short_conv(x0_ref, cw0_ref, cb0_ref)
    skip = skip_ref[...]
    for j1 in range(n1h):
        rows = slice(j1 * n2, (j1 + 1) * n2)
        prow = slice(j1 * _pitch(n2), j1 * _pitch(n2) + n2)
        out = x0[rows] * (y_scr[prow, :] + z_scr[prow, :] * skip)
        o_ref[rows, :] = (out * sg_ref[rows, :].astype(F32)).astype(BF16)


def _hyena_lat(hu, sg, conv_w, conv_b, skip, kr, ki, tables, bsz, n, layer):
    m1, w2, ma, mb = tables
    big_l = 2 * n
    n2 = HY_N2
    n1 = big_l // n2
    nch = HY_CH // LANES
    ucol = lambda s: pl.BlockSpec((n, LANES), lambda c, b: (b, s * nch + c))
    wcol = lambda s: pl.BlockSpec((None, 3, LANES), lambda c, b: (layer, 0, s * nch + c))
    bcol = lambda s: pl.BlockSpec((None, 1, LANES), lambda c, b: (layer, 0, s * nch + c))
    kspec = pl.BlockSpec((None, big_l, LANES), lambda c, b: (layer, 0, c))
    const = lambda a: pl.BlockSpec(a.shape, lambda c, b: (0,) * a.ndim)
    cb3 = conv_b.reshape(DEPTH, 1, 768)
    return pl.pallas_call(
        functools.partial(_hyena_lat_kernel, n1=n1, n2=n2),
        grid=(nch, bsz),
        in_specs=[ucol(0), ucol(1), ucol(2), wcol(0), wcol(1), wcol(2), bcol(0), bcol(1), bcol(2),
                  kspec, kspec, const(m1), const(w2), const(ma), const(mb),
                  pl.BlockSpec((None, 1, LANES), lambda c, b: (layer, 0, c)),
                  pl.BlockSpec((n, LANES), lambda c, b: (b, 2 * nch + c))],
        out_specs=pl.BlockSpec((n, LANES), lambda c, b: (b, c)),
        out_shape=jax.ShapeDtypeStruct((bsz * n, 256), BF16),
        scratch_shapes=[pltpu.VMEM((n1 // 2 * _pitch(n2), LANES), F32)] * 2
                       + [pltpu.VMEM((n2 * _pitch(n1), LANES), F32)] * 2,
        compiler_params=_cparams("arbitrary", "arbitrary"), name="hyena_lat",
    )(hu, hu, hu, conv_w, conv_w, conv_w, cb3, cb3, cb3, kr, ki, m1, w2, ma, mb,
      skip.reshape(DEPTH, 1, HY_CH), sg)


def _fnet_ctx_kernel(u_ref, cs_ref, dn_ref, fw_ref, fb_ref, sg_ref, o_ref):
    ab = _dot(u_ref[...], cs_ref[...])
    stack = jnp.concatenate([ab[:, :GROUP_W], ab[:, GROUP_W:]], axis=0).astype(BF16)
    f = _dot(dn_ref[...], stack)
    out = _bdot(f, fw_ref[...]) + fb_ref[...]
    o_ref[...] = (out * sg_ref[...].astype(F32)).astype(BF16)


def _fnet_ctx(fu, sg, fn_w, fn_b, cs, dn, bsz, n, layer):
    return pl.pallas_call(
        _fnet_ctx_kernel, grid=(bsz,),
        in_specs=[pl.BlockSpec((n, 256), lambda i: (i, 0)),
                  pl.BlockSpec(cs.shape, lambda i: (0, 0)),
                  pl.BlockSpec(dn.shape, lambda i: (0, 0)),
                  pl.BlockSpec((None, GROUP_W, GROUP_W), lambda i: (layer, 0, 0)),
                  pl.BlockSpec((None, 1, GROUP_W), lambda i: (layer, 0, 0)),
                  pl.BlockSpec((n, 256), lambda i: (i, 3))],
        out_specs=pl.BlockSpec((n, 256), lambda i: (i, 0)),
        out_shape=jax.ShapeDtypeStruct((bsz * n, 256), BF16),
        compiler_params=_cparams("arbitrary"), name="fnet_ctx",
    )(fu, cs, dn, fn_w, fn_b.reshape(DEPTH, 1, GROUP_W), sg)


def _fnet_lat_kernel(u_ref, cs_ref, f1_ref, f2_ref, fw_ref, fb_ref, sg_ref, o_ref,
                     a_scr, b_scr, s_re, s_im, o_scr, *, n1, n2):
    halves = GROUP_W // LANES

    def put(scr, rows, val):
        for hh in range(halves):
            scr[hh, rows, :] = val[:, hh * LANES:(hh + 1) * LANES]

    def get(scr, rows):
        return jnp.concatenate([scr[hh, rows, :] for hh in range(halves)], axis=1)

    def block(i, blk):
        return slice(i * _pitch(blk), i * _pitch(blk) + blk)

    ab = _dot(u_ref[...], cs_ref[...])
    for j1 in range(n1):
        put(a_scr, block(j1, n2), ab[j1 * n2:(j1 + 1) * n2, :GROUP_W])
        put(b_scr, block(j1, n2), ab[j1 * n2:(j1 + 1) * n2, GROUP_W:])
    for j2 in range(n2):
        rows = pl.ds(j2, n1, stride=_pitch(n2))
        g = jnp.concatenate([get(a_scr, rows), get(b_scr, rows)], axis=0)
        t = _dot(f1_ref[j2], g.astype(BF16))
        put(s_re, block(j2, n1), t[:n1])
        put(s_im, block(j2, n1), t[n1:])
    for k1 in range(n1):
        rows = pl.ds(k1, n2, stride=_pitch(n1))
        g = jnp.concatenate([get(s_re, rows), get(s_im, rows)], axis=0)
        put(o_scr, rows, _dot(f2_ref[...], g.astype(BF16)))
    fw = fw_ref[...].astype(BF16)
    fb = fb_ref[...]
    for k2 in range(n2):
        rows = slice(k2 * n1, (k2 + 1) * n1)
        out = _dot(get(o_scr, block(k2, n1)).astype(BF16), fw) + fb
        o_ref[rows, :] = (out * sg_ref[rows, :].astype(F32)).astype(BF16)


def _fnet_lat(fu, sg, fn_w, fn_b, cs, f1, f2, bsz, n, layer):
    n2 = FN_N2
    n1 = n // n2
    return pl.pallas_call(
        functools.partial(_fnet_lat_kernel, n1=n1, n2=n2), grid=(bsz,),
        in_specs=[pl.BlockSpec((n, 256), lambda i: (i, 0)),
                  pl.BlockSpec(cs.shape, lambda i: (0, 0)),
                  pl.BlockSpec(f1.shape, lambda i: (0, 0, 0)),
                  pl.BlockSpec(f2.shape, lambda i: (0, 0)),
                  pl.BlockSpec((None, GROUP_W, GROUP_W), lambda i: (layer, 0, 0)),
                  pl.BlockSpec((None, 1, GROUP_W), lambda i: (layer, 0, 0)),
                  pl.BlockSpec((n, 256), lambda i: (i, 3))],
        out_specs=pl.BlockSpec((n, 256), lambda i: (i, 0)),
        out_shape=jax.ShapeDtypeStruct((bsz * n, 256), BF16),
        scratch_shapes=[pltpu.VMEM((GROUP_W // LANES, n1 * _pitch(n2), LANES), F32)] * 2
                       + [pltpu.VMEM((GROUP_W // LANES, n2 * _pitch(n1), LANES), F32)] * 3,
        compiler_params=_cparams("arbitrary"), name="fnet_lat",
    )(fu, cs, f1, f2, fn_w, fn_b.reshape(DEPTH, 1, GROUP_W), sg)


def _exit_kernel(x_ref, oa_ref, ob_ref, oc_ref, od_ref, w_ref, g_ref, mod_ref, o_ref, *, latent, tiles_per_b):
    row = (pl.program_id(0) // tiles_per_b) if latent else CTX_ROW
    gate = mod_ref[pl.ds(row, 1), :][:, 2 * D_MODEL:]
    mixed = jnp.concatenate([oa_ref[...], ob_ref[...], oc_ref[...], od_ref[...]], axis=1)
    y = _dot(mixed, w_ref[...])
    y = y * lax.rsqrt(jnp.mean(y * y, axis=-1, keepdims=True) + EPS) * g_ref[...]
    o_ref[...] = x_ref[...] + gate * y


def _layer_exit(x2d, outs, w_out_bf, g_post, mod_all, n, layer, latent):
    t = x2d.shape[0]
    tm = 512
    tiles_per_b = n // tm if latent else 1
    row = lambda i: (i, 0)
    return pl.pallas_call(
        functools.partial(_exit_kernel, latent=latent, tiles_per_b=tiles_per_b),
        grid=(t // tm,),
        in_specs=[pl.BlockSpec((tm, D_MODEL), row)] + [pl.BlockSpec((tm, 256), row)] * 4
                 + [pl.BlockSpec((None, D_MODEL, D_MODEL), lambda i: (layer, 0, 0)),
                    pl.BlockSpec((None, 1, D_MODEL), lambda i: (layer, 0, 0)),
                    pl.BlockSpec((None, COND_ROWS, 3 * D_MODEL), lambda i: (layer, 0, 0))],
        out_specs=pl.BlockSpec((tm, D_MODEL), row),
        out_shape=jax.ShapeDtypeStruct((t, D_MODEL), F32),
        compiler_params=_cparams("arbitrary"),
        name="exit_latent" if latent else "exit_ctx",
    )(x2d, *outs, w_out_bf, g_post.reshape(DEPTH, 1, D_MODEL), mod_all)


def kernel(x_prompt, x_sample, cache_attn_k, cache_attn_v, cache_diff_k, cache_diff_v, c, c_ctx, w_ada, b_ada, norm_pre, norm_post, w_in, w_out, attn_sink, diff_lambda, diff_subln, hy_conv_w, hy_conv_b, hy_filt_w1, hy_filt_b1, hy_filt_w2, hy_filt_b2, hy_filt_w3, hy_filt_freq, hy_skip, fn_w, fn_b):
    bp, lp, _ = x_prompt.shape
    bs, ls, _ = x_sample.shape
    past = cache_attn_k.shape[2]
    assert bs < CTX_ROW + 1 <= COND_ROWS

    cond = jnp.concatenate([c, c_ctx[None, :], jnp.zeros((COND_ROWS - bs - 1, D_MODEL), F32)], axis=0)
    mod_all = _modulation(cond, w_ada, b_ada)
    w_in_bf = w_in.astype(BF16)
    w_out_bf = w_out.astype(BF16)
    sink = attn_sink.reshape(DEPTH * ATT_HEADS)

    rope = _rope_tables(ls, ATT_HEAD_DIM) + _rope_tables(ls, DIF_QK_DIM)
    filt = (jnp.pad(hy_filt_w1, ((0, 0), (0, LANES - FILT_EMB), (0, 0))), hy_filt_b1, hy_filt_w2, hy_filt_b2,
            hy_filt_w3, hy_filt_freq)
    cos_c, sin_c, fwd_c, inv_c = _dense_conv_tables(lp)
    kr_c, ki_c = _hyena_filters_ctx(lp, filt, cos_c, sin_c)
    hy_tables = _hyena_fft_tables(ls)
    hf_l, hb_l = _hyena_filters_lat(ls, filt)
    kr_l, ki_l = _filter_fft(hf_l, hb_l, hy_tables[0], hy_tables[1], ls)
    cs_c, dn_c = _fnet_channel_table(lp), _fnet_dense_table(lp)
    cs_l = _fnet_channel_table(ls)
    f1_l, f2_l = _fnet_fft_tables(ls)

    cak = cache_attn_k.reshape(bs, DEPTH, past, 128).transpose(0, 1, 3, 2).astype(BF16)
    cav = cache_attn_v.reshape(bs, DEPTH, past, 128).astype(BF16)
    cdk = cache_diff_k.reshape(bs, DEPTH, past, 256).transpose(0, 1, 3, 2).astype(BF16)
    cdv = jnp.concatenate([cache_diff_v, jnp.ones_like(cache_diff_v)], axis=-1)
    cdv = cdv.reshape(bs, DEPTH, past, 2 * GROUP_W).astype(BF16)

    xp = x_prompt.reshape(bp * lp, D_MODEL)
    xs = x_sample.reshape(bs * ls, D_MODEL)
    st_ak, st_av, st_dk, st_dv = [], [], [], []
    for l in range(DEPTH):
        (aq, akt, av, dq, dkt, dv, hu, fu, sg, ak32, av32, dk32, dv32) = _layer_entry(
            xp, bp, lp, mod_all, norm_pre, w_in_bf, l, None)
        outs = (_attn_a_ctx(aq, akt, av, sg, sink, bp, lp, l),
                _diff_attention(dq, [(dkt, dv)], sg, diff_lambda, diff_subln, bp, lp, l, lp),
                _hyena_ctx(hu, sg, hy_conv_w, hy_conv_b, hy_skip, kr_c, ki_c, fwd_c, inv_c, bp, lp, l),
                _fnet_ctx(fu, sg, fn_w, fn_b, cs_c, dn_c, bp, lp, l))
        st_ak.append(ak32.reshape(bp, lp, ATT_KV_HEADS, ATT_HEAD_DIM))
        st_av.append(av32.reshape(bp, lp, ATT_KV_HEADS, ATT_HEAD_DIM))
        st_dk.append(dk32.reshape(bp, lp, 2, DIF_HEADS, DIF_QK_DIM))
        st_dv.append(dv32.reshape(bp, lp, DIF_HEADS, DIF_V_DIM))
        xp = _layer_exit(xp, outs, w_out_bf, norm_post, mod_all, lp, l, False)

        (aq, akt, av, dq, dkt, dv, hu, fu, sg) = _layer_entry(xs, bs, ls, mod_all, norm_pre, w_in_bf, l, rope)
        outs = (_attn_a_latent(aq, akt, av, cak[:, l], cav[:, l], sg, sink, bs, ls, l),
                _diff_attention(dq, [(dkt, dv), (cdk[:, l], cdv[:, l])], sg, diff_lambda, diff_subln,
                                bs, ls, l, 256),
                _hyena_lat(hu, sg, hy_conv_w, hy_conv_b, hy_skip, kr_l, ki_l, hy_tables, bs, ls, l),
                _fnet_lat(fu, sg, fn_w, fn_b, cs_l, f1_l, f2_l, bs, ls, l))
        xs = _layer_exit(xs, outs, w_out_bf, norm_post, mod_all, ls, l, True)

    return (xp.reshape(bp, lp, D_MODEL), xs.reshape(bs, ls, D_MODEL),
            jnp.stack(st_ak, axis=1), jnp.stack(st_av, axis=1), jnp.stack(st_dk, axis=1), jnp.stack(st_dv, axis=1))
```

```python
import functools
import math

import numpy as np
import jax
import jax.numpy as jnp
from jax import lax
from jax.experimental import pallas as pl
from jax.experimental.pallas import tpu as pltpu

F32 = jnp.float32
BF16 = jnp.bfloat16

D_MODEL = 1024
DEPTH = 2
GRID_W = 64
GROUP_W = 256
ATT_HEADS = 4
ATT_KV_HEADS = 2
ATT_HEAD_DIM = 64
WINDOW = 128
BLOCK = 128
DIF_HEADS = 4
DIF_V_DIM = 64
DIF_QK_DIM = 32
HY_CH = 256
FILT_BANDS = 16
FILT_EMB = 1 + 2 * FILT_BANDS
FILT_HIDDEN = 64
HY_MIN_DECAY = math.log(1e-2) / 1.5
HY_MAX_DECAY = math.log(1e-2) / 0.3
FN_GROUP_CH = 64
ROPE_BASE = 10000.0
EPS = 1e-6
NEG_INF = -1e30

C_AQ, C_AK, C_AV, C_AG = 0, 256, 384, 512
C_DQ, C_DK, C_DV, C_DG = 768, 1024, 1280, 1536
C_HU, C_HG, C_FU, C_FG, D_IN = 1792, 2560, 2816, 3072, 3328
LOG2_E = math.log2(math.e)

LANES = 128
COND_ROWS = 8
CTX_ROW = 4
VMEM_LIMIT = 56 * 1024 * 1024
PITCH_PAD = 8

HY_N2 = 64
FN_N2 = 64


def _cparams(*sem):
    return pltpu.CompilerParams(dimension_semantics=sem, vmem_limit_bytes=VMEM_LIMIT)


def _dot(a, b):
    return jnp.dot(a, b, preferred_element_type=F32)


def _bdot(a, b):
    return jnp.dot(a.astype(BF16), b.astype(BF16), preferred_element_type=F32)


def _dot3(a, b):
    ah = a.astype(BF16)
    al = (a - ah.astype(F32)).astype(BF16)
    bh = b.astype(BF16)
    bl = (b - bh.astype(F32)).astype(BF16)
    return _dot(ah, bh) + _dot(ah, bl) + _dot(al, bh)


def _bf16_table(a):
    return jnp.asarray(a, F32).astype(BF16)


def _rope_tables(n, head_dim):
    pos = np.arange(n)
    row = (pos // GRID_W).astype(np.float64)
    col = (pos % GRID_W).astype(np.float64)
    n_freq = head_dim // 4
    inv = ROPE_BASE ** (-np.arange(n_freq, dtype=np.float64) / n_freq)
    ang = np.concatenate([row[:, None] * inv, col[:, None] * inv], axis=-1)
    reps = LANES // head_dim
    cos = np.tile(np.concatenate([np.cos(ang), np.cos(ang)], axis=-1), (1, reps))
    sin = np.tile(np.concatenate([-np.sin(ang), np.sin(ang)], axis=-1), (1, reps))
    return jnp.asarray(cos, F32), jnp.asarray(sin, F32)


def _filter_feats(n):
    t = np.linspace(0.0, 1.0, n)[:, None]
    w = (2.0 * math.pi / n) * np.arange(n)[:, None]
    f = np.linspace(1e-4, FILT_BANDS - 1, FILT_BANDS)[None, :]
    feats = np.concatenate([t, np.cos(f * w), -np.sin(f * w)], axis=-1)
    feats = np.pad(feats, ((0, 0), (0, LANES - FILT_EMB)))
    feats = np.concatenate([feats[:n // 2], feats[n // 2:]], axis=1)
    deltas = np.abs(np.linspace(HY_MIN_DECAY, HY_MAX_DECAY, HY_CH))
    decay = np.exp(-t * deltas[None, :])
    return jnp.asarray(feats, F32), jnp.asarray(decay, F32)


def _dense_conv_tables(n):
    big_l = 2 * n
    k = np.arange(big_l)[:, None]
    t = np.arange(n)[None, :]
    th = 2.0 * math.pi * k * t / big_l
    fwd = np.concatenate([np.cos(th), -np.sin(th)], axis=0)
    inv = np.concatenate([np.cos(th).T, -np.sin(th).T], axis=1) / big_l
    return (jnp.asarray(np.cos(th), F32), jnp.asarray(np.sin(th), F32), _bf16_table(fwd), _bf16_table(inv))


def _hyena_fft_tables(n):
    big_l = 2 * n
    n2 = HY_N2
    n1 = big_l // n2
    n1h = n1 // 2
    k1 = np.arange(n1)
    j1 = np.arange(n1h)
    j2 = np.arange(n2)
    k2 = np.arange(n2)
    th = 2.0 * math.pi * k1[None, :, None] * (j1[None, None, :] * n2 + j2[:, None, None]) / big_l
    m1 = np.concatenate([np.cos(th), -np.sin(th)], axis=1)
    th2 = 2.0 * math.pi * k2[:, None] * j2[None, :] / n2
    c2, s2 = np.cos(th2), np.sin(th2)
    w2 = np.block([[c2, s2], [-s2, c2]])
    tha = 2.0 * math.pi * j2[None, :, None] * (k1[:, None, None] + n1 * k2[None, None, :]) / big_l
    ca, sa = np.cos(tha), np.sin(tha)
    ma = np.concatenate([np.concatenate([ca, -sa], axis=2), np.concatenate([sa, ca], axis=2)], axis=1)
    thb = 2.0 * math.pi * j1[:, None] * k1[None, :] / n1
    mb = np.concatenate([np.cos(thb), -np.sin(thb)], axis=1) / big_l
    return _bf16_table(m1), _bf16_table(w2), _bf16_table(ma), _bf16_table(mb)


def _fnet_channel_table(n):
    m = np.arange(FN_GROUP_CH)
    th = 2.0 * math.pi * m[:, None] * m[None, :] / FN_GROUP_CH
    eye = np.eye(GROUP_W // FN_GROUP_CH)
    sc = 1.0 / math.sqrt(n * FN_GROUP_CH)
    cs = np.concatenate([np.kron(eye, np.cos(th)), np.kron(eye, np.sin(th))], axis=1) * sc
    return _bf16_table(cs)


def _fnet_dense_table(n):
    k = np.arange(n)
    th = 2.0 * math.pi * k[:, None] * k[None, :] / n
    return _bf16_table(np.concatenate([np.cos(th), -np.sin(th)], axis=1))


def _fnet_fft_tables(n):
    n2 = FN_N2
    n1 = n // n2
    k1 = np.arange(n1)
    j1 = np.arange(n1)
    j2 = np.arange(n2)
    k2 = np.arange(n2)
    th = 2.0 * math.pi * k1[None, :, None] * (j1[None, None, :] * n2 + j2[:, None, None]) / n
    c, s = np.cos(th), np.sin(th)
    f1 = np.concatenate([np.concatenate([c, -s], axis=2), np.concatenate([s, c], axis=2)], axis=1)
    th2 = 2.0 * math.pi * k2[:, None] * j2[None, :] / n2
    f2 = np.concatenate([np.cos(th2), -np.sin(th2)], axis=1)
    return _bf16_table(f1), _bf16_table(f2)


def _mod_kernel(cond_ref, w_ref, b_ref, o_ref):
    c = cond_ref[...]
    s = c * jax.nn.sigmoid(c)
    o_ref[...] = _dot3(s, w_ref[...]) + b_ref[...]


def _modulation(cond, w_ada, b_ada):
    tn = 1024
    return pl.pallas_call(
        _mod_kernel,
        grid=(DEPTH, 3 * D_MODEL // tn),
        in_specs=[pl.BlockSpec((COND_ROWS, D_MODEL), lambda l, j: (0, 0)),
                  pl.BlockSpec((None, D_MODEL, tn), lambda l, j: (l, 0, j)),
                  pl.BlockSpec((None, 1, tn), lambda l, j: (l, 0, j))],
        out_specs=pl.BlockSpec((None, COND_ROWS, tn), lambda l, j: (l, 0, j)),
        out_shape=jax.ShapeDtypeStruct((DEPTH, COND_ROWS, 3 * D_MODEL), F32),
        compiler_params=_cparams("arbitrary", "arbitrary"),
        name="modulation",
    )(cond, w_ada, b_ada.reshape(DEPTH, 1, 3 * D_MODEL))


def _rope(x, cos, sin, half):
    lane = lax.broadcasted_iota(jnp.int32, x.shape, 1)
    first = (lane % (2 * half)) < half
    partner = jnp.where(first, pltpu.roll(x, LANES - half, 1), pltpu.roll(x, half, 1))
    return x * cos + partner * sin


def _entry_kernel(*refs, latent, tiles_per_b):
    if latent:
        (x_ref, mod_ref, g_ref, w_ref, ca_ref, sa_ref, cd_ref, sd_ref,
         aq_ref, akt_ref, av_ref, dq_ref, dkt_ref, dv_ref, hu_ref, fu_ref, sg_ref) = refs
        row = pl.program_id(0) // tiles_per_b
    else:
        (x_ref, mod_ref, g_ref, w_ref, _, _, _, _,
         aq_ref, akt_ref, av_ref, dq_ref, dkt_ref, dv_ref, hu_ref, fu_ref, sg_ref,
         ak32_ref, av32_ref, dk32_ref, dv32_ref) = refs
        row = CTX_ROW
    m = mod_ref[pl.ds(row, 1), :]
    shift, scale = m[:, :D_MODEL], m[:, D_MODEL:2 * D_MODEL]
    x = x_ref[...]
    h = x * lax.rsqrt(jnp.mean(x * x, axis=-1, keepdims=True) + EPS) * g_ref[...]
    hb = (h * (1.0 + scale) + shift).astype(BF16)

    def proj(c0, c1):
        return _dot(hb, w_ref[:, c0:c1])

    def roped(p, cos_ref, sin_ref, half):
        if not latent:
            return p
        cos, sin = cos_ref[...], sin_ref[...]
        chunks = [_rope(p[:, j:j + LANES], cos, sin, half) for j in range(0, p.shape[1], LANES)]
        return chunks[0] if len(chunks) == 1 else jnp.concatenate(chunks, axis=1)

    ca = sa = cd = sd = None
    if latent:
        ca, sa, cd, sd = ca_ref, sa_ref, cd_ref, sd_ref

    aq_ref[...] = (roped(proj(C_AQ, C_AK), ca, sa, ATT_HEAD_DIM // 2) * (ATT_HEAD_DIM ** -0.5 * LOG2_E)).astype(BF16)
    ak = roped(proj(C_AK, C_AV), ca, sa, ATT_HEAD_DIM // 2)
    akt_ref[...] = ak.T.astype(BF16)
    av = proj(C_AV, C_AG)
    av_ref[...] = av.astype(BF16)
    dq_ref[...] = (roped(proj(C_DQ, C_DK), cd, sd, DIF_QK_DIM // 2) * (DIF_QK_DIM ** -0.5 * LOG2_E)).astype(BF16)
    dk = roped(proj(C_DK, C_DV), cd, sd, DIF_QK_DIM // 2)
    dkt_ref[...] = dk.T.astype(BF16)
    dv = proj(C_DV, C_DG)
    value_lane = lax.broadcasted_iota(jnp.int32, (dv.shape[0], LANES), 1) < DIF_V_DIM
    for h in range(DIF_HEADS):
        pair = dv[:, (h // 2) * LANES:(h // 2 + 1) * LANES]
        if h % 2:
            pair = pltpu.roll(pair, DIF_V_DIM, 1)
        dv_ref[:, h * LANES:(h + 1) * LANES] = jnp.where(value_lane, pair, 1.0).astype(BF16)
    if not latent:
        ak32_ref[...] = ak
        av32_ref[...] = av
        dk32_ref[...] = dk
        dv32_ref[...] = dv
    hu_ref[...] = proj(C_HU, C_HG).astype(BF16)
    fu_ref[...] = proj(C_FU, C_FG).astype(BF16)
    for j, c0 in enumerate((C_AG, C_DG, C_HG, C_FG)):
        g = proj(c0, c0 + GROUP_W)
        sg_ref[:, j * GROUP_W:(j + 1) * GROUP_W] = (g * jax.nn.sigmoid(g)).astype(BF16)


def _layer_entry(x2d, bsz, n, mod_all, g_pre, w_in_bf, layer, rope, caches=None):
    latent = rope is not None
    t = bsz * n
    tm = 512 if latent else n
    tiles_per_b = n // tm
    grid = (t // tm,)
    row = lambda i: (i, 0)
    kt_map = lambda i: (i // tiles_per_b, 0, i % tiles_per_b)
    in_specs = [pl.BlockSpec((tm, D_MODEL), row),
                pl.BlockSpec((None, COND_ROWS, 3 * D_MODEL), lambda i: (layer, 0, 0)),
                pl.BlockSpec((None, 1, D_MODEL), lambda i: (layer, 0, 0)),
                pl.BlockSpec((None, D_MODEL, D_IN), lambda i: (layer, 0, 0))]
    args = [x2d, mod_all, g_pre.reshape(DEPTH, 1, D_MODEL), w_in_bf]
    if latent:
        pos = lambda i: (i % tiles_per_b, 0)
        in_specs += [pl.BlockSpec((tm, LANES), pos)] * 4
        args += list(rope)
    bf = lambda c: jax.ShapeDtypeStruct((t, c), BF16)
    out_shape = [bf(256), jax.ShapeDtypeStruct((bsz, 128, n), BF16), bf(128),
                 bf(256), jax.ShapeDtypeStruct((bsz, 256, n), BF16), bf(512),
                 bf(768), bf(256), bf(1024)]
    rs = lambda c: pl.BlockSpec((tm, c), row)
    out_specs = [rs(256), pl.BlockSpec((None, 128, tm), kt_map), rs(128),
                 rs(256), pl.BlockSpec((None, 256, tm), kt_map), rs(512),
                 rs(768), rs(256), rs(1024)]
    aliases = {}
    if not latent:
        assert tm == n
        for a in caches:
            aliases[len(args)] = len(out_shape)
            in_specs.append(pl.BlockSpec(memory_space=pl.ANY))
            args.append(a)
            out_shape.append(jax.ShapeDtypeStruct(a.shape, a.dtype))
            out_specs.append(pl.BlockSpec((None, None, n, a.shape[-1]), lambda i: (i, layer, 0, 0)))
    res = pl.pallas_call(
        functools.partial(_entry_kernel, latent=latent, tiles_per_b=tiles_per_b),
        grid=grid, in_specs=in_specs, out_specs=out_specs, out_shape=out_shape,
        input_output_aliases=aliases,
        compiler_params=_cparams("arbitrary"),
        name="entry_latent" if latent else "entry_ctx",
    )(*args)
    return res if latent else (*res[:9], list(res[9:]))


def _sink_columns(sink_ref, layer, tq):
    grp = ATT_HEADS // ATT_KV_HEADS
    head_row = lax.broadcasted_iota(jnp.int32, (grp * tq, 1), 0) // tq
    cols = []
    for g in range(ATT_KV_HEADS):
        col = jnp.zeros((grp * tq, 1), F32)
        for j in range(grp):
            col = jnp.where(head_row == j, sink_ref[layer * ATT_HEADS + g * grp + j] * LOG2_E, col)
        cols.append(col)
    return cols


def _stack_heads(q_ref, rows, g):
    grp = ATT_HEADS // ATT_KV_HEADS
    return jnp.concatenate([q_ref[rows, (g * grp + t) * ATT_HEAD_DIM:(g * grp + t + 1) * ATT_HEAD_DIM]
                            for t in range(grp)], axis=0)


def _attn_a_kernel(sink_ref, q_ref, k_ref, v_ref, sg_ref, o_ref, *, layer):
    tq = q_ref.shape[0]
    grp = ATT_HEADS // ATT_KV_HEADS
    sink_cols = _sink_columns(sink_ref, layer, tq)
    outs = []
    for g in range(ATT_KV_HEADS):
        d0 = g * ATT_HEAD_DIM
        s = _dot(_stack_heads(q_ref, slice(None), g), k_ref[d0:d0 + ATT_HEAD_DIM, :])
        m = jnp.maximum(jnp.max(s, axis=-1, keepdims=True), sink_cols[g])
        p = jnp.exp2(s - m)
        l = jnp.sum(p, axis=-1, keepdims=True) + jnp.exp2(sink_cols[g] - m)
        o = _dot(p.astype(BF16), v_ref[:, d0:d0 + ATT_HEAD_DIM]) / l
        outs += [o[j * tq:(j + 1) * tq] for j in range(grp)]
    o_full = jnp.concatenate(outs, axis=1)
    o_ref[...] = (o_full * sg_ref[...].astype(F32)).astype(BF16)


def _attn_a_win_kernel(sink_ref, q_ref, kp_ref, kc_ref, kn_ref, vp_ref, vc_ref, vn_ref, ck_ref, cv_ref, sg_ref,
                       o_ref, *, layer, nsteps, qb):
    i = pl.program_id(1)
    grp = ATT_HEADS // ATT_KV_HEADS
    rows = grp * BLOCK
    kw = jnp.concatenate([kp_ref[...], kc_ref[...], kn_ref[...]], axis=1)
    vw = jnp.concatenate([vp_ref[...], vc_ref[...], vn_ref[...]], axis=0)
    r = lax.broadcasted_iota(jnp.int32, (rows, 3 * BLOCK), 0) % BLOCK
    c = lax.broadcasted_iota(jnp.int32, (rows, 3 * BLOCK), 1)
    band = (c >= r) & (c <= r + 2 * WINDOW)
    masks = {0: band & ((c >= BLOCK) | (i > 0)), qb - 1: band & ((c < 2 * BLOCK) | (i < nsteps - 1))}
    sink_cols = _sink_columns(sink_ref, layer, BLOCK)
    ck = ck_ref[...]
    cv = cv_ref[...]
    items = [(j, g) for j in range(qb) for g in range(ATT_KV_HEADS)]
    st_a, st_b, outs = {}, {}, {}
    for step in range(len(items) + 2):
        a, b, cc = step, step - 1, step - 2
        if a < len(items):
            j, g = items[a]
            d0 = g * ATT_HEAD_DIM
            q2 = _stack_heads(q_ref, slice(j * BLOCK, (j + 1) * BLOCK), g)
            s_w = _dot(q2, kw[d0:d0 + ATT_HEAD_DIM, j * BLOCK:(j + 3) * BLOCK])
            s_w = jnp.where(masks.get(j, band), s_w, NEG_INF)
            s_c = _dot(q2, ck[d0:d0 + ATT_HEAD_DIM, :])
            m = jnp.maximum(jnp.maximum(jnp.max(s_w, axis=-1, keepdims=True),
                                        jnp.max(s_c, axis=-1, keepdims=True)), sink_cols[g])
            st_a[a] = (s_w, s_c, m)
        if 0 <= b < len(items):
            s_w, s_c, m = st_a.pop(b)
            p_w = jnp.exp2(s_w - m)
            p_c = jnp.exp2(s_c - m)
            l = (jnp.sum(p_w, axis=-1, keepdims=True) + jnp.sum(p_c, axis=-1, keepdims=True)
                 + jnp.exp2(sink_cols[items[b][1]] - m))
            st_b[b] = (p_w.astype(BF16), p_c.astype(BF16), l)
        if 0 <= cc < len(items):
            j, g = items[cc]
            d0 = g * ATT_HEAD_DIM
            p_w, p_c, l = st_b.pop(cc)
            o = (_dot(p_w, vw[j * BLOCK:(j + 3) * BLOCK, d0:d0 + ATT_HEAD_DIM])
                 + _dot(p_c, cv[:, d0:d0 + ATT_HEAD_DIM])) / l
            outs.setdefault(j, []).extend([o[t * BLOCK:(t + 1) * BLOCK] for t in range(grp)])
            if g == ATT_KV_HEADS - 1:
                rs = slice(j * BLOCK, (j + 1) * BLOCK)
                o_full = jnp.concatenate(outs.pop(j), axis=1)
                o_ref[rs, :] = (o_full * sg_ref[rs, :].astype(F32)).astype(BF16)


def _attn_a_latent(aq, akt, av, ck_t, cv, sg, sink, bsz, n, layer, qb=4):
    nb = n // BLOCK
    nsteps = nb // qb
    qmap = lambda b, i: (b * nsteps + i, 0)
    prev = lambda i: jnp.maximum(qb * i - 1, 0)
    nxt = lambda i: jnp.minimum(qb * i + qb, nb - 1)
    kedge = lambda f: pl.BlockSpec((None, 128, BLOCK), lambda b, i: (b, 0, f(i)))
    vedge = lambda f: pl.BlockSpec((BLOCK, 128), lambda b, i: (b * nb + f(i), 0))
    return pl.pallas_call(
        functools.partial(_attn_a_win_kernel, layer=layer, nsteps=nsteps, qb=qb),
        grid=(bsz, nsteps),
        in_specs=[pl.BlockSpec(memory_space=pltpu.SMEM),
                  pl.BlockSpec((qb * BLOCK, 256), qmap),
                  kedge(prev), pl.BlockSpec((None, 128, qb * BLOCK), lambda b, i: (b, 0, i)), kedge(nxt),
                  vedge(prev), pl.BlockSpec((qb * BLOCK, 128), qmap), vedge(nxt),
                  pl.BlockSpec((None, 128, ck_t.shape[2]), lambda b, i: (b, 0, 0)),
                  pl.BlockSpec((None, cv.shape[1], 128), lambda b, i: (b, 0, 0)),
                  pl.BlockSpec((qb * BLOCK, 256), qmap)],
        out_specs=pl.BlockSpec((qb * BLOCK, 256), qmap),
        out_shape=jax.ShapeDtypeStruct((bsz * n, 256), BF16),
        compiler_params=_cparams("arbitrary", "arbitrary"),
        name="attn_a_latent",
    )(sink, aq, akt, akt, akt, av, av, av, ck_t, cv, sg)


def _diff_kernel(lam_ref, subln_ref, q_ref, *refs, lam_init, npieces, tq_sub, kc):
    kts = refs[0:2 * npieces:2]
    vs = refs[1:2 * npieces:2]
    sg_ref, o_ref, s_scr, p_scr = refs[2 * npieces:2 * npieces + 4]
    lp = lam_ref[...]
    lam = (jnp.exp(jnp.sum(lp[0:1] * lp[1:2], axis=-1, keepdims=True))
           - jnp.exp(jnp.sum(lp[2:3] * lp[3:4], axis=-1, keepdims=True)) + lam_init)
    subln = subln_ref[...]
    tq = q_ref.shape[0]
    chunks, off = [], 0
    for kt, v in zip(kts, vs):
        for c0 in range(0, kt.shape[1], kc):
            w = min(kc, kt.shape[1] - c0)
            chunks.append((kt, v, c0, w, off))
            off += w
    items = [(r0, h, mp) for r0 in range(0, tq, tq_sub) for h in range(DIF_HEADS) for mp in range(2)]
    n_items = len(items)
    mx, acc, om, outs = {}, {}, {}, {}

    def fold(s):
        return functools.reduce(jnp.maximum, [s[:, j:j + LANES] for j in range(0, s.shape[1], LANES)])

    nslot = s_scr.shape[0]
    lag = nslot - 1
    for step in range(n_items + 2 * lag):
        a = step if step < n_items else None
        b = step - lag if 0 <= step - lag < n_items else None
        c = step - 2 * lag if 0 <= step - 2 * lag < n_items else None
        if a is not None:
            r0, h, mp = items[a]
            f0 = (mp * DIF_HEADS + h) * DIF_QK_DIM
            q_a = q_ref[r0:r0 + tq_sub, f0:f0 + DIF_QK_DIM]
            mx[a] = jnp.full((tq_sub, LANES), NEG_INF, F32)
        if b is not None:
            m_b = jnp.max(mx.pop(b), axis=-1, keepdims=True)
        if c is not None:
            acc[c] = jnp.zeros((tq_sub, 2 * DIF_V_DIM), F32)
            hc = items[c][1]
        for kt, v, c0, w, o0 in chunks:
            if a is not None:
                s = _dot(q_a, kt[f0:f0 + DIF_QK_DIM, c0:c0 + w])
                s_scr[a % nslot, :, o0:o0 + w] = s
                mx[a] = jnp.maximum(mx[a], fold(s))
            if b is not None:
                p_scr[b % nslot, :, o0:o0 + w] = jnp.exp2(s_scr[b % nslot, :, o0:o0 + w] - m_b).astype(BF16)
            if c is not None:
                acc[c] = acc[c] + _dot(p_scr[c % nslot, :, o0:o0 + w],
                                       v[c0:c0 + w, 2 * DIF_V_DIM * hc:2 * DIF_V_DIM * (hc + 1)])
        if c is not None:
            r0, h, mp = items[c]
            o = acc.pop(c)
            om[(r0, h, mp)] = o[:, :DIF_V_DIM] / o[:, DIF_V_DIM:DIF_V_DIM + 1]
            if mp == 1:
                av = om.pop((r0, h, 0)) - lam * om.pop((r0, h, 1))
                y = av * lax.rsqrt(jnp.mean(av * av, axis=-1, keepdims=True) + EPS) * subln * (1.0 - lam_init)
                outs.setdefault(r0, []).append(y)
                if h == DIF_HEADS - 1:
                    o_full = jnp.concatenate(outs.pop(r0), axis=1)
                    o_ref[r0:r0 + tq_sub, :] = (o_full * sg_ref[r0:r0 + tq_sub, :].astype(F32)).astype(BF16)


def _diff_attention(dq, pieces, sg, diff_lambda, diff_subln, bsz, n, layer, tq, kc=512):
    lam_init = 0.8 - 0.6 * math.exp(-0.3 * layer)
    nt = n // tq
    nk_all = sum(kt.shape[2] for kt, _ in pieces)
    nslot = 2 if nk_all > kc else 2 * DIF_HEADS + 1
    in_specs = [pl.BlockSpec((None, 4, DIF_QK_DIM), lambda b, i: (layer, 0, 0)),
                pl.BlockSpec((None, 1, DIF_V_DIM), lambda b, i: (layer, 0, 0)),
                pl.BlockSpec((tq, 256), lambda b, i: (b * nt + i, 0))]
    args = [diff_lambda, diff_subln.reshape(DEPTH, 1, DIF_V_DIM), dq]
    for kt, v in pieces:
        nk = kt.shape[2]
        in_specs.append(pl.BlockSpec((None, 256, nk), lambda b, i: (b, 0, 0)))
        if v.ndim == 3:
            in_specs.append(pl.BlockSpec((None, nk, 512), lambda b, i: (b, 0, 0)))
        else:
            in_specs.append(pl.BlockSpec((nk, 512), lambda b, i: (b, 0)))
        args += [kt, v]
    in_specs.append(pl.BlockSpec((tq, 256), lambda b, i: (b * nt + i, 1)))
    args.append(sg)
    return pl.pallas_call(
        functools.partial(_diff_kernel, lam_init=lam_init, npieces=len(pieces), tq_sub=tq, kc=kc),
        grid=(bsz, nt), in_specs=in_specs,
        out_specs=pl.BlockSpec((tq, 256), lambda b, i: (b * nt + i, 0)),
        out_shape=jax.ShapeDtypeStruct((bsz * n, 256), BF16),
        scratch_shapes=[pltpu.VMEM((nslot, tq, nk_all), F32), pltpu.VMEM((nslot, tq, nk_all), BF16)],
        compiler_params=_cparams("arbitrary", "arbitrary"),
        name="diff_attn_%d" % len(pieces),
    )(*args)


def _filter_core(feats_ref, w1_ref, b1_ref, w2_ref, b2_ref, w3_ref, fr_ref, decay_ref):
    fr = fr_ref[...]
    h = jnp.sin(fr * (_dot3(feats_ref[...], w1_ref[...]) + b1_ref[...]))
    h = jnp.sin(fr * (_dot3(h, w2_ref[...]) + b2_ref[...]))
    h = _dot3(h, w3_ref[...])
    h = jnp.concatenate([h[:, :2 * HY_CH], h[:, 2 * HY_CH:]], axis=0)
    decay = decay_ref[...]
    hf = h[:, :HY_CH] * decay
    hb = h[:, HY_CH:] * decay
    tot = (jnp.sum(jnp.abs(hf), axis=0, keepdims=True) + jnp.sum(jnp.abs(hb), axis=0, keepdims=True) + EPS)
    hf = hf / tot
    hb = hb / tot
    rowi = lax.broadcasted_iota(jnp.int32, hb.shape, 0)
    return hf, jnp.where(rowi == 0, 0.0, hb)


def _filter_ctx_kernel(feats_ref, w1_ref, b1_ref, w2_ref, b2_ref, w3_ref, fr_ref, decay_ref,
                       cos_ref, sin_ref, kr_ref, ki_ref):
    hf, hb0 = _filter_core(feats_ref, w1_ref, b1_ref, w2_ref, b2_ref, w3_ref, fr_ref, decay_ref)
    kr_ref[...] = _dot3(cos_ref[...], hf + hb0)
    ki_ref[...] = _dot3(sin_ref[...], hb0 - hf)


def _filter_lat_kernel(feats_ref, w1_ref, b1_ref, w2_ref, b2_ref, w3_ref, fr_ref, decay_ref, hf_ref, hb_ref):
    hf, hb0 = _filter_core(feats_ref, w1_ref, b1_ref, w2_ref, b2_ref, w3_ref, fr_ref, decay_ref)
    hf_ref[...] = hf
    hb_ref[...] = hb0


def _filter_specs(n, w1p, b1, w2, b2, w3, freq, feats, decay):
    const = lambda shape: pl.BlockSpec(shape, lambda l: (0,) * len(shape))
    lay = lambda a, b: pl.BlockSpec((None, a, b), lambda l: (l, 0, 0))
    hid = 2 * FILT_HIDDEN
    in_specs = [const((n // 2, 2 * LANES)), lay(2 * LANES, hid), lay(1, hid), lay(hid, hid), lay(1, hid),
                lay(hid, 4 * HY_CH), lay(1, hid), const((n, HY_CH))]
    row2 = lambda a: jnp.tile(a.reshape(DEPTH, 1, FILT_HIDDEN), (1, 1, 2))
    args = [feats, _block_diag2(w1p), row2(b1), _block_diag2(w2), row2(b2), _block_diag2(w3), row2(freq), decay]
    return in_specs, args


def _block_diag2(w):
    z = jnp.zeros_like(w)
    return jnp.concatenate([jnp.concatenate([w, z], axis=2), jnp.concatenate([z, w], axis=2)], axis=1)


def _hyena_filters_ctx(n, filt, cos_t, sin_t):
    feats, decay = _filter_feats(n)
    in_specs, args = _filter_specs(n, *filt, feats, decay)
    big_l = 2 * n
    in_specs += [pl.BlockSpec((big_l, n), lambda l: (0, 0))] * 2
    out = pl.BlockSpec((None, big_l, HY_CH), lambda l: (l, 0, 0))
    return pl.pallas_call(
        _filter_ctx_kernel, grid=(DEPTH,), in_specs=in_specs, out_specs=[out, out],
        out_shape=[jax.ShapeDtypeStruct((DEPTH, big_l, HY_CH), F32)] * 2,
        compiler_params=_cparams("arbitrary"), name="hyena_filter_ctx",
    )(*args, cos_t, sin_t)


def _hyena_filters_lat(n, filt):
    feats, decay = _filter_feats(n)
    in_specs, args = _filter_specs(n, *filt, feats, decay)
    out = pl.BlockSpec((None, n, HY_CH), lambda l: (l, 0, 0))
    return pl.pallas_call(
        _filter_lat_kernel, grid=(DEPTH,), in_specs=in_specs, out_specs=[out, out],
        out_shape=[jax.ShapeDtypeStruct((DEPTH, n, HY_CH), F32)] * 2,
        compiler_params=_cparams("arbitrary"), name="hyena_filter_lat",
    )(*args)


def _short_conv(u_ref, cw_ref, cb_ref):
    u = u_ref[...].astype(F32)
    n = u.shape[0]
    rowi = lax.broadcasted_iota(jnp.int32, u.shape, 0)
    up = jnp.where(rowi == 0, 0.0, pltpu.roll(u, 1, 0))
    un = jnp.where(rowi == n - 1, 0.0, pltpu.roll(u, n - 1, 0))
    w = cw_ref[...]
    return up * w[0:1] + u * w[1:2] + un * w[2:3] + cb_ref[...]


def _hyena_ctx_kernel(hu_ref, cw_ref, cb_ref, kr_ref, ki_ref, fwd_ref, inv_ref, skip_ref, sg_ref, o_ref):
    uc = _short_conv(hu_ref, cw_ref, cb_ref)
    x0, x1, v = uc[:, :HY_CH], uc[:, HY_CH:2 * HY_CH], uc[:, 2 * HY_CH:]
    z = x1 * v
    zf = _dot(fwd_ref[...], z.astype(BF16))
    big_l = zf.shape[0] // 2
    zr, zi = zf[:big_l], zf[big_l:]
    kr, ki = kr_ref[...], ki_ref[...]
    p = jnp.concatenate([zr * kr - zi * ki, zr * ki + zi * kr], axis=0).astype(BF16)
    y = _dot(inv_ref[...], p)
    out = x0 * (y + z * skip_ref[...])
    o_ref[...] = (out * sg_ref[...].astype(F32)).astype(BF16)


def _pitch(rows):
    return rows + PITCH_PAD


def _to_pitched(dst, src, blk):
    for i in range(src.shape[0] // blk):
        dst[i * _pitch(blk):i * _pitch(blk) + blk, :] = src[i * blk:(i + 1) * blk, :]


def _fft_stage1(src, m1_ref, s_re, s_im, n1, n2):
    for j2 in range(n2):
        x = src[pl.ds(j2, n1 // 2, stride=_pitch(n2)), :]
        a = _dot(m1_ref[j2], x.astype(BF16))
        s_re[j2 * _pitch(n1):j2 * _pitch(n1) + n1, :] = a[:n1]
        s_im[j2 * _pitch(n1):j2 * _pitch(n1) + n1, :] = a[n1:]


def _fft_stage2(w2_ref, s_re, s_im, k1, n1, n2):
    re = s_re[pl.ds(k1, n2, stride=_pitch(n1)), :]
    im = s_im[pl.ds(k1, n2, stride=_pitch(n1)), :]
    x = _dot(w2_ref[...], jnp.concatenate([re, im], axis=0).astype(BF16))
    return x[:n2], x[n2:]


def _filter_fft_kernel(hf_ref, hb_ref, m1_ref, w2_ref, kr_ref, ki_ref, x_scr, s_re, s_im, *, n1, n2):
    _to_pitched(x_scr, hf_ref, n2)
    _fft_stage1(x_scr, m1_ref, s_re, s_im, n1, n2)
    for k1 in range(n1):
        xr, xi = _fft_stage2(w2_ref, s_re, s_im, k1, n1, n2)
        kr_ref[k1 * n2:(k1 + 1) * n2, :] = xr
        ki_ref[k1 * n2:(k1 + 1) * n2, :] = xi
    _to_pitched(x_scr, hb_ref, n2)
    _fft_stage1(x_scr, m1_ref, s_re, s_im, n1, n2)
    for k1 in range(n1):
        xr, xi = _fft_stage2(w2_ref, s_re, s_im, k1, n1, n2)
        kr_ref[k1 * n2:(k1 + 1) * n2, :] += xr
        ki_ref[k1 * n2:(k1 + 1) * n2, :] -= xi


def _filter_fft(hf, hb0, m1, w2, n):
    big_l = 2 * n
    n2 = HY_N2
    n1 = big_l // n2
    inp = pl.BlockSpec((None, n, LANES), lambda l, c: (l, 0, c))
    out = pl.BlockSpec((None, big_l, LANES), lambda l, c: (l, 0, c))
    return pl.pallas_call(
        functools.partial(_filter_fft_kernel, n1=n1, n2=n2),
        grid=(DEPTH, HY_CH // LANES),
        in_specs=[inp, inp,
                  pl.BlockSpec(m1.shape, lambda l, c: (0, 0, 0)),
                  pl.BlockSpec(w2.shape, lambda l, c: (0, 0))],
        out_specs=[out, out],
        out_shape=[jax.ShapeDtypeStruct((DEPTH, big_l, HY_CH), F32)] * 2,
        scratch_shapes=[pltpu.VMEM((n1 // 2 * _pitch(n2), LANES), F32)]
                       + [pltpu.VMEM((n2 * _pitch(n1), LANES), F32)] * 2,
        compiler_params=_cparams("arbitrary", "arbitrary"), name="hyena_filter_fft",
    )(hf, hb0, m1, w2)


def _hyena_lat_kernel(x0_ref, x1_ref, v_ref, cw0_ref, cw1_ref, cw2_ref, cb0_ref, cb1_ref, cb2_ref,
                      kr_ref, ki_ref, m1_ref, w2_ref, ma_ref, mb_ref, skip_ref, sg_ref, o_ref,
                      z_scr, y_scr, s_re, s_im, *, n1, n2):
    n1h = n1 // 2
    z = _short_conv(x1_ref, cw1_ref, cb1_ref) * _short_conv(v_ref, cw2_ref, cb2_ref)
    _to_pitched(z_scr, z, n2)
    _fft_stage1(z_scr, m1_ref, s_re, s_im, n1, n2)
    for k1 in range(n1):
        xr, xi = _fft_stage2(w2_ref, s_re, s_im, k1, n1, n2)
        kr = kr_ref[k1 * n2:(k1 + 1) * n2, :]
        ki = ki_ref[k1 * n2:(k1 + 1) * n2, :]
        p = jnp.concatenate([xr * kr - xi * ki, xr * ki + xi * kr], axis=0).astype(BF16)
        b = _dot(ma_ref[k1], p)
        s_re[pl.ds(k1, n2, stride=_pitch(n1)), :] = b[:n2]
        s_im[pl.ds(k1, n2, stride=_pitch(n1)), :] = b[n2:]
    for j2 in range(n2):
        r0 = j2 * _pitch(n1)
        blk = jnp.concatenate([s_re[r0:r0 + n1, :], s_im[r0:r0 + n1, :]], axis=0)
        y_scr[pl.ds(j2, n1h, stride=_pitch(n2)), :] = _dot(mb_ref[...], blk.astype(BF16))
    x0 = _short_conv(x0_ref, cw0_ref, cb0_ref)
    skip = skip_ref[...]
    for j1 in range(n1h):
        rows = slice(j1 * n2, (j1 + 1) * n2)
        prow = slice(j1 * _pitch(n2), j1 * _pitch(n2) + n2)
        out = x0[rows] * (y_scr[prow, :] + z_scr[prow, :] * skip)
        o_ref[rows, :] = (out * sg_ref[rows, :].astype(F32)).astype(BF16)


def _hyena_lat(hu, sg, conv_w, conv_b, skip, kr, ki, tables, bsz, n, layer):
    m1, w2, ma, mb = tables
    big_l = 2 * n
    n2 = HY_N2
    n1 = big_l // n2
    nch = HY_CH // LANES
    ucol = lambda s: pl.BlockSpec((n, LANES), lambda c, b: (b, s * nch + c))
    wcol = lambda s: pl.BlockSpec((None, 3, LANES), lambda c, b: (layer, 0, s * nch + c))
    bcol = lambda s: pl.BlockSpec((None, 1, LANES), lambda c, b: (layer, 0, s * nch + c))
    kspec = pl.BlockSpec((None, big_l, LANES), lambda c, b: (layer, 0, c))
    const = lambda a: pl.BlockSpec(a.shape, lambda c, b: (0,) * a.ndim)
    cb3 = conv_b.reshape(DEPTH, 1, 768)
    return pl.pallas_call(
        functools.partial(_hyena_lat_kernel, n1=n1, n2=n2),
        grid=(nch, bsz),
        in_specs=[ucol(0), ucol(1), ucol(2), wcol(0), wcol(1), wcol(2), bcol(0), bcol(1), bcol(2),
                  kspec, kspec, const(m1), const(w2), const(ma), const(mb),
                  pl.BlockSpec((None, 1, LANES), lambda c, b: (layer, 0, c)),
                  pl.BlockSpec((n, LANES), lambda c, b: (b, 2 * nch + c))],
        out_specs=pl.BlockSpec((n, LANES), lambda c, b: (b, c)),
        out_shape=jax.ShapeDtypeStruct((bsz * n, 256), BF16),
        scratch_shapes=[pltpu.VMEM((n1 // 2 * _pitch(n2), LANES), F32)] * 2
                       + [pltpu.VMEM((n2 * _pitch(n1), LANES), F32)] * 2,
        compiler_params=_cparams("arbitrary", "arbitrary"), name="hyena_lat",
    )(hu, hu, hu, conv_w, conv_w, conv_w, cb3, cb3, cb3, kr, ki, m1, w2, ma, mb,
      skip.reshape(DEPTH, 1, HY_CH), sg)


def _fnet_ctx_kernel(u_ref, cs_ref, dn_ref, fw_ref, fb_ref, sg_ref, o_ref):
    ab = _dot(u_ref[...], cs_ref[...])
    stack = jnp.concatenate([ab[:, :GROUP_W], ab[:, GROUP_W:]], axis=0).astype(BF16)
    f = _dot(dn_ref[...], stack)
    out = _bdot(f, fw_ref[...]) + fb_ref[...]
    o_ref[...] = (out * sg_ref[...].astype(F32)).astype(BF16)


def _fnet_lat_kernel(u_ref, cs_ref, f1_ref, f2_ref, fw_ref, fb_ref, sg_ref, o_ref,
                     a_scr, b_scr, s_re, s_im, o_scr, *, n1, n2):
    halves = GROUP_W // LANES

    def put(scr, rows, val):
        for hh in range(halves):
            scr[hh, rows, :] = val[:, hh * LANES:(hh + 1) * LANES]

    def get(scr, rows):
        return jnp.concatenate([scr[hh, rows, :] for hh in range(halves)], axis=1)

    def block(i, blk):
        return slice(i * _pitch(blk), i * _pitch(blk) + blk)

    ab = _dot(u_ref[...], cs_ref[...])
    for j1 in range(n1):
        put(a_scr, block(j1, n2), ab[j1 * n2:(j1 + 1) * n2, :GROUP_W])
        put(b_scr, block(j1, n2), ab[j1 * n2:(j1 + 1) * n2, GROUP_W:])
    for j2 in range(n2):
        rows = pl.ds(j2, n1, stride=_pitch(n2))
        g = jnp.concatenate([get(a_scr, rows), get(b_scr, rows)], axis=0)
        t = _dot(f1_ref[j2], g.astype(BF16))
        put(s_re, block(j2, n1), t[:n1])
        put(s_im, block(j2, n1), t[n1:])
    for k1 in range(n1):
        rows = pl.ds(k1, n2, stride=_pitch(n1))
        g = jnp.concatenate([get(s_re, rows), get(s_im, rows)], axis=0)
        put(o_scr, rows, _dot(f2_ref[...], g.astype(BF16)))
    fw = fw_ref[...].astype(BF16)
    fb = fb_ref[...]
    for k2 in range(n2):
        rows = slice(k2 * n1, (k2 + 1) * n1)
        out = _dot(get(o_scr, block(k2, n1)).astype(BF16), fw) + fb
        o_ref[rows, :] = (out * sg_ref[rows, :].astype(F32)).astype(BF16)


def _fnet_lat(fu, sg, fn_w, fn_b, cs, f1, f2, bsz, n, layer):
    n2 = FN_N2
    n1 = n // n2
    return pl.pallas_call(
        functools.partial(_fnet_lat_kernel, n1=n1, n2=n2), grid=(bsz,),
        in_specs=[pl.BlockSpec((n, 256), lambda i: (i, 0)),
                  pl.BlockSpec(cs.shape, lambda i: (0, 0)),
                  pl.BlockSpec(f1.shape, lambda i: (0, 0, 0)),
                  pl.BlockSpec(f2.shape, lambda i: (0, 0)),
                  pl.BlockSpec((None, GROUP_W, GROUP_W), lambda i: (layer, 0, 0)),
                  pl.BlockSpec((None, 1, GROUP_W), lambda i: (layer, 0, 0)),
                  pl.BlockSpec((n, 256), lambda i: (i, 3))],
        out_specs=pl.BlockSpec((n, 256), lambda i: (i, 0)),
        out_shape=jax.ShapeDtypeStruct((bsz * n, 256), BF16),
        scratch_shapes=[pltpu.VMEM((GROUP_W // LANES, n1 * _pitch(n2), LANES), F32)] * 2
                       + [pltpu.VMEM((GROUP_W // LANES, n2 * _pitch(n1), LANES), F32)] * 3,
        compiler_params=_cparams("arbitrary"), name="fnet_lat",
    )(fu, cs, f1, f2, fn_w, fn_b.reshape(DEPTH, 1, GROUP_W), sg)


def _ctx_mixers_kernel(sink_ref, aq_ref, akt_ref, av_ref, lam_ref, subln_ref, dq_ref, dkt_ref, dv_ref,
                       hu_ref, cw_ref, cb_ref, kr_ref, ki_ref, fwd_ref, inv_ref, skip_ref,
                       fu_ref, cs_ref, dn_ref, fw_ref, fb_ref, sg_ref, o_ref, s_scr, p_scr, *, layer, lam_init, kc):
    sg = [sg_ref.at[:, j * GROUP_W:(j + 1) * GROUP_W] for j in range(4)]
    out = [o_ref.at[:, j * GROUP_W:(j + 1) * GROUP_W] for j in range(4)]
    _attn_a_kernel(sink_ref, aq_ref, akt_ref, av_ref, sg[0], out[0], layer=layer)
    _diff_kernel(lam_ref, subln_ref, dq_ref, dkt_ref, dv_ref, sg[1], out[1], s_scr, p_scr,
                 lam_init=lam_init, npieces=1, tq_sub=dq_ref.shape[0], kc=kc)
    _hyena_ctx_kernel(hu_ref, cw_ref, cb_ref, kr_ref, ki_ref, fwd_ref, inv_ref, skip_ref, sg[2], out[2])
    _fnet_ctx_kernel(fu_ref, cs_ref, dn_ref, fw_ref, fb_ref, sg[3], out[3])


def _ctx_mixers(entry_out, sink, diff_lambda, diff_subln, conv_w, conv_b, skip, kr, ki, fwd_t, inv_t,
                fn_w, fn_b, cs, dn, bsz, n, layer, kc=512):
    aq, akt, av, dq, dkt, dv, hu, fu, sg = entry_out
    lam_init = 0.8 - 0.6 * math.exp(-0.3 * layer)
    big_l = 2 * n
    rows = lambda c: pl.BlockSpec((n, c), lambda b: (b, 0))
    feat = lambda c: pl.BlockSpec((None, c, n), lambda b: (b, 0, 0))
    lay = lambda a, c: pl.BlockSpec((None, a, c), lambda b: (layer, 0, 0))
    const = lambda a: pl.BlockSpec(a.shape, lambda b: (0,) * a.ndim)
    nslot = DIF_HEADS * 2 + 1
    return pl.pallas_call(
        functools.partial(_ctx_mixers_kernel, layer=layer, lam_init=lam_init, kc=kc),
        grid=(bsz,),
        in_specs=[pl.BlockSpec(memory_space=pltpu.SMEM), rows(256), feat(128), rows(128),
                  lay(4, DIF_QK_DIM), lay(1, DIF_V_DIM), rows(256), feat(256), rows(512),
                  rows(768), lay(3, 768), lay(1, 768), lay(big_l, HY_CH), lay(big_l, HY_CH),
                  const(fwd_t), const(inv_t), lay(1, HY_CH),
                  rows(256), const(cs), const(dn), lay(GROUP_W, GROUP_W), lay(1, GROUP_W),
                  rows(1024)],
        out_specs=rows(1024),
        out_shape=jax.ShapeDtypeStruct((bsz * n, 4 * GROUP_W), BF16),
        scratch_shapes=[pltpu.VMEM((nslot, n, n), F32), pltpu.VMEM((nslot, n, n), BF16)],
        compiler_params=_cparams("arbitrary"), name="ctx_mixers",
    )(sink, aq, akt, av, diff_lambda, diff_subln.reshape(DEPTH, 1, DIF_V_DIM), dq, dkt, dv,
      hu, conv_w, conv_b.reshape(DEPTH, 1, 768), kr, ki, fwd_t, inv_t, skip.reshape(DEPTH, 1, HY_CH),
      fu, cs, dn, fn_w, fn_b.reshape(DEPTH, 1, GROUP_W), sg)


def _exit_kernel(x_ref, *refs, latent, tiles_per_b):
    mix_refs, (w_ref, g_ref, mod_ref, o_ref) = refs[:-4], refs[-4:]
    row = (pl.program_id(0) // tiles_per_b) if latent else CTX_ROW
    gate = mod_ref[pl.ds(row, 1), :][:, 2 * D_MODEL:]
    mixed = mix_refs[0][...] if len(mix_refs) == 1 else jnp.concatenate([r[...] for r in mix_refs], axis=1)
    y = _dot(mixed, w_ref[...])
    y = y * lax.rsqrt(jnp.mean(y * y, axis=-1, keepdims=True) + EPS) * g_ref[...]
    o_ref[...] = x_ref[...] + gate * y


def _layer_exit(x2d, outs, w_out_bf, g_post, mod_all, n, layer, latent):
    t = x2d.shape[0]
    tm = 512
    tiles_per_b = n // tm if latent else 1
    row = lambda i: (i, 0)
    return pl.pallas_call(
        functools.partial(_exit_kernel, latent=latent, tiles_per_b=tiles_per_b),
        grid=(t // tm,),
        in_specs=[pl.BlockSpec((tm, D_MODEL), row)] + [pl.BlockSpec((tm, o.shape[1]), row) for o in outs]
                 + [pl.BlockSpec((None, D_MODEL, D_MODEL), lambda i: (layer, 0, 0)),
                    pl.BlockSpec((None, 1, D_MODEL), lambda i: (layer, 0, 0)),
                    pl.BlockSpec((None, COND_ROWS, 3 * D_MODEL), lambda i: (layer, 0, 0))],
        out_specs=pl.BlockSpec((tm, D_MODEL), row),
        out_shape=jax.ShapeDtypeStruct((t, D_MODEL), F32),
        compiler_params=_cparams("arbitrary"),
        name="exit_latent" if latent else "exit_ctx",
    )(x2d, *outs, w_out_bf, g_post.reshape(DEPTH, 1, D_MODEL), mod_all)


def kernel(x_prompt, x_sample, cache_attn_k, cache_attn_v, cache_diff_k, cache_diff_v, c, c_ctx, w_ada, b_ada, norm_pre, norm_post, w_in, w_out, attn_sink, diff_lambda, diff_subln, hy_conv_w, hy_conv_b, hy_filt_w1, hy_filt_b1, hy_filt_w2, hy_filt_b2, hy_filt_w3, hy_filt_freq, hy_skip, fn_w, fn_b):
    bp, lp, _ = x_prompt.shape
    bs, ls, _ = x_sample.shape
    past = cache_attn_k.shape[2]
    assert bs < CTX_ROW + 1 <= COND_ROWS

    cond = jnp.concatenate([c, c_ctx[None, :], jnp.zeros((COND_ROWS - bs - 1, D_MODEL), F32)], axis=0)
    mod_all = _modulation(cond, w_ada, b_ada)
    w_in_bf = w_in.astype(BF16)
    w_out_bf = w_out.astype(BF16)
    sink = attn_sink.reshape(DEPTH * ATT_HEADS)

    rope = _rope_tables(ls, ATT_HEAD_DIM) + _rope_tables(ls, DIF_QK_DIM)
    filt = (jnp.pad(hy_filt_w1, ((0, 0), (0, LANES - FILT_EMB), (0, 0))), hy_filt_b1, hy_filt_w2, hy_filt_b2,
            hy_filt_w3, hy_filt_freq)
    cos_c, sin_c, fwd_c, inv_c = _dense_conv_tables(lp)
    kr_c, ki_c = _hyena_filters_ctx(lp, filt, cos_c, sin_c)
    hy_tables = _hyena_fft_tables(ls)
    hf_l, hb_l = _hyena_filters_lat(ls, filt)
    kr_l, ki_l = _filter_fft(hf_l, hb_l, hy_tables[0], hy_tables[1], ls)
    cs_c, dn_c = _fnet_channel_table(lp), _fnet_dense_table(lp)
    cs_l = _fnet_channel_table(ls)
    f1_l, f2_l = _fnet_fft_tables(ls)

    cak = cache_attn_k.reshape(bs, DEPTH, past, 128).transpose(0, 1, 3, 2).astype(BF16)
    cav = cache_attn_v.reshape(bs, DEPTH, past, 128).astype(BF16)
    cdk = cache_diff_k.reshape(bs, DEPTH, past, 256).transpose(0, 1, 3, 2).astype(BF16)
    cdv = jnp.concatenate([cache_diff_v, jnp.ones_like(cache_diff_v)], axis=-1)
    cdv = cdv.reshape(bs, DEPTH, past, 2 * GROUP_W).astype(BF16)

    xp = x_prompt.reshape(bp * lp, D_MODEL)
    xs = x_sample.reshape(bs * ls, D_MODEL)
    new_caches = [jnp.zeros((bp, DEPTH, lp, c), F32) for c in (128, 128, 256, 256)]
    for l in range(DEPTH):
        *entry_out, new_caches = _layer_entry(xp, bp, lp, mod_all, norm_pre, w_in_bf, l, None, new_caches)
        mixed = _ctx_mixers(entry_out, sink, diff_lambda, diff_subln, hy_conv_w, hy_conv_b, hy_skip, kr_c, ki_c,
                            fwd_c, inv_c, fn_w, fn_b, cs_c, dn_c, bp, lp, l)
        xp = _layer_exit(xp, (mixed,), w_out_bf, norm_post, mod_all, lp, l, False)

        (aq, akt, av, dq, dkt, dv, hu, fu, sg) = _layer_entry(xs, bs, ls, mod_all, norm_pre, w_in_bf, l, rope)
        outs = (_attn_a_latent(aq, akt, av, cak[:, l], cav[:, l], sg, sink, bs, ls, l),
                _diff_attention(dq, [(dkt, dv), (cdk[:, l], cdv[:, l])], sg, diff_lambda, diff_subln,
                                bs, ls, l, 256),
                _hyena_lat(hu, sg, hy_conv_w, hy_conv_b, hy_skip, kr_l, ki_l, hy_tables, bs, ls, l),
                _fnet_lat(fu, sg, fn_w, fn_b, cs_l, f1_l, f2_l, bs, ls, l))
        xs = _layer_exit(xs, outs, w_out_bf, norm_post, mod_all, ls, l, True)

    nak, nav, ndk, ndv = new_caches
    return (xp.reshape(bp, lp, D_MODEL), xs.reshape(bs, ls, D_MODEL),
            nak.reshape(bp, DEPTH, lp, ATT_KV_HEADS, ATT_HEAD_DIM), nav.reshape(bp, DEPTH, lp, ATT_KV_HEADS, ATT_HEAD_DIM),
            ndk.reshape(bp, DEPTH, lp, 2, DIF_HEADS, DIF_QK_DIM), ndv.reshape(bp, DEPTH, lp, DIF_HEADS, DIF_V_DIM))
```

```python
import functools
import math

import numpy as np
import jax
import jax.numpy as jnp
from jax import lax
from jax.experimental import pallas as pl
from jax.experimental.pallas import tpu as pltpu

F32 = jnp.float32
BF16 = jnp.bfloat16

D_MODEL = 1024
DEPTH = 2
GRID_W = 64
GROUP_W = 256
ATT_HEADS = 4
ATT_KV_HEADS = 2
ATT_HEAD_DIM = 64
WINDOW = 128
BLOCK = 128
DIF_HEADS = 4
DIF_V_DIM = 64
DIF_QK_DIM = 32
HY_CH = 256
FILT_BANDS = 16
FILT_EMB = 1 + 2 * FILT_BANDS
FILT_HIDDEN = 64
HY_MIN_DECAY = math.log(1e-2) / 1.5
HY_MAX_DECAY = math.log(1e-2) / 0.3
FN_GROUP_CH = 64
ROPE_BASE = 10000.0
EPS = 1e-6
NEG_INF = -1e30

C_AQ, C_AK, C_AV, C_AG = 0, 256, 384, 512
C_DQ, C_DK, C_DV, C_DG = 768, 1024, 1280, 1536
C_HU, C_HG, C_FU, C_FG, D_IN = 1792, 2560, 2816, 3072, 3328
LOG2_E = math.log2(math.e)

LANES = 128
COND_ROWS = 8
CTX_ROW = 4
VMEM_LIMIT = 56 * 1024 * 1024
PITCH_PAD = 8

HY_N2 = 64
FN_N2 = 64


def _cparams(*sem):
    return pltpu.CompilerParams(dimension_semantics=sem, vmem_limit_bytes=VMEM_LIMIT)


def _dot(a, b):
    return jnp.dot(a, b, preferred_element_type=F32)


def _bdot(a, b):
    return jnp.dot(a.astype(BF16), b.astype(BF16), preferred_element_type=F32)


def _dot3(a, b):
    ah = a.astype(BF16)
    al = (a - ah.astype(F32)).astype(BF16)
    bh = b.astype(BF16)
    bl = (b - bh.astype(F32)).astype(BF16)
    return _dot(ah, bh) + _dot(ah, bl) + _dot(al, bh)


def _bf16_table(a):
    return jnp.asarray(a, F32).astype(BF16)


def _rope_tables(n, head_dim):
    pos = np.arange(n)
    row = (pos // GRID_W).astype(np.float64)
    col = (pos % GRID_W).astype(np.float64)
    n_freq = head_dim // 4
    inv = ROPE_BASE ** (-np.arange(n_freq, dtype=np.float64) / n_freq)
    ang = np.concatenate([row[:, None] * inv, col[:, None] * inv], axis=-1)
    reps = LANES // head_dim
    cos = np.tile(np.concatenate([np.cos(ang), np.cos(ang)], axis=-1), (1, reps))
    sin = np.tile(np.concatenate([-np.sin(ang), np.sin(ang)], axis=-1), (1, reps))
    return jnp.asarray(cos, F32), jnp.asarray(sin, F32)


def _filter_feats(n):
    t = np.linspace(0.0, 1.0, n)[:, None]
    w = (2.0 * math.pi / n) * np.arange(n)[:, None]
    f = np.linspace(1e-4, FILT_BANDS - 1, FILT_BANDS)[None, :]
    feats = np.concatenate([t, np.cos(f * w), -np.sin(f * w)], axis=-1)
    feats = np.pad(feats, ((0, 0), (0, LANES - FILT_EMB)))
    feats = np.concatenate([feats[:n // 2], feats[n // 2:]], axis=1)
    deltas = np.abs(np.linspace(HY_MIN_DECAY, HY_MAX_DECAY, HY_CH))
    decay = np.exp(-t * deltas[None, :])
    return jnp.asarray(feats, F32), jnp.asarray(decay, F32)


def _dense_conv_tables(n):
    big_l = 2 * n
    k = np.arange(big_l)[:, None]
    t = np.arange(n)[None, :]
    th = 2.0 * math.pi * k * t / big_l
    fwd = np.concatenate([np.cos(th), -np.sin(th)], axis=0)
    inv = np.concatenate([np.cos(th).T, -np.sin(th).T], axis=1) / big_l
    return (jnp.asarray(np.cos(th), F32), jnp.asarray(np.sin(th), F32), _bf16_table(fwd), _bf16_table(inv))


def _hyena_fft_tables(n):
    big_l = 2 * n
    n2 = HY_N2
    n1 = big_l // n2
    n1h = n1 // 2
    k1 = np.arange(n1)
    j1 = np.arange(n1h)
    j2 = np.arange(n2)
    k2 = np.arange(n2)
    th = 2.0 * math.pi * k1[None, :, None] * (j1[None, None, :] * n2 + j2[:, None, None]) / big_l
    m1 = np.concatenate([np.cos(th), -np.sin(th)], axis=1)
    th2 = 2.0 * math.pi * k2[:, None] * j2[None, :] / n2
    c2, s2 = np.cos(th2), np.sin(th2)
    w2 = np.block([[c2, s2], [-s2, c2]])
    tha = 2.0 * math.pi * j2[None, :, None] * (k1[:, None, None] + n1 * k2[None, None, :]) / big_l
    ca, sa = np.cos(tha), np.sin(tha)
    ma = np.concatenate([np.concatenate([ca, -sa], axis=2), np.concatenate([sa, ca], axis=2)], axis=1)
    thb = 2.0 * math.pi * j1[:, None] * k1[None, :] / n1
    mb = np.concatenate([np.cos(thb), -np.sin(thb)], axis=1) / big_l
    m1 = np.concatenate([m1[0::2], m1[1::2]], axis=2)
    ma = np.concatenate([ma[0::2], ma[1::2]], axis=2)
    return _bf16_table(m1), _bf16_table(w2), _bf16_table(ma), _bf16_table(mb)


def _fnet_channel_table(n):
    m = np.arange(FN_GROUP_CH)
    th = 2.0 * math.pi * m[:, None] * m[None, :] / FN_GROUP_CH
    eye = np.eye(GROUP_W // FN_GROUP_CH)
    sc = 1.0 / math.sqrt(n * FN_GROUP_CH)
    cs = np.concatenate([np.kron(eye, np.cos(th)), np.kron(eye, np.sin(th))], axis=1) * sc
    return _bf16_table(cs)


def _fnet_dense_table(n):
    k = np.arange(n)
    th = 2.0 * math.pi * k[:, None] * k[None, :] / n
    return _bf16_table(np.concatenate([np.cos(th), -np.sin(th)], axis=1))


def _fnet_fft_tables(n):
    n2 = FN_N2
    n1 = n // n2
    k1 = np.arange(n1)
    j1 = np.arange(n1)
    j2 = np.arange(n2)
    k2 = np.arange(n2)
    th = 2.0 * math.pi * k1[None, :, None] * (j1[None, None, :] * n2 + j2[:, None, None]) / n
    c, s = np.cos(th), np.sin(th)
    f1 = np.concatenate([np.concatenate([c, -s], axis=2), np.concatenate([s, c], axis=2)], axis=1)
    th2 = 2.0 * math.pi * k2[:, None] * j2[None, :] / n2
    f2 = np.concatenate([np.cos(th2), -np.sin(th2)], axis=1)
    return _bf16_table(f1), _bf16_table(f2)


def _mod_kernel(cond_ref, w_ref, b_ref, o_ref):
    c = cond_ref[...]
    s = c * jax.nn.sigmoid(c)
    o_ref[...] = _dot3(s, w_ref[...]) + b_ref[...]


def _modulation(cond, w_ada, b_ada):
    tn = 1024
    return pl.pallas_call(
        _mod_kernel,
        grid=(DEPTH, 3 * D_MODEL // tn),
        in_specs=[pl.BlockSpec((COND_ROWS, D_MODEL), lambda l, j: (0, 0)),
                  pl.BlockSpec((None, D_MODEL, tn), lambda l, j: (l, 0, j)),
                  pl.BlockSpec((None, 1, tn), lambda l, j: (l, 0, j))],
        out_specs=pl.BlockSpec((None, COND_ROWS, tn), lambda l, j: (l, 0, j)),
        out_shape=jax.ShapeDtypeStruct((DEPTH, COND_ROWS, 3 * D_MODEL), F32),
        compiler_params=_cparams("arbitrary", "arbitrary"),
        name="modulation",
    )(cond, w_ada, b_ada.reshape(DEPTH, 1, 3 * D_MODEL))


def _rope(x, cos, sin, half):
    lane = lax.broadcasted_iota(jnp.int32, x.shape, 1)
    first = (lane % (2 * half)) < half
    partner = jnp.where(first, pltpu.roll(x, LANES - half, 1), pltpu.roll(x, half, 1))
    return x * cos + partner * sin


def _entry_kernel(*refs, latent, tiles_per_b):
    if latent:
        (x_ref, mod_ref, g_ref, w_ref, ca_ref, sa_ref, cd_ref, sd_ref,
         aq_ref, akt_ref, av_ref, dq_ref, dkt_ref, dv_ref, hu_ref, fu_ref, sg_ref) = refs
        row = pl.program_id(0) // tiles_per_b
    else:
        (x_ref, mod_ref, g_ref, w_ref,
         aq_ref, akt_ref, av_ref, dq_ref, dkt_ref, dv_ref, hu_ref, fu_ref, sg_ref,
         ak32_ref, av32_ref, dk32_ref, dv32_ref) = refs
        row = CTX_ROW
    m = mod_ref[pl.ds(row, 1), :]
    shift, scale = m[:, :D_MODEL], m[:, D_MODEL:2 * D_MODEL]
    x = x_ref[...]
    h = x * lax.rsqrt(jnp.mean(x * x, axis=-1, keepdims=True) + EPS) * g_ref[...]
    hb = (h * (1.0 + scale) + shift).astype(BF16)

    def proj(c0, c1):
        return _dot(hb, w_ref[:, c0:c1])

    def roped(p, cos_ref, sin_ref, half):
        if not latent:
            return p
        cos, sin = cos_ref[...], sin_ref[...]
        chunks = [_rope(p[:, j:j + LANES], cos, sin, half) for j in range(0, p.shape[1], LANES)]
        return chunks[0] if len(chunks) == 1 else jnp.concatenate(chunks, axis=1)

    ca = sa = cd = sd = None
    if latent:
        ca, sa, cd, sd = ca_ref, sa_ref, cd_ref, sd_ref

    aq_ref[...] = (roped(proj(C_AQ, C_AK), ca, sa, ATT_HEAD_DIM // 2) * (ATT_HEAD_DIM ** -0.5 * LOG2_E)).astype(BF16)
    ak = roped(proj(C_AK, C_AV), ca, sa, ATT_HEAD_DIM // 2)
    akt_ref[...] = ak.T.astype(BF16)
    av = proj(C_AV, C_AG)
    av_ref[...] = av.astype(BF16)
    dq_ref[...] = (roped(proj(C_DQ, C_DK), cd, sd, DIF_QK_DIM // 2) * (DIF_QK_DIM ** -0.5 * LOG2_E)).astype(BF16)
    dk = roped(proj(C_DK, C_DV), cd, sd, DIF_QK_DIM // 2)
    dkt_ref[...] = dk.T.astype(BF16)
    dv = proj(C_DV, C_DG)
    value_lane = lax.broadcasted_iota(jnp.int32, (dv.shape[0], LANES), 1) < DIF_V_DIM
    for h in range(DIF_HEADS):
        pair = dv[:, (h // 2) * LANES:(h // 2 + 1) * LANES]
        if h % 2:
            pair = pltpu.roll(pair, DIF_V_DIM, 1)
        dv_ref[:, h * LANES:(h + 1) * LANES] = jnp.where(value_lane, pair, 1.0).astype(BF16)
    if not latent:
        ak32_ref[...] = ak
        av32_ref[...] = av
        dk32_ref[...] = dk
        dv32_ref[...] = dv
    hu_ref[...] = proj(C_HU, C_HG).astype(BF16)
    fu_ref[...] = proj(C_FU, C_FG).astype(BF16)
    for j, c0 in enumerate((C_AG, C_DG, C_HG, C_FG)):
        g = proj(c0, c0 + GROUP_W)
        sg_ref[:, j * GROUP_W:(j + 1) * GROUP_W] = (g * jax.nn.sigmoid(g)).astype(BF16)


def _layer_entry(x2d, bsz, n, mod_all, g_pre, w_in_bf, layer, rope):
    t = bsz * n
    tm = 512
    tiles_per_b = n // tm
    row = lambda i: (i, 0)
    pos = lambda i: (i % tiles_per_b, 0)
    kt_map = lambda i: (i // tiles_per_b, 0, i % tiles_per_b)
    in_specs = [pl.BlockSpec((tm, D_MODEL), row),
                pl.BlockSpec((None, COND_ROWS, 3 * D_MODEL), lambda i: (layer, 0, 0)),
                pl.BlockSpec((None, 1, D_MODEL), lambda i: (layer, 0, 0)),
                pl.BlockSpec((None, D_MODEL, D_IN), lambda i: (layer, 0, 0))] + [pl.BlockSpec((tm, LANES), pos)] * 4
    bf = lambda c: jax.ShapeDtypeStruct((t, c), BF16)
    rs = lambda c: pl.BlockSpec((tm, c), row)
    out_shape = [bf(256), jax.ShapeDtypeStruct((bsz, 128, n), BF16), bf(128),
                 bf(256), jax.ShapeDtypeStruct((bsz, 256, n), BF16), bf(512),
                 bf(768), bf(256), bf(1024)]
    out_specs = [rs(256), pl.BlockSpec((None, 128, tm), kt_map), rs(128),
                 rs(256), pl.BlockSpec((None, 256, tm), kt_map), rs(512),
                 rs(768), rs(256), rs(1024)]
    return pl.pallas_call(
        functools.partial(_entry_kernel, latent=True, tiles_per_b=tiles_per_b),
        grid=(t // tm,), in_specs=in_specs, out_specs=out_specs, out_shape=out_shape,
        compiler_params=_cparams("arbitrary"), name="entry_latent",
    )(x2d, mod_all, g_pre.reshape(DEPTH, 1, D_MODEL), w_in_bf, *rope)


def _sink_columns(sink_ref, layer, tq):
    grp = ATT_HEADS // ATT_KV_HEADS
    head_row = lax.broadcasted_iota(jnp.int32, (grp * tq, 1), 0) // tq
    cols = []
    for g in range(ATT_KV_HEADS):
        col = jnp.zeros((grp * tq, 1), F32)
        for j in range(grp):
            col = jnp.where(head_row == j, sink_ref[layer * ATT_HEADS + g * grp + j] * LOG2_E, col)
        cols.append(col)
    return cols


def _stack_heads(q_ref, rows, g):
    grp = ATT_HEADS // ATT_KV_HEADS
    return jnp.concatenate([q_ref[rows, (g * grp + t) * ATT_HEAD_DIM:(g * grp + t + 1) * ATT_HEAD_DIM]
                            for t in range(grp)], axis=0)


def _attn_a_kernel(sink_ref, q_ref, k_ref, v_ref, sg_ref, o_ref, *, layer):
    tq = q_ref.shape[0]
    grp = ATT_HEADS // ATT_KV_HEADS
    sink_cols = _sink_columns(sink_ref, layer, tq)
    outs = []
    for g in range(ATT_KV_HEADS):
        d0 = g * ATT_HEAD_DIM
        s = _dot(_stack_heads(q_ref, slice(None), g), k_ref[d0:d0 + ATT_HEAD_DIM, :])
        m = jnp.maximum(jnp.max(s, axis=-1, keepdims=True), sink_cols[g])
        p = jnp.exp2(s - m)
        l = jnp.sum(p, axis=-1, keepdims=True) + jnp.exp2(sink_cols[g] - m)
        o = _dot(p.astype(BF16), v_ref[:, d0:d0 + ATT_HEAD_DIM]) / l
        outs += [o[j * tq:(j + 1) * tq] for j in range(grp)]
    o_full = jnp.concatenate(outs, axis=1)
    o_ref[...] = (o_full * sg_ref[...].astype(F32)).astype(BF16)


def _attn_a_win_kernel(sink_ref, q_ref, kp_ref, kc_ref, kn_ref, vp_ref, vc_ref, vn_ref, ck_ref, cv_ref, sg_ref,
                       o_ref, *, layer, nsteps, qb):
    i = pl.program_id(1)
    grp = ATT_HEADS // ATT_KV_HEADS
    rows = grp * BLOCK
    kw = jnp.concatenate([kp_ref[...], kc_ref[...], kn_ref[...]], axis=1)
    vw = jnp.concatenate([vp_ref[...], vc_ref[...], vn_ref[...]], axis=0)
    r = lax.broadcasted_iota(jnp.int32, (rows, 3 * BLOCK), 0) % BLOCK
    c = lax.broadcasted_iota(jnp.int32, (rows, 3 * BLOCK), 1)
    band = (c >= r) & (c <= r + 2 * WINDOW)
    masks = {0: band & ((c >= BLOCK) | (i > 0)), qb - 1: band & ((c < 2 * BLOCK) | (i < nsteps - 1))}
    sink_cols = _sink_columns(sink_ref, layer, BLOCK)
    ck = ck_ref[...]
    cv = cv_ref[...]
    items = [(j, g) for j in range(qb) for g in range(ATT_KV_HEADS)]
    st_a, st_b, outs = {}, {}, {}
    for step in range(len(items) + 2):
        a, b, cc = step, step - 1, step - 2
        if a < len(items):
            j, g = items[a]
            d0 = g * ATT_HEAD_DIM
            q2 = _stack_heads(q_ref, slice(j * BLOCK, (j + 1) * BLOCK), g)
            s_w = _dot(q2, kw[d0:d0 + ATT_HEAD_DIM, j * BLOCK:(j + 3) * BLOCK])
            s_w = jnp.where(masks.get(j, band), s_w, NEG_INF)
            s_c = _dot(q2, ck[d0:d0 + ATT_HEAD_DIM, :])
            m = jnp.maximum(jnp.maximum(jnp.max(s_w, axis=-1, keepdims=True),
                                        jnp.max(s_c, axis=-1, keepdims=True)), sink_cols[g])
            st_a[a] = (s_w, s_c, m)
        if 0 <= b < len(items):
            s_w, s_c, m = st_a.pop(b)
            p_w = jnp.exp2(s_w - m)
            p_c = jnp.exp2(s_c - m)
            l = (jnp.sum(p_w, axis=-1, keepdims=True) + jnp.sum(p_c, axis=-1, keepdims=True)
                 + jnp.exp2(sink_cols[items[b][1]] - m))
            st_b[b] = (p_w.astype(BF16), p_c.astype(BF16), l)
        if 0 <= cc < len(items):
            j, g = items[cc]
            d0 = g * ATT_HEAD_DIM
            p_w, p_c, l = st_b.pop(cc)
            o = (_dot(p_w, vw[j * BLOCK:(j + 3) * BLOCK, d0:d0 + ATT_HEAD_DIM])
                 + _dot(p_c, cv[:, d0:d0 + ATT_HEAD_DIM])) / l
            outs.setdefault(j, []).extend([o[t * BLOCK:(t + 1) * BLOCK] for t in range(grp)])
            if g == ATT_KV_HEADS - 1:
                rs = slice(j * BLOCK, (j + 1) * BLOCK)
                o_full = jnp.concatenate(outs.pop(j), axis=1)
                o_ref[rs, :] = (o_full * sg_ref[rs, :].astype(F32)).astype(BF16)


def _attn_a_latent(aq, akt, av, ck_t, cv, sg, sink, bsz, n, layer, qb=4):
    nb = n // BLOCK
    nsteps = nb // qb
    qmap = lambda b, i: (b * nsteps + i, 0)
    prev = lambda i: jnp.maximum(qb * i - 1, 0)
    nxt = lambda i: jnp.minimum(qb * i + qb, nb - 1)
    kedge = lambda f: pl.BlockSpec((None, 128, BLOCK), lambda b, i: (b, 0, f(i)))
    vedge = lambda f: pl.BlockSpec((BLOCK, 128), lambda b, i: (b * nb + f(i), 0))
    return pl.pallas_call(
        functools.partial(_attn_a_win_kernel, layer=layer, nsteps=nsteps, qb=qb),
        grid=(bsz, nsteps),
        in_specs=[pl.BlockSpec(memory_space=pltpu.SMEM),
                  pl.BlockSpec((qb * BLOCK, 256), qmap),
                  kedge(prev), pl.BlockSpec((None, 128, qb * BLOCK), lambda b, i: (b, 0, i)), kedge(nxt),
                  vedge(prev), pl.BlockSpec((qb * BLOCK, 128), qmap), vedge(nxt),
                  pl.BlockSpec((None, 128, ck_t.shape[2]), lambda b, i: (b, 0, 0)),
                  pl.BlockSpec((None, cv.shape[1], 128), lambda b, i: (b, 0, 0)),
                  pl.BlockSpec((qb * BLOCK, 256), qmap)],
        out_specs=pl.BlockSpec((qb * BLOCK, 256), qmap),
        out_shape=jax.ShapeDtypeStruct((bsz * n, 256), BF16),
        compiler_params=_cparams("arbitrary", "arbitrary"),
        name="attn_a_latent",
    )(sink, aq, akt, akt, akt, av, av, av, ck_t, cv, sg)


def _diff_kernel(lam_ref, subln_ref, q_ref, *refs, lam_init, npieces, tq_sub, kc):
    kts = refs[0:2 * npieces:2]
    vs = refs[1:2 * npieces:2]
    sg_ref, o_ref, s_scr, p_scr = refs[2 * npieces:2 * npieces + 4]
    lp = lam_ref[...]
    lam = (jnp.exp(jnp.sum(lp[0:1] * lp[1:2], axis=-1, keepdims=True))
           - jnp.exp(jnp.sum(lp[2:3] * lp[3:4], axis=-1, keepdims=True)) + lam_init)
    subln = subln_ref[...]
    tq = q_ref.shape[0]
    chunks, off = [], 0
    for kt, v in zip(kts, vs):
        for c0 in range(0, kt.shape[1], kc):
            w = min(kc, kt.shape[1] - c0)
            chunks.append((kt, v, c0, w, off))
            off += w
    items = [(r0, h, mp) for r0 in range(0, tq, tq_sub) for h in range(DIF_HEADS) for mp in range(2)]
    n_items = len(items)
    mx, acc, om, outs = {}, {}, {}, {}

    def fold(s):
        return functools.reduce(jnp.maximum, [s[:, j:j + LANES] for j in range(0, s.shape[1], LANES)])

    nslot = s_scr.shape[0]
    lag = nslot - 1
    for step in range(n_items + 2 * lag):
        a = step if step < n_items else None
        b = step - lag if 0 <= step - lag < n_items else None
        c = step - 2 * lag if 0 <= step - 2 * lag < n_items else None
        if a is not None:
            r0, h, mp = items[a]
            f0 = (mp * DIF_HEADS + h) * DIF_QK_DIM
            q_a = q_ref[r0:r0 + tq_sub, f0:f0 + DIF_QK_DIM]
            mx[a] = jnp.full((tq_sub, LANES), NEG_INF, F32)
        if b is not None:
            m_b = jnp.max(mx.pop(b), axis=-1, keepdims=True)
        if c is not None:
            acc[c] = jnp.zeros((tq_sub, 2 * DIF_V_DIM), F32)
            hc = items[c][1]
        for kt, v, c0, w, o0 in chunks:
            if a is not None:
                s = _dot(q_a, kt[f0:f0 + DIF_QK_DIM, c0:c0 + w])
                s_scr[a % nslot, :, o0:o0 + w] = s
                mx[a] = jnp.maximum(mx[a], fold(s))
            if b is not None:
                p_scr[b % nslot, :, o0:o0 + w] = jnp.exp2(s_scr[b % nslot, :, o0:o0 + w] - m_b).astype(BF16)
            if c is not None:
                acc[c] = acc[c] + _dot(p_scr[c % nslot, :, o0:o0 + w],
                                       v[c0:c0 + w, 2 * DIF_V_DIM * hc:2 * DIF_V_DIM * (hc + 1)])
        if c is not None:
            r0, h, mp = items[c]
            o = acc.pop(c)
            om[(r0, h, mp)] = o[:, :DIF_V_DIM] / o[:, DIF_V_DIM:DIF_V_DIM + 1]
            if mp == 1:
                av = om.pop((r0, h, 0)) - lam * om.pop((r0, h, 1))
                y = av * lax.rsqrt(jnp.mean(av * av, axis=-1, keepdims=True) + EPS) * subln * (1.0 - lam_init)
                outs.setdefault(r0, []).append(y)
                if h == DIF_HEADS - 1:
                    o_full = jnp.concatenate(outs.pop(r0), axis=1)
                    o_ref[r0:r0 + tq_sub, :] = (o_full * sg_ref[r0:r0 + tq_sub, :].astype(F32)).astype(BF16)


def _diff_attention(dq, pieces, sg, diff_lambda, diff_subln, bsz, n, layer, tq, kc=512):
    lam_init = 0.8 - 0.6 * math.exp(-0.3 * layer)
    nt = n // tq
    nk_all = sum(kt.shape[2] for kt, _ in pieces)
    nslot = 2 if nk_all > kc else 2 * DIF_HEADS + 1
    in_specs = [pl.BlockSpec((None, 4, DIF_QK_DIM), lambda b, i: (layer, 0, 0)),
                pl.BlockSpec((None, 1, DIF_V_DIM), lambda b, i: (layer, 0, 0)),
                pl.BlockSpec((tq, 256), lambda b, i: (b * nt + i, 0))]
    args = [diff_lambda, diff_subln.reshape(DEPTH, 1, DIF_V_DIM), dq]
    for kt, v in pieces:
        nk = kt.shape[2]
        in_specs.append(pl.BlockSpec((None, 256, nk), lambda b, i: (b, 0, 0)))
        if v.ndim == 3:
            in_specs.append(pl.BlockSpec((None, nk, 512), lambda b, i: (b, 0, 0)))
        else:
            in_specs.append(pl.BlockSpec((nk, 512), lambda b, i: (b, 0)))
        args += [kt, v]
    in_specs.append(pl.BlockSpec((tq, 256), lambda b, i: (b * nt + i, 1)))
    args.append(sg)
    return pl.pallas_call(
        functools.partial(_diff_kernel, lam_init=lam_init, npieces=len(pieces), tq_sub=tq, kc=kc),
        grid=(bsz, nt), in_specs=in_specs,
        out_specs=pl.BlockSpec((tq, 256), lambda b, i: (b * nt + i, 0)),
        out_shape=jax.ShapeDtypeStruct((bsz * n, 256), BF16),
        scratch_shapes=[pltpu.VMEM((nslot, tq, nk_all), F32), pltpu.VMEM((nslot, tq, nk_all), BF16)],
        compiler_params=_cparams("arbitrary", "arbitrary"),
        name="diff_attn_%d" % len(pieces),
    )(*args)


def _filter_core(feats_ref, w1_ref, b1_ref, w2_ref, b2_ref, w3_ref, fr_ref, decay_ref):
    fr = fr_ref[...]
    h = jnp.sin(fr * (_dot3(feats_ref[...], w1_ref[...]) + b1_ref[...]))
    h = jnp.sin(fr * (_dot3(h, w2_ref[...]) + b2_ref[...]))
    h = _dot3(h, w3_ref[...])
    h = jnp.concatenate([h[:, :2 * HY_CH], h[:, 2 * HY_CH:]], axis=0)
    decay = decay_ref[...]
    hf = h[:, :HY_CH] * decay
    hb = h[:, HY_CH:] * decay
    tot = (jnp.sum(jnp.abs(hf), axis=0, keepdims=True) + jnp.sum(jnp.abs(hb), axis=0, keepdims=True) + EPS)
    hf = hf / tot
    hb = hb / tot
    rowi = lax.broadcasted_iota(jnp.int32, hb.shape, 0)
    return hf, jnp.where(rowi == 0, 0.0, hb)


def _filter_ctx_kernel(feats_ref, w1_ref, b1_ref, w2_ref, b2_ref, w3_ref, fr_ref, decay_ref,
                       cos_ref, sin_ref, kr_ref, ki_ref):
    hf, hb0 = _filter_core(feats_ref, w1_ref, b1_ref, w2_ref, b2_ref, w3_ref, fr_ref, decay_ref)
    kr_ref[...] = _dot3(cos_ref[...], hf + hb0)
    ki_ref[...] = _dot3(sin_ref[...], hb0 - hf)


def _filter_lat_kernel(feats_ref, w1_ref, b1_ref, w2_ref, b2_ref, w3_ref, fr_ref, decay_ref, hf_ref, hb_ref):
    hf, hb0 = _filter_core(feats_ref, w1_ref, b1_ref, w2_ref, b2_ref, w3_ref, fr_ref, decay_ref)
    hf_ref[...] = hf
    hb_ref[...] = hb0


def _filter_specs(n, w1p, b1, w2, b2, w3, freq, feats, decay):
    const = lambda shape: pl.BlockSpec(shape, lambda l: (0,) * len(shape))
    lay = lambda a, b: pl.BlockSpec((None, a, b), lambda l: (l, 0, 0))
    hid = 2 * FILT_HIDDEN
    in_specs = [const((n // 2, 2 * LANES)), lay(2 * LANES, hid), lay(1, hid), lay(hid, hid), lay(1, hid),
                lay(hid, 4 * HY_CH), lay(1, hid), const((n, HY_CH))]
    row2 = lambda a: jnp.tile(a.reshape(DEPTH, 1, FILT_HIDDEN), (1, 1, 2))
    args = [feats, _block_diag2(w1p), row2(b1), _block_diag2(w2), row2(b2), _block_diag2(w3), row2(freq), decay]
    return in_specs, args


def _block_diag2(w):
    z = jnp.zeros_like(w)
    return jnp.concatenate([jnp.concatenate([w, z], axis=2), jnp.concatenate([z, w], axis=2)], axis=1)


def _hyena_filters_ctx(n, filt, cos_t, sin_t):
    feats, decay = _filter_feats(n)
    in_specs, args = _filter_specs(n, *filt, feats, decay)
    big_l = 2 * n
    in_specs += [pl.BlockSpec((big_l, n), lambda l: (0, 0))] * 2
    out = pl.BlockSpec((None, big_l, HY_CH), lambda l: (l, 0, 0))
    return pl.pallas_call(
        _filter_ctx_kernel, grid=(DEPTH,), in_specs=in_specs, out_specs=[out, out],
        out_shape=[jax.ShapeDtypeStruct((DEPTH, big_l, HY_CH), F32)] * 2,
        compiler_params=_cparams("arbitrary"), name="hyena_filter_ctx",
    )(*args, cos_t, sin_t)


def _hyena_filters_lat(n, filt):
    feats, decay = _filter_feats(n)
    in_specs, args = _filter_specs(n, *filt, feats, decay)
    out = pl.BlockSpec((None, n, HY_CH), lambda l: (l, 0, 0))
    return pl.pallas_call(
        _filter_lat_kernel, grid=(DEPTH,), in_specs=in_specs, out_specs=[out, out],
        out_shape=[jax.ShapeDtypeStruct((DEPTH, n, HY_CH), F32)] * 2,
        compiler_params=_cparams("arbitrary"), name="hyena_filter_lat",
    )(*args)


def _short_conv(u_ref, cw_ref, cb_ref):
    u = u_ref[...].astype(F32)
    n = u.shape[0]
    rowi = lax.broadcasted_iota(jnp.int32, u.shape, 0)
    up = jnp.where(rowi == 0, 0.0, pltpu.roll(u, 1, 0))
    un = jnp.where(rowi == n - 1, 0.0, pltpu.roll(u, n - 1, 0))
    w = cw_ref[...]
    return up * w[0:1] + u * w[1:2] + un * w[2:3] + cb_ref[...]


def _hyena_ctx_kernel(hu_ref, cw_ref, cb_ref, kr_ref, ki_ref, fwd_ref, inv_ref, skip_ref, sg_ref, o_ref):
    uc = _short_conv(hu_ref, cw_ref, cb_ref)
    x0, x1, v = uc[:, :HY_CH], uc[:, HY_CH:2 * HY_CH], uc[:, 2 * HY_CH:]
    z = x1 * v
    zf = _dot(fwd_ref[...], z.astype(BF16))
    big_l = zf.shape[0] // 2
    zr, zi = zf[:big_l], zf[big_l:]
    kr, ki = kr_ref[...], ki_ref[...]
    p = jnp.concatenate([zr * kr - zi * ki, zr * ki + zi * kr], axis=0).astype(BF16)
    y = _dot(inv_ref[...], p)
    out = x0 * (y + z * skip_ref[...])
    o_ref[...] = (out * sg_ref[...].astype(F32)).astype(BF16)


def _pitch(rows):
    return rows + PITCH_PAD


def _to_pitched(dst, src, blk):
    for i in range(src.shape[0] // blk):
        dst[i * _pitch(blk):i * _pitch(blk) + blk, :] = src[i * blk:(i + 1) * blk, :]


def _block_diag_rows(a, b):
    za, zb = jnp.zeros_like(a), jnp.zeros_like(b)
    return jnp.concatenate([jnp.concatenate([a, zb], axis=1), jnp.concatenate([za, b], axis=1)], axis=0)


def _fft_stage1(src, m1_ref, s_re, s_im, n1, n2):
    c = src.shape[1]
    for i in range(n2 // 2):
        xs = [src[pl.ds(2 * i + t, n1 // 2, stride=_pitch(n2)), :].astype(BF16) for t in range(2)]
        a = _dot(m1_ref[i], _block_diag_rows(*xs))
        for t in range(2):
            r0 = (2 * i + t) * _pitch(n1)
            s_re[r0:r0 + n1, :] = a[:n1, t * c:(t + 1) * c]
            s_im[r0:r0 + n1, :] = a[n1:, t * c:(t + 1) * c]


def _fft_stage2(w2_ref, s_re, s_im, i, n1, n2):
    re = [s_re[pl.ds(2 * i + t, n2, stride=_pitch(n1)), :] for t in range(2)]
    im = [s_im[pl.ds(2 * i + t, n2, stride=_pitch(n1)), :] for t in range(2)]
    rhs = jnp.concatenate([jnp.concatenate(re, axis=1), jnp.concatenate(im, axis=1)], axis=0)
    x = _dot(w2_ref[...], rhs.astype(BF16))
    return x[:n2], x[n2:]


def _filter_fft_kernel(hf_ref, hb_ref, m1_ref, w2_ref, kr_ref, ki_ref, x_scr, s_re, s_im, *, n1, n2):
    c = kr_ref.shape[1]
    for src_ref, sign in ((hf_ref, 1.0), (hb_ref, -1.0)):
        _to_pitched(x_scr, src_ref, n2)
        _fft_stage1(x_scr, m1_ref, s_re, s_im, n1, n2)
        for i in range(n1 // 2):
            xr, xi = _fft_stage2(w2_ref, s_re, s_im, i, n1, n2)
            for t in range(2):
                rows = slice((2 * i + t) * n2, (2 * i + t + 1) * n2)
                if sign > 0:
                    kr_ref[rows, :] = xr[:, t * c:(t + 1) * c]
                    ki_ref[rows, :] = xi[:, t * c:(t + 1) * c]
                else:
                    kr_ref[rows, :] += xr[:, t * c:(t + 1) * c]
                    ki_ref[rows, :] -= xi[:, t * c:(t + 1) * c]


def _filter_fft(hf, hb0, m1, w2, n):
    big_l = 2 * n
    n2 = HY_N2
    n1 = big_l // n2
    inp = pl.BlockSpec((None, n, LANES), lambda l, c: (l, 0, c))
    out = pl.BlockSpec((None, big_l, LANES), lambda l, c: (l, 0, c))
    return pl.pallas_call(
        functools.partial(_filter_fft_kernel, n1=n1, n2=n2),
        grid=(DEPTH, HY_CH // LANES),
        in_specs=[inp, inp,
                  pl.BlockSpec(m1.shape, lambda l, c: (0, 0, 0)),
                  pl.BlockSpec(w2.shape, lambda l, c: (0, 0))],
        out_specs=[out, out],
        out_shape=[jax.ShapeDtypeStruct((DEPTH, big_l, HY_CH), F32)] * 2,
        scratch_shapes=[pltpu.VMEM((n1 // 2 * _pitch(n2), LANES), F32)]
                       + [pltpu.VMEM((n2 * _pitch(n1), LANES), F32)] * 2,
        compiler_params=_cparams("arbitrary", "arbitrary"), name="hyena_filter_fft",
    )(hf, hb0, m1, w2)


def _hyena_lat_kernel(x0_ref, x1_ref, v_ref, cw0_ref, cw1_ref, cw2_ref, cb0_ref, cb1_ref, cb2_ref,
                      kr_ref, ki_ref, m1_ref, w2_ref, ma_ref, mb_ref, skip_ref, sg_ref, o_ref,
                      z_scr, y_scr, s_re, s_im, *, n1, n2):
    n1h = n1 // 2
    c = o_ref.shape[1]
    z = _short_conv(x1_ref, cw1_ref, cb1_ref) * _short_conv(v_ref, cw2_ref, cb2_ref)
    _to_pitched(z_scr, z, n2)
    _fft_stage1(z_scr, m1_ref, s_re, s_im, n1, n2)
    for i in range(n1 // 2):
        xr, xi = _fft_stage2(w2_ref, s_re, s_im, i, n1, n2)
        kr = jnp.concatenate([kr_ref[(2 * i + t) * n2:(2 * i + t + 1) * n2, :] for t in range(2)], axis=1)
        ki = jnp.concatenate([ki_ref[(2 * i + t) * n2:(2 * i + t + 1) * n2, :] for t in range(2)], axis=1)
        pr = (xr * kr - xi * ki).astype(BF16)
        pi = (xr * ki + xi * kr).astype(BF16)
        ps = [jnp.concatenate([pr[:, t * c:(t + 1) * c], pi[:, t * c:(t + 1) * c]], axis=0) for t in range(2)]
        b = _dot(ma_ref[i], _block_diag_rows(*ps))
        for t in range(2):
            s_re[pl.ds(2 * i + t, n2, stride=_pitch(n1)), :] = b[:n2, t * c:(t + 1) * c]
            s_im[pl.ds(2 * i + t, n2, stride=_pitch(n1)), :] = b[n2:, t * c:(t + 1) * c]
    for i in range(n2 // 2):
        blks = []
        for t in range(2):
            r0 = (2 * i + t) * _pitch(n1)
            blks.append(jnp.concatenate([s_re[r0:r0 + n1, :], s_im[r0:r0 + n1, :]], axis=0))
        y = _dot(mb_ref[...], jnp.concatenate(blks, axis=1).astype(BF16))
        for t in range(2):
            y_scr[pl.ds(2 * i + t, n1h, stride=_pitch(n2)), :] = y[:, t * c:(t + 1) * c]
    x0 = _short_conv(x0_ref, cw0_ref, cb0_ref)
    skip = skip_ref[...]
    for j1 in range(n1h):
        rows = slice(j1 * n2, (j1 + 1) * n2)
        prow = slice(j1 * _pitch(n2), j1 * _pitch(n2) + n2)
        out = x0[rows] * (y_scr[prow, :] + z_scr[prow, :] * skip)
        o_ref[rows, :] = (out * sg_ref[rows, :].astype(F32)).astype(BF16)


def _hyena_lat(hu, sg, conv_w, conv_b, skip, kr, ki, tables, bsz, n, layer):
    m1, w2, ma, mb = tables
    big_l = 2 * n
    n2 = HY_N2
    n1 = big_l // n2
    nch = HY_CH // LANES
    ucol = lambda s: pl.BlockSpec((n, LANES), lambda c, b: (b, s * nch + c))
    wcol = lambda s: pl.BlockSpec((None, 3, LANES), lambda c, b: (layer, 0, s * nch + c))
    bcol = lambda s: pl.BlockSpec((None, 1, LANES), lambda c, b: (layer, 0, s * nch + c))
    kspec = pl.BlockSpec((None, big_l, LANES), lambda c, b: (layer, 0, c))
    const = lambda a: pl.BlockSpec(a.shape, lambda c, b: (0,) * a.ndim)
    cb3 = conv_b.reshape(DEPTH, 1, 768)
    return pl.pallas_call(
        functools.partial(_hyena_lat_kernel, n1=n1, n2=n2),
        grid=(nch, bsz),
        in_specs=[ucol(0), ucol(1), ucol(2), wcol(0), wcol(1), wcol(2), bcol(0), bcol(1), bcol(2),
                  kspec, kspec, const(m1), const(w2), const(ma), const(mb),
                  pl.BlockSpec((None, 1, LANES), lambda c, b: (layer, 0, c)),
                  pl.BlockSpec((n, LANES), lambda c, b: (b, 2 * nch + c))],
        out_specs=pl.BlockSpec((n, LANES), lambda c, b: (b, c)),
        out_shape=jax.ShapeDtypeStruct((bsz * n, 256), BF16),
        scratch_shapes=[pltpu.VMEM((n1 // 2 * _pitch(n2), LANES), F32)] * 2
                       + [pltpu.VMEM((n2 * _pitch(n1), LANES), F32)] * 2,
        compiler_params=_cparams("arbitrary", "arbitrary"), name="hyena_lat",
    )(hu, hu, hu, conv_w, conv_w, conv_w, cb3, cb3, cb3, kr, ki, m1, w2, ma, mb,
      skip.reshape(DEPTH, 1, HY_CH), sg)


def _fnet_ctx_kernel(u_ref, cs_ref, dn_ref, fw_ref, fb_ref, sg_ref, o_ref):
    ab = _dot(u_ref[...], cs_ref[...])
    stack = jnp.concatenate([ab[:, :GROUP_W], ab[:, GROUP_W:]], axis=0).astype(BF16)
    f = _dot(dn_ref[...], stack)
    out = _bdot(f, fw_ref[...]) + fb_ref[...]
    o_ref[...] = (out * sg_ref[...].astype(F32)).astype(BF16)


def _fnet_lat_kernel(u_ref, cs_ref, f1_ref, f2_ref, fw_ref, fb_ref, sg_ref, o_ref,
                     a_scr, b_scr, s_re, s_im, o_scr, *, n1, n2):
    halves = GROUP_W // LANES

    def put(scr, rows, val):
        for hh in range(halves):
            scr[hh, rows, :] = val[:, hh * LANES:(hh + 1) * LANES]

    def get(scr, rows):
        return jnp.concatenate([scr[hh, rows, :] for hh in range(halves)], axis=1)

    def block(i, blk):
        return slice(i * _pitch(blk), i * _pitch(blk) + blk)

    ab = _dot(u_ref[...], cs_ref[...])
    for j1 in range(n1):
        put(a_scr, block(j1, n2), ab[j1 * n2:(j1 + 1) * n2, :GROUP_W])
        put(b_scr, block(j1, n2), ab[j1 * n2:(j1 + 1) * n2, GROUP_W:])
    for j2 in range(n2):
        rows = pl.ds(j2, n1, stride=_pitch(n2))
        g = jnp.concatenate([get(a_scr, rows), get(b_scr, rows)], axis=0)
        t = _dot(f1_ref[j2], g.astype(BF16))
        put(s_re, block(j2, n1), t[:n1])
        put(s_im, block(j2, n1), t[n1:])
    for k1 in range(n1):
        rows = pl.ds(k1, n2, stride=_pitch(n1))
        g = jnp.concatenate([get(s_re, rows), get(s_im, rows)], axis=0)
        put(o_scr, rows, _dot(f2_ref[...], g.astype(BF16)))
    fw = fw_ref[...].astype(BF16)
    fb = fb_ref[...]
    for k2 in range(n2):
        rows = slice(k2 * n1, (k2 + 1) * n1)
        out = _dot(get(o_scr, block(k2, n1)).astype(BF16), fw) + fb
        o_ref[rows, :] = (out * sg_ref[rows, :].astype(F32)).astype(BF16)


def _fnet_lat(fu, sg, fn_w, fn_b, cs, f1, f2, bsz, n, layer):
    n2 = FN_N2
    n1 = n // n2
    return pl.pallas_call(
        functools.partial(_fnet_lat_kernel, n1=n1, n2=n2), grid=(bsz,),
        in_specs=[pl.BlockSpec((n, 256), lambda i: (i, 0)),
                  pl.BlockSpec(cs.shape, lambda i: (0, 0)),
                  pl.BlockSpec(f1.shape, lambda i: (0, 0, 0)),
                  pl.BlockSpec(f2.shape, lambda i: (0, 0)),
                  pl.BlockSpec((None, GROUP_W, GROUP_W), lambda i: (layer, 0, 0)),
                  pl.BlockSpec((None, 1, GROUP_W), lambda i: (layer, 0, 0)),
                  pl.BlockSpec((n, 256), lambda i: (i, 3))],
        out_specs=pl.BlockSpec((n, 256), lambda i: (i, 0)),
        out_shape=jax.ShapeDtypeStruct((bsz * n, 256), BF16),
        scratch_shapes=[pltpu.VMEM((GROUP_W // LANES, n1 * _pitch(n2), LANES), F32)] * 2
                       + [pltpu.VMEM((GROUP_W // LANES, n2 * _pitch(n1), LANES), F32)] * 3,
        compiler_params=_cparams("arbitrary"), name="fnet_lat",
    )(fu, cs, f1, f2, fn_w, fn_b.reshape(DEPTH, 1, GROUP_W), sg)


def _ctx_mixers_kernel(sink_ref, aq_ref, akt_ref, av_ref, lam_ref, subln_ref, dq_ref, dkt_ref, dv_ref,
                       hu_ref, cw_ref, cb_ref, kr_ref, ki_ref, fwd_ref, inv_ref, skip_ref,
                       fu_ref, cs_ref, dn_ref, fw_ref, fb_ref, sg_ref, o_ref, s_scr, p_scr, *, layer, lam_init, kc):
    sg = [sg_ref.at[:, j * GROUP_W:(j + 1) * GROUP_W] for j in range(4)]
    out = [o_ref.at[:, j * GROUP_W:(j + 1) * GROUP_W] for j in range(4)]
    _attn_a_kernel(sink_ref, aq_ref, akt_ref, av_ref, sg[0], out[0], layer=layer)
    _diff_kernel(lam_ref, subln_ref, dq_ref, dkt_ref, dv_ref, sg[1], out[1], s_scr, p_scr,
                 lam_init=lam_init, npieces=1, tq_sub=dq_ref.shape[0], kc=kc)
    _hyena_ctx_kernel(hu_ref, cw_ref, cb_ref, kr_ref, ki_ref, fwd_ref, inv_ref, skip_ref, sg[2], out[2])
    _fnet_ctx_kernel(fu_ref, cs_ref, dn_ref, fw_ref, fb_ref, sg[3], out[3])


def _exit_kernel(x_ref, *refs, latent, tiles_per_b):
    mix_refs, (w_ref, g_ref, mod_ref, o_ref) = refs[:-4], refs[-4:]
    row = (pl.program_id(0) // tiles_per_b) if latent else CTX_ROW
    gate = mod_ref[pl.ds(row, 1), :][:, 2 * D_MODEL:]
    mixed = mix_refs[0][...] if len(mix_refs) == 1 else jnp.concatenate([r[...] for r in mix_refs], axis=1)
    y = _dot(mixed, w_ref[...])
    y = y * lax.rsqrt(jnp.mean(y * y, axis=-1, keepdims=True) + EPS) * g_ref[...]
    o_ref[...] = x_ref[...] + gate * y


def _layer_exit(x2d, outs, w_out_bf, g_post, mod_all, n, layer, latent):
    t = x2d.shape[0]
    tm = 512
    tiles_per_b = n // tm if latent else 1
    row = lambda i: (i, 0)
    return pl.pallas_call(
        functools.partial(_exit_kernel, latent=latent, tiles_per_b=tiles_per_b),
        grid=(t // tm,),
        in_specs=[pl.BlockSpec((tm, D_MODEL), row)] + [pl.BlockSpec((tm, o.shape[1]), row) for o in outs]
                 + [pl.BlockSpec((None, D_MODEL, D_MODEL), lambda i: (layer, 0, 0)),
                    pl.BlockSpec((None, 1, D_MODEL), lambda i: (layer, 0, 0)),
                    pl.BlockSpec((None, COND_ROWS, 3 * D_MODEL), lambda i: (layer, 0, 0))],
        out_specs=pl.BlockSpec((tm, D_MODEL), row),
        out_shape=jax.ShapeDtypeStruct((t, D_MODEL), F32),
        compiler_params=_cparams("arbitrary"),
        name="exit_latent" if latent else "exit_ctx",
    )(x2d, *outs, w_out_bf, g_post.reshape(DEPTH, 1, D_MODEL), mod_all)


def _ctx_layer_kernel(x_ref, mod_ref, gpre_ref, win_ref, _c0, _c1, _c2, _c3, sink_ref, lam_ref, subln_ref,
                      cw_ref, cb_ref, kr_ref, ki_ref, fwd_ref, inv_ref, skip_ref, cs_ref, dn_ref, fw_ref, fb_ref,
                      wout_ref, gpost_ref,
                      xo_ref, ak32_ref, av32_ref, dk32_ref, dv32_ref,
                      aq, akt, av, dq, dkt, dv, hu, fu, sg, mixed, s_scr, p_scr, *, layer, lam_init, kc):
    _entry_kernel(x_ref, mod_ref, gpre_ref, win_ref, aq, akt, av, dq, dkt, dv, hu, fu, sg,
                  ak32_ref, av32_ref, dk32_ref, dv32_ref, latent=False, tiles_per_b=1)
    _ctx_mixers_kernel(sink_ref, aq, akt, av, lam_ref, subln_ref, dq, dkt, dv, hu, cw_ref, cb_ref, kr_ref, ki_ref,
                       fwd_ref, inv_ref, skip_ref, fu, cs_ref, dn_ref, fw_ref, fb_ref, sg, mixed, s_scr, p_scr,
                       layer=layer, lam_init=lam_init, kc=kc)
    _exit_kernel(x_ref, mixed, wout_ref, gpost_ref, mod_ref, xo_ref, latent=False, tiles_per_b=1)


def _ctx_layer(x2d, caches, mod_all, g_pre, w_in_bf, sink, diff_lambda, diff_subln, conv_w, conv_b, skip, kr, ki,
               fwd_t, inv_t, fn_w, fn_b, cs, dn, w_out_bf, g_post, bsz, n, layer, kc=512):
    lam_init = 0.8 - 0.6 * math.exp(-0.3 * layer)
    big_l = 2 * n
    rows = lambda c: pl.BlockSpec((n, c), lambda b: (b, 0))
    lay = lambda a, c: pl.BlockSpec((None, a, c), lambda b: (layer, 0, 0))
    const = lambda a: pl.BlockSpec(a.shape, lambda b: (0,) * a.ndim)
    hbm = pl.BlockSpec(memory_space=pl.ANY)
    cache_out = lambda a: pl.BlockSpec((None, None, n, a.shape[-1]), lambda b: (b, layer, 0, 0))
    nslot = DIF_HEADS * 2 + 1
    vm = lambda shape, dt=BF16: pltpu.VMEM(shape, dt)
    res = pl.pallas_call(
        functools.partial(_ctx_layer_kernel, layer=layer, lam_init=lam_init, kc=kc),
        grid=(bsz,),
        in_specs=[rows(D_MODEL), lay(COND_ROWS, 3 * D_MODEL), lay(1, D_MODEL), lay(D_MODEL, D_IN),
                  hbm, hbm, hbm, hbm, pl.BlockSpec(memory_space=pltpu.SMEM),
                  lay(4, DIF_QK_DIM), lay(1, DIF_V_DIM), lay(3, 768), lay(1, 768), lay(big_l, HY_CH), lay(big_l, HY_CH),
                  const(fwd_t), const(inv_t), lay(1, HY_CH), const(cs), const(dn), lay(GROUP_W, GROUP_W),
                  lay(1, GROUP_W), lay(D_MODEL, D_MODEL), lay(1, D_MODEL)],
        out_specs=[rows(D_MODEL)] + [cache_out(a) for a in caches],
        out_shape=[jax.ShapeDtypeStruct(x2d.shape, F32)] + [jax.ShapeDtypeStruct(a.shape, a.dtype) for a in caches],
        input_output_aliases={4 + j: 1 + j for j in range(4)},
        scratch_shapes=[vm((n, 256)), vm((128, n)), vm((n, 128)), vm((n, 256)), vm((256, n)), vm((n, 512)),
                        vm((n, 768)), vm((n, 256)), vm((n, 1024)), vm((n, 1024)),
                        vm((nslot, n, n), F32), vm((nslot, n, n))],
        compiler_params=_cparams("arbitrary"), name="ctx_layer",
    )(x2d, mod_all, g_pre.reshape(DEPTH, 1, D_MODEL), w_in_bf, *caches, sink, diff_lambda,
      diff_subln.reshape(DEPTH, 1, DIF_V_DIM), conv_w, conv_b.reshape(DEPTH, 1, 768), kr, ki, fwd_t, inv_t,
      skip.reshape(DEPTH, 1, HY_CH), cs, dn, fn_w, fn_b.reshape(DEPTH, 1, GROUP_W), w_out_bf,
      g_post.reshape(DEPTH, 1, D_MODEL))
    return res[0], list(res[1:])


def kernel(x_prompt, x_sample, cache_attn_k, cache_attn_v, cache_diff_k, cache_diff_v, c, c_ctx, w_ada, b_ada, norm_pre, norm_post, w_in, w_out, attn_sink, diff_lambda, diff_subln, hy_conv_w, hy_conv_b, hy_filt_w1, hy_filt_b1, hy_filt_w2, hy_filt_b2, hy_filt_w3, hy_filt_freq, hy_skip, fn_w, fn_b):
    bp, lp, _ = x_prompt.shape
    bs, ls, _ = x_sample.shape
    past = cache_attn_k.shape[2]
    assert bs < CTX_ROW + 1 <= COND_ROWS

    cond = jnp.concatenate([c, c_ctx[None, :], jnp.zeros((COND_ROWS - bs - 1, D_MODEL), F32)], axis=0)
    mod_all = _modulation(cond, w_ada, b_ada)
    w_in_bf = w_in.astype(BF16)
    w_out_bf = w_out.astype(BF16)
    sink = attn_sink.reshape(DEPTH * ATT_HEADS)

    rope = _rope_tables(ls, ATT_HEAD_DIM) + _rope_tables(ls, DIF_QK_DIM)
    filt = (jnp.pad(hy_filt_w1, ((0, 0), (0, LANES - FILT_EMB), (0, 0))), hy_filt_b1, hy_filt_w2, hy_filt_b2,
            hy_filt_w3, hy_filt_freq)
    cos_c, sin_c, fwd_c, inv_c = _dense_conv_tables(lp)
    kr_c, ki_c = _hyena_filters_ctx(lp, filt, cos_c, sin_c)
    hy_tables = _hyena_fft_tables(ls)
    hf_l, hb_l = _hyena_filters_lat(ls, filt)
    kr_l, ki_l = _filter_fft(hf_l, hb_l, hy_tables[0], hy_tables[1], ls)
    cs_c, dn_c = _fnet_channel_table(lp), _fnet_dense_table(lp)
    cs_l = _fnet_channel_table(ls)
    f1_l, f2_l = _fnet_fft_tables(ls)

    cak = cache_attn_k.reshape(bs, DEPTH, past, 128).transpose(0, 1, 3, 2).astype(BF16)
    cav = cache_attn_v.reshape(bs, DEPTH, past, 128).astype(BF16)
    cdk = cache_diff_k.reshape(bs, DEPTH, past, 256).transpose(0, 1, 3, 2).astype(BF16)
    cdv = jnp.concatenate([cache_diff_v, jnp.ones_like(cache_diff_v)], axis=-1)
    cdv = cdv.reshape(bs, DEPTH, past, 2 * GROUP_W).astype(BF16)

    xp = x_prompt.reshape(bp * lp, D_MODEL)
    xs = x_sample.reshape(bs * ls, D_MODEL)
    new_caches = [jnp.zeros((bp, DEPTH, lp, c), F32) for c in (128, 128, 256, 256)]
    for l in range(DEPTH):
        xp, new_caches = _ctx_layer(xp, new_caches, mod_all, norm_pre, w_in_bf, sink, diff_lambda, diff_subln,
                                    hy_conv_w, hy_conv_b, hy_skip, kr_c, ki_c, fwd_c, inv_c, fn_w, fn_b, cs_c, dn_c,
                                    w_out_bf, norm_post, bp, lp, l)

        (aq, akt, av, dq, dkt, dv, hu, fu, sg) = _layer_entry(xs, bs, ls, mod_all, norm_pre, w_in_bf, l, rope)
        outs = (_attn_a_latent(aq, akt, av, cak[:, l], cav[:, l], sg, sink, bs, ls, l),
                _diff_attention(dq, [(dkt, dv), (cdk[:, l], cdv[:, l])], sg, diff_lambda, diff_subln,
                                bs, ls, l, 256),
                _hyena_lat(hu, sg, hy_conv_w, hy_conv_b, hy_skip, kr_l, ki_l, hy_tables, bs, ls, l),
                _fnet_lat(fu, sg, fn_w, fn_b, cs_l, f1_l, f2_l, bs, ls, l))
        xs = _layer_exit(xs, outs, w_out_bf, norm_post, mod_all, ls, l, True)

    nak, nav, ndk, ndv = new_caches
    return (xp.reshape(bp, lp, D_MODEL), xs.reshape(bs, ls, D_MODEL),
            nak.reshape(bp, DEPTH, lp, ATT_KV_HEADS, ATT_HEAD_DIM), nav.reshape(bp, DEPTH, lp, ATT_KV_HEADS, ATT_HEAD_DIM),
            ndk.reshape(bp, DEPTH, lp, 2, DIF_HEADS, DIF_QK_DIM), ndv.reshape(bp, DEPTH, lp, DIF_HEADS, DIF_V_DIM))
```

```python
import functools
import math

import numpy as np
import jax
import jax.numpy as jnp
from jax import lax
from jax.experimental import pallas as pl
from jax.experimental.pallas import tpu as pltpu

F32 = jnp.float32
BF16 = jnp.bfloat16

D_MODEL = 1024
DEPTH = 2
GRID_W = 64
GROUP_W = 256
ATT_HEADS = 4
ATT_KV_HEADS = 2
ATT_HEAD_DIM = 64
WINDOW = 128
BLOCK = 128
DIF_HEADS = 4
DIF_V_DIM = 64
DIF_QK_DIM = 32
HY_CH = 256
FILT_BANDS = 16
FILT_EMB = 1 + 2 * FILT_BANDS
FILT_HIDDEN = 64
HY_MIN_DECAY = math.log(1e-2) / 1.5
HY_MAX_DECAY = math.log(1e-2) / 0.3
FN_GROUP_CH = 64
ROPE_BASE = 10000.0
EPS = 1e-6
NEG_INF = -1e30

C_AQ, C_AK, C_AV, C_AG = 0, 256, 384, 512
C_DQ, C_DK, C_DV, C_DG = 768, 1024, 1280, 1536
C_HU, C_HG, C_FU, C_FG, D_IN = 1792, 2560, 2816, 3072, 3328
LOG2_E = math.log2(math.e)

LANES = 128
COND_ROWS = 8
CTX_ROW = 4
VMEM_LIMIT = 56 * 1024 * 1024
PITCH_PAD = 8

ROW_TILE = 1024
ATTN_A_QBLOCKS = 8
DIFF_Q_ROWS = 256
DIFF_KEY_CHUNK = 512

HY_N2 = 64
FN_N2 = 64


def _cparams(*sem):
    return pltpu.CompilerParams(dimension_semantics=sem, vmem_limit_bytes=VMEM_LIMIT)


def _dot(a, b):
    return jnp.dot(a, b, preferred_element_type=F32)


def _bdot(a, b):
    return jnp.dot(a.astype(BF16), b.astype(BF16), preferred_element_type=F32)


def _dot3(a, b):
    ah = a.astype(BF16)
    al = (a - ah.astype(F32)).astype(BF16)
    bh = b.astype(BF16)
    bl = (b - bh.astype(F32)).astype(BF16)
    return _dot(ah, bh) + _dot(ah, bl) + _dot(al, bh)


def _bf16_table(a):
    return jnp.asarray(a, F32).astype(BF16)


def _rope_tables(n, head_dim):
    pos = np.arange(n)
    row = (pos // GRID_W).astype(np.float64)
    col = (pos % GRID_W).astype(np.float64)
    n_freq = head_dim // 4
    inv = ROPE_BASE ** (-np.arange(n_freq, dtype=np.float64) / n_freq)
    ang = np.concatenate([row[:, None] * inv, col[:, None] * inv], axis=-1)
    reps = LANES // head_dim
    cos = np.tile(np.concatenate([np.cos(ang), np.cos(ang)], axis=-1), (1, reps))
    sin = np.tile(np.concatenate([-np.sin(ang), np.sin(ang)], axis=-1), (1, reps))
    return jnp.asarray(cos, F32), jnp.asarray(sin, F32)


def _filter_feats(n):
    t = np.linspace(0.0, 1.0, n)[:, None]
    w = (2.0 * math.pi / n) * np.arange(n)[:, None]
    f = np.linspace(1e-4, FILT_BANDS - 1, FILT_BANDS)[None, :]
    feats = np.concatenate([t, np.cos(f * w), -np.sin(f * w)], axis=-1)
    feats = np.pad(feats, ((0, 0), (0, LANES - FILT_EMB)))
    feats = np.concatenate([feats[:n // 2], feats[n // 2:]], axis=1)
    deltas = np.abs(np.linspace(HY_MIN_DECAY, HY_MAX_DECAY, HY_CH))
    decay = np.exp(-t * deltas[None, :])
    return jnp.asarray(feats, F32), jnp.asarray(decay, F32)


def _dense_conv_tables(n):
    big_l = 2 * n
    k = np.arange(big_l)[:, None]
    t = np.arange(n)[None, :]
    th = 2.0 * math.pi * k * t / big_l
    fwd = np.concatenate([np.cos(th), -np.sin(th)], axis=0)
    inv = np.concatenate([np.cos(th).T, -np.sin(th).T], axis=1) / big_l
    return (jnp.asarray(np.cos(th), F32), jnp.asarray(np.sin(th), F32), _bf16_table(fwd), _bf16_table(inv))


def _hyena_fft_tables(n):
    big_l = 2 * n
    n2 = HY_N2
    n1 = big_l // n2
    n1h = n1 // 2
    k1 = np.arange(n1)
    j1 = np.arange(n1h)
    j2 = np.arange(n2)
    k2 = np.arange(n2)
    th = 2.0 * math.pi * k1[None, :, None] * (j1[None, None, :] * n2 + j2[:, None, None]) / big_l
    m1 = np.concatenate([np.cos(th), -np.sin(th)], axis=1)
    th2 = 2.0 * math.pi * k2[:, None] * j2[None, :] / n2
    c2, s2 = np.cos(th2), np.sin(th2)
    w2 = np.block([[c2, s2], [-s2, c2]])
    tha = 2.0 * math.pi * j2[None, :, None] * (k1[:, None, None] + n1 * k2[None, None, :]) / big_l
    ca, sa = np.cos(tha), np.sin(tha)
    ma = np.concatenate([np.concatenate([ca, -sa], axis=2), np.concatenate([sa, ca], axis=2)], axis=1)
    thb = 2.0 * math.pi * j1[:, None] * k1[None, :] / n1
    mb = np.concatenate([np.cos(thb), -np.sin(thb)], axis=1) / big_l
    m1 = np.concatenate([m1[0::2], m1[1::2]], axis=2)
    ma = np.concatenate([ma[0::2], ma[1::2]], axis=2)
    return _bf16_table(m1), _bf16_table(w2), _bf16_table(ma), _bf16_table(mb)


def _fnet_channel_table(n):
    m = np.arange(FN_GROUP_CH)
    th = 2.0 * math.pi * m[:, None] * m[None, :] / FN_GROUP_CH
    eye = np.eye(GROUP_W // FN_GROUP_CH)
    sc = 1.0 / math.sqrt(n * FN_GROUP_CH)
    cs = np.concatenate([np.kron(eye, np.cos(th)), np.kron(eye, np.sin(th))], axis=1) * sc
    return _bf16_table(cs)


def _fnet_dense_table(n):
    k = np.arange(n)
    th = 2.0 * math.pi * k[:, None] * k[None, :] / n
    return _bf16_table(np.concatenate([np.cos(th), -np.sin(th)], axis=1))


def _fnet_fft_tables(n):
    n2 = FN_N2
    n1 = n // n2
    k1 = np.arange(n1)
    j1 = np.arange(n1)
    j2 = np.arange(n2)
    k2 = np.arange(n2)
    th = 2.0 * math.pi * k1[None, :, None] * (j1[None, None, :] * n2 + j2[:, None, None]) / n
    c, s = np.cos(th), np.sin(th)
    f1 = np.concatenate([np.concatenate([c, -s], axis=2), np.concatenate([s, c], axis=2)], axis=1)
    th2 = 2.0 * math.pi * k2[:, None] * j2[None, :] / n2
    f2 = np.concatenate([np.cos(th2), -np.sin(th2)], axis=1)
    return _bf16_table(f1), _bf16_table(f2)


def _mod_kernel(cond_ref, w_ref, b_ref, o_ref):
    c = cond_ref[...]
    s = c * jax.nn.sigmoid(c)
    o_ref[...] = _dot3(s, w_ref[...]) + b_ref[...]


def _modulation(cond, w_ada, b_ada):
    tn = 1024
    return pl.pallas_call(
        _mod_kernel,
        grid=(DEPTH, 3 * D_MODEL // tn),
        in_specs=[pl.BlockSpec((COND_ROWS, D_MODEL), lambda l, j: (0, 0)),
                  pl.BlockSpec((None, D_MODEL, tn), lambda l, j: (l, 0, j)),
                  pl.BlockSpec((None, 1, tn), lambda l, j: (l, 0, j))],
        out_specs=pl.BlockSpec((None, COND_ROWS, tn), lambda l, j: (l, 0, j)),
        out_shape=jax.ShapeDtypeStruct((DEPTH, COND_ROWS, 3 * D_MODEL), F32),
        compiler_params=_cparams("arbitrary", "arbitrary"),
        name="modulation",
    )(cond, w_ada, b_ada.reshape(DEPTH, 1, 3 * D_MODEL))


def _rope(x, cos, sin, half):
    lane = lax.broadcasted_iota(jnp.int32, x.shape, 1)
    first = (lane % (2 * half)) < half
    partner = jnp.where(first, pltpu.roll(x, LANES - half, 1), pltpu.roll(x, half, 1))
    return x * cos + partner * sin


def _entry_kernel(*refs, latent, tiles_per_b):
    if latent:
        (x_ref, mod_ref, g_ref, w_ref, ca_ref, sa_ref, cd_ref, sd_ref,
         aq_ref, akt_ref, av_ref, dq_ref, dkt_ref, dv_ref, hu_ref, fu_ref, sg_ref) = refs
        row = pl.program_id(0) // tiles_per_b
    else:
        (x_ref, mod_ref, g_ref, w_ref,
         aq_ref, akt_ref, av_ref, dq_ref, dkt_ref, dv_ref, hu_ref, fu_ref, sg_ref,
         ak32_ref, av32_ref, dk32_ref, dv32_ref) = refs
        row = CTX_ROW
    m = mod_ref[pl.ds(row, 1), :]
    shift, scale = m[:, :D_MODEL], m[:, D_MODEL:2 * D_MODEL]
    x = x_ref[...]
    h = x * lax.rsqrt(jnp.mean(x * x, axis=-1, keepdims=True) + EPS) * g_ref[...]
    hb = (h * (1.0 + scale) + shift).astype(BF16)

    def proj(c0, c1):
        return _dot(hb, w_ref[:, c0:c1])

    def roped(p, cos_ref, sin_ref, half):
        if not latent:
            return p
        cos, sin = cos_ref[...], sin_ref[...]
        chunks = [_rope(p[:, j:j + LANES], cos, sin, half) for j in range(0, p.shape[1], LANES)]
        return chunks[0] if len(chunks) == 1 else jnp.concatenate(chunks, axis=1)

    ca = sa = cd = sd = None
    if latent:
        ca, sa, cd, sd = ca_ref, sa_ref, cd_ref, sd_ref

    aq_ref[...] = (roped(proj(C_AQ, C_AK), ca, sa, ATT_HEAD_DIM // 2) * (ATT_HEAD_DIM ** -0.5 * LOG2_E)).astype(BF16)
    ak = roped(proj(C_AK, C_AV), ca, sa, ATT_HEAD_DIM // 2)
    akt_ref[...] = ak.T.astype(BF16)
    av = proj(C_AV, C_AG)
    av_ref[...] = av.astype(BF16)
    dq_ref[...] = (roped(proj(C_DQ, C_DK), cd, sd, DIF_QK_DIM // 2) * (DIF_QK_DIM ** -0.5 * LOG2_E)).astype(BF16)
    dk = roped(proj(C_DK, C_DV), cd, sd, DIF_QK_DIM // 2)
    dkt_ref[...] = dk.T.astype(BF16)
    dv = proj(C_DV, C_DG)
    value_lane = lax.broadcasted_iota(jnp.int32, (dv.shape[0], LANES), 1) < DIF_V_DIM
    for h in range(DIF_HEADS):
        pair = dv[:, (h // 2) * LANES:(h // 2 + 1) * LANES]
        if h % 2:
            pair = pltpu.roll(pair, DIF_V_DIM, 1)
        dv_ref[:, h * LANES:(h + 1) * LANES] = jnp.where(value_lane, pair, 1.0).astype(BF16)
    if not latent:
        ak32_ref[...] = ak
        av32_ref[...] = av
        dk32_ref[...] = dk
        dv32_ref[...] = dv
    hu_ref[...] = proj(C_HU, C_HG).astype(BF16)
    fu_ref[...] = proj(C_FU, C_FG).astype(BF16)
    for j, c0 in enumerate((C_AG, C_DG, C_HG, C_FG)):
        g = proj(c0, c0 + GROUP_W)
        sg_ref[:, j * GROUP_W:(j + 1) * GROUP_W] = (g * jax.nn.sigmoid(g)).astype(BF16)


def _layer_entry(x2d, bsz, n, mod_all, g_pre, w_in_bf, layer, rope):
    t = bsz * n
    tm = ROW_TILE
    tiles_per_b = n // tm
    row = lambda i: (i, 0)
    pos = lambda i: (i % tiles_per_b, 0)
    kt_map = lambda i: (i // tiles_per_b, 0, i % tiles_per_b)
    in_specs = [pl.BlockSpec((tm, D_MODEL), row),
                pl.BlockSpec((None, COND_ROWS, 3 * D_MODEL), lambda i: (layer, 0, 0)),
                pl.BlockSpec((None, 1, D_MODEL), lambda i: (layer, 0, 0)),
                pl.BlockSpec((None, D_MODEL, D_IN), lambda i: (layer, 0, 0))] + [pl.BlockSpec((tm, LANES), pos)] * 4
    bf = lambda c: jax.ShapeDtypeStruct((t, c), BF16)
    rs = lambda c: pl.BlockSpec((tm, c), row)
    out_shape = [bf(256), jax.ShapeDtypeStruct((bsz, 128, n), BF16), bf(128),
                 bf(256), jax.ShapeDtypeStruct((bsz, 256, n), BF16), bf(512),
                 bf(768), bf(256), bf(1024)]
    out_specs = [rs(256), pl.BlockSpec((None, 128, tm), kt_map), rs(128),
                 rs(256), pl.BlockSpec((None, 256, tm), kt_map), rs(512),
                 rs(768), rs(256), rs(1024)]
    return pl.pallas_call(
        functools.partial(_entry_kernel, latent=True, tiles_per_b=tiles_per_b),
        grid=(t // tm,), in_specs=in_specs, out_specs=out_specs, out_shape=out_shape,
        compiler_params=_cparams("arbitrary"), name="entry_latent",
    )(x2d, mod_all, g_pre.reshape(DEPTH, 1, D_MODEL), w_in_bf, *rope)


def _sink_columns(sink_ref, layer, tq):
    grp = ATT_HEADS // ATT_KV_HEADS
    head_row = lax.broadcasted_iota(jnp.int32, (grp * tq, 1), 0) // tq
    cols = []
    for g in range(ATT_KV_HEADS):
        col = jnp.zeros((grp * tq, 1), F32)
        for j in range(grp):
            col = jnp.where(head_row == j, sink_ref[layer * ATT_HEADS + g * grp + j] * LOG2_E, col)
        cols.append(col)
    return cols


def _stack_heads(q_ref, rows, g):
    grp = ATT_HEADS // ATT_KV_HEADS
    return jnp.concatenate([q_ref[rows, (g * grp + t) * ATT_HEAD_DIM:(g * grp + t + 1) * ATT_HEAD_DIM]
                            for t in range(grp)], axis=0)


def _attn_a_kernel(sink_ref, q_ref, k_ref, v_ref, sg_ref, o_ref, *, layer):
    tq = q_ref.shape[0]
    grp = ATT_HEADS // ATT_KV_HEADS
    sink_cols = _sink_columns(sink_ref, layer, tq)
    outs = []
    for g in range(ATT_KV_HEADS):
        d0 = g * ATT_HEAD_DIM
        s = _dot(_stack_heads(q_ref, slice(None), g), k_ref[d0:d0 + ATT_HEAD_DIM, :])
        m = jnp.maximum(jnp.max(s, axis=-1, keepdims=True), sink_cols[g])
        p = jnp.exp2(s - m)
        l = jnp.sum(p, axis=-1, keepdims=True) + jnp.exp2(sink_cols[g] - m)
        o = _dot(p.astype(BF16), v_ref[:, d0:d0 + ATT_HEAD_DIM]) / l
        outs += [o[j * tq:(j + 1) * tq] for j in range(grp)]
    o_full = jnp.concatenate(outs, axis=1)
    o_ref[...] = (o_full * sg_ref[...].astype(F32)).astype(BF16)


def _attn_a_win_kernel(sink_ref, q_ref, kp_ref, kc_ref, kn_ref, vp_ref, vc_ref, vn_ref, ck_ref, cv_ref, sg_ref,
                       o_ref, *, layer, nsteps, qb):
    i = pl.program_id(1)
    grp = ATT_HEADS // ATT_KV_HEADS
    rows = grp * BLOCK
    kw = jnp.concatenate([kp_ref[...], kc_ref[...], kn_ref[...]], axis=1)
    vw = jnp.concatenate([vp_ref[...], vc_ref[...], vn_ref[...]], axis=0)
    r = lax.broadcasted_iota(jnp.int32, (rows, 3 * BLOCK), 0) % BLOCK
    c = lax.broadcasted_iota(jnp.int32, (rows, 3 * BLOCK), 1)
    band = (c >= r) & (c <= r + 2 * WINDOW)
    masks = {0: band & ((c >= BLOCK) | (i > 0)), qb - 1: band & ((c < 2 * BLOCK) | (i < nsteps - 1))}
    sink_cols = _sink_columns(sink_ref, layer, BLOCK)
    ck = ck_ref[...]
    cv = cv_ref[...]
    items = [(j, g) for j in range(qb) for g in range(ATT_KV_HEADS)]
    st_a, st_b, outs = {}, {}, {}
    for step in range(len(items) + 2):
        a, b, cc = step, step - 1, step - 2
        if a < len(items):
            j, g = items[a]
            d0 = g * ATT_HEAD_DIM
            q2 = _stack_heads(q_ref, slice(j * BLOCK, (j + 1) * BLOCK), g)
            s_w = _dot(q2, kw[d0:d0 + ATT_HEAD_DIM, j * BLOCK:(j + 3) * BLOCK])
            s_w = jnp.where(masks.get(j, band), s_w, NEG_INF)
            s_c = _dot(q2, ck[d0:d0 + ATT_HEAD_DIM, :])
            m = jnp.maximum(jnp.maximum(jnp.max(s_w, axis=-1, keepdims=True),
                                        jnp.max(s_c, axis=-1, keepdims=True)), sink_cols[g])
            st_a[a] = (s_w, s_c, m)
        if 0 <= b < len(items):
            s_w, s_c, m = st_a.pop(b)
            p_w = jnp.exp2(s_w - m)
            p_c = jnp.exp2(s_c - m)
            l = (jnp.sum(p_w, axis=-1, keepdims=True) + jnp.sum(p_c, axis=-1, keepdims=True)
                 + jnp.exp2(sink_cols[items[b][1]] - m))
            st_b[b] = (p_w.astype(BF16), p_c.astype(BF16), l)
        if 0 <= cc < len(items):
            j, g = items[cc]
            d0 = g * ATT_HEAD_DIM
            p_w, p_c, l = st_b.pop(cc)
            o = (_dot(p_w, vw[j * BLOCK:(j + 3) * BLOCK, d0:d0 + ATT_HEAD_DIM])
                 + _dot(p_c, cv[:, d0:d0 + ATT_HEAD_DIM])) / l
            outs.setdefault(j, []).extend([o[t * BLOCK:(t + 1) * BLOCK] for t in range(grp)])
            if g == ATT_KV_HEADS - 1:
                rs = slice(j * BLOCK, (j + 1) * BLOCK)
                o_full = jnp.concatenate(outs.pop(j), axis=1)
                o_ref[rs, :] = (o_full * sg_ref[rs, :].astype(F32)).astype(BF16)


def _attn_a_latent(aq, akt, av, ck_t, cv, sg, sink, bsz, n, layer):
    qb = ATTN_A_QBLOCKS
    nb = n // BLOCK
    nsteps = nb // qb
    qmap = lambda b, i: (b * nsteps + i, 0)
    prev = lambda i: jnp.maximum(qb * i - 1, 0)
    nxt = lambda i: jnp.minimum(qb * i + qb, nb - 1)
    kedge = lambda f: pl.BlockSpec((None, 128, BLOCK), lambda b, i: (b, 0, f(i)))
    vedge = lambda f: pl.BlockSpec((BLOCK, 128), lambda b, i: (b * nb + f(i), 0))
    return pl.pallas_call(
        functools.partial(_attn_a_win_kernel, layer=layer, nsteps=nsteps, qb=qb),
        grid=(bsz, nsteps),
        in_specs=[pl.BlockSpec(memory_space=pltpu.SMEM),
                  pl.BlockSpec((qb * BLOCK, 256), qmap),
                  kedge(prev), pl.BlockSpec((None, 128, qb * BLOCK), lambda b, i: (b, 0, i)), kedge(nxt),
                  vedge(prev), pl.BlockSpec((qb * BLOCK, 128), qmap), vedge(nxt),
                  pl.BlockSpec((None, 128, ck_t.shape[2]), lambda b, i: (b, 0, 0)),
                  pl.BlockSpec((None, cv.shape[1], 128), lambda b, i: (b, 0, 0)),
                  pl.BlockSpec((qb * BLOCK, 256), qmap)],
        out_specs=pl.BlockSpec((qb * BLOCK, 256), qmap),
        out_shape=jax.ShapeDtypeStruct((bsz * n, 256), BF16),
        compiler_params=_cparams("arbitrary", "arbitrary"),
        name="attn_a_latent",
    )(sink, aq, akt, akt, akt, av, av, av, ck_t, cv, sg)


def _diff_kernel(lam_ref, subln_ref, q_ref, *refs, lam_init, npieces, tq_sub, kc):
    kts = refs[0:2 * npieces:2]
    vs = refs[1:2 * npieces:2]
    sg_ref, o_ref, s_scr, p_scr = refs[2 * npieces:2 * npieces + 4]
    lp = lam_ref[...]
    lam = (jnp.exp(jnp.sum(lp[0:1] * lp[1:2], axis=-1, keepdims=True))
           - jnp.exp(jnp.sum(lp[2:3] * lp[3:4], axis=-1, keepdims=True)) + lam_init)
    subln = subln_ref[...]
    tq = q_ref.shape[0]
    chunks, off = [], 0
    for kt, v in zip(kts, vs):
        for c0 in range(0, kt.shape[1], kc):
            w = min(kc, kt.shape[1] - c0)
            chunks.append((kt, v, c0, w, off))
            off += w
    items = [(r0, h, mp) for r0 in range(0, tq, tq_sub) for h in range(DIF_HEADS) for mp in range(2)]
    n_items = len(items)
    mx, acc, om, outs = {}, {}, {}, {}

    def fold(s):
        return functools.reduce(jnp.maximum, [s[:, j:j + LANES] for j in range(0, s.shape[1], LANES)])

    nslot = s_scr.shape[0]
    lag = nslot - 1
    for step in range(n_items + 2 * lag):
        a = step if step < n_items else None
        b = step - lag if 0 <= step - lag < n_items else None
        c = step - 2 * lag if 0 <= step - 2 * lag < n_items else None
        if a is not None:
            r0, h, mp = items[a]
            f0 = (mp * DIF_HEADS + h) * DIF_QK_DIM
            q_a = q_ref[r0:r0 + tq_sub, f0:f0 + DIF_QK_DIM]
            mx[a] = jnp.full((tq_sub, LANES), NEG_INF, F32)
        if b is not None:
            m_b = jnp.max(mx.pop(b), axis=-1, keepdims=True)
        if c is not None:
            acc[c] = jnp.zeros((tq_sub, 2 * DIF_V_DIM), F32)
            hc = items[c][1]
        for kt, v, c0, w, o0 in chunks:
            if a is not None:
                s = _dot(q_a, kt[f0:f0 + DIF_QK_DIM, c0:c0 + w])
                s_scr[a % nslot, :, o0:o0 + w] = s
                mx[a] = jnp.maximum(mx[a], fold(s))
            if b is not None:
                p_scr[b % nslot, :, o0:o0 + w] = jnp.exp2(s_scr[b % nslot, :, o0:o0 + w] - m_b).astype(BF16)
            if c is not None:
                acc[c] = acc[c] + _dot(p_scr[c % nslot, :, o0:o0 + w],
                                       v[c0:c0 + w, 2 * DIF_V_DIM * hc:2 * DIF_V_DIM * (hc + 1)])
        if c is not None:
            r0, h, mp = items[c]
            o = acc.pop(c)
            om[(r0, h, mp)] = o[:, :DIF_V_DIM] / o[:, DIF_V_DIM:DIF_V_DIM + 1]
            if mp == 1:
                av = om.pop((r0, h, 0)) - lam * om.pop((r0, h, 1))
                y = av * lax.rsqrt(jnp.mean(av * av, axis=-1, keepdims=True) + EPS) * subln * (1.0 - lam_init)
                outs.setdefault(r0, []).append(y)
                if h == DIF_HEADS - 1:
                    o_full = jnp.concatenate(outs.pop(r0), axis=1)
                    o_ref[r0:r0 + tq_sub, :] = (o_full * sg_ref[r0:r0 + tq_sub, :].astype(F32)).astype(BF16)


def _diff_attention(dq, pieces, sg, diff_lambda, diff_subln, bsz, n, layer):
    tq, kc = DIFF_Q_ROWS, DIFF_KEY_CHUNK
    lam_init = 0.8 - 0.6 * math.exp(-0.3 * layer)
    nt = n // tq
    nk_all = sum(kt.shape[2] for kt, _ in pieces)
    assert nk_all > kc
    nslot = 2
    in_specs = [pl.BlockSpec((None, 4, DIF_QK_DIM), lambda b, i: (layer, 0, 0)),
                pl.BlockSpec((None, 1, DIF_V_DIM), lambda b, i: (layer, 0, 0)),
                pl.BlockSpec((tq, 256), lambda b, i: (b * nt + i, 0))]
    args = [diff_lambda, diff_subln.reshape(DEPTH, 1, DIF_V_DIM), dq]
    for kt, v in pieces:
        nk = kt.shape[2]
        in_specs.append(pl.BlockSpec((None, 256, nk), lambda b, i: (b, 0, 0)))
        if v.ndim == 3:
            in_specs.append(pl.BlockSpec((None, nk, 512), lambda b, i: (b, 0, 0)))
        else:
            in_specs.append(pl.BlockSpec((nk, 512), lambda b, i: (b, 0)))
        args += [kt, v]
    in_specs.append(pl.BlockSpec((tq, 256), lambda b, i: (b * nt + i, 1)))
    args.append(sg)
    return pl.pallas_call(
        functools.partial(_diff_kernel, lam_init=lam_init, npieces=len(pieces), tq_sub=tq, kc=kc),
        grid=(bsz, nt), in_specs=in_specs,
        out_specs=pl.BlockSpec((tq, 256), lambda b, i: (b * nt + i, 0)),
        out_shape=jax.ShapeDtypeStruct((bsz * n, 256), BF16),
        scratch_shapes=[pltpu.VMEM((nslot, tq, nk_all), F32), pltpu.VMEM((nslot, tq, nk_all), BF16)],
        compiler_params=_cparams("arbitrary", "arbitrary"),
        name="diff_attn_latent",
    )(*args)


def _filter_core(feats_ref, w1_ref, b1_ref, w2_ref, b2_ref, w3_ref, fr_ref, decay_ref):
    fr = fr_ref[...]
    h = jnp.sin(fr * (_dot3(feats_ref[...], w1_ref[...]) + b1_ref[...]))
    h = jnp.sin(fr * (_dot3(h, w2_ref[...]) + b2_ref[...]))
    h = _dot3(h, w3_ref[...])
    h = jnp.concatenate([h[:, :2 * HY_CH], h[:, 2 * HY_CH:]], axis=0)
    decay = decay_ref[...]
    hf = h[:, :HY_CH] * decay
    hb = h[:, HY_CH:] * decay
    tot = (jnp.sum(jnp.abs(hf), axis=0, keepdims=True) + jnp.sum(jnp.abs(hb), axis=0, keepdims=True) + EPS)
    hf = hf / tot
    hb = hb / tot
    rowi = lax.broadcasted_iota(jnp.int32, hb.shape, 0)
    return hf, jnp.where(rowi == 0, 0.0, hb)


def _filter_ctx_kernel(feats_ref, w1_ref, b1_ref, w2_ref, b2_ref, w3_ref, fr_ref, decay_ref,
                       cos_ref, sin_ref, kr_ref, ki_ref):
    hf, hb0 = _filter_core(feats_ref, w1_ref, b1_ref, w2_ref, b2_ref, w3_ref, fr_ref, decay_ref)
    kr_ref[...] = _dot3(cos_ref[...], hf + hb0)
    ki_ref[...] = _dot3(sin_ref[...], hb0 - hf)


def _filter_lat_kernel(feats_ref, w1_ref, b1_ref, w2_ref, b2_ref, w3_ref, fr_ref, decay_ref, hf_ref, hb_ref):
    hf, hb0 = _filter_core(feats_ref, w1_ref, b1_ref, w2_ref, b2_ref, w3_ref, fr_ref, decay_ref)
    hf_ref[...] = hf
    hb_ref[...] = hb0


def _filter_specs(n, w1p, b1, w2, b2, w3, freq, feats, decay):
    const = lambda shape: pl.BlockSpec(shape, lambda l: (0,) * len(shape))
    lay = lambda a, b: pl.BlockSpec((None, a, b), lambda l: (l, 0, 0))
    hid = 2 * FILT_HIDDEN
    in_specs = [const((n // 2, 2 * LANES)), lay(2 * LANES, hid), lay(1, hid), lay(hid, hid), lay(1, hid),
                lay(hid, 4 * HY_CH), lay(1, hid), const((n, HY_CH))]
    row2 = lambda a: jnp.tile(a.reshape(DEPTH, 1, FILT_HIDDEN), (1, 1, 2))
    args = [feats, _block_diag2(w1p), row2(b1), _block_diag2(w2), row2(b2), _block_diag2(w3), row2(freq), decay]
    return in_specs, args


def _block_diag2(w):
    z = jnp.zeros_like(w)
    return jnp.concatenate([jnp.concatenate([w, z], axis=2), jnp.concatenate([z, w], axis=2)], axis=1)


def _hyena_filters_ctx(n, filt, cos_t, sin_t):
    feats, decay = _filter_feats(n)
    in_specs, args = _filter_specs(n, *filt, feats, decay)
    big_l = 2 * n
    in_specs += [pl.BlockSpec((big_l, n), lambda l: (0, 0))] * 2
    out = pl.BlockSpec((None, big_l, HY_CH), lambda l: (l, 0, 0))
    return pl.pallas_call(
        _filter_ctx_kernel, grid=(DEPTH,), in_specs=in_specs, out_specs=[out, out],
        out_shape=[jax.ShapeDtypeStruct((DEPTH, big_l, HY_CH), F32)] * 2,
        compiler_params=_cparams("arbitrary"), name="hyena_filter_ctx",
    )(*args, cos_t, sin_t)


def _hyena_filters_lat(n, filt):
    feats, decay = _filter_feats(n)
    in_specs, args = _filter_specs(n, *filt, feats, decay)
    out = pl.BlockSpec((None, n, HY_CH), lambda l: (l, 0, 0))
    return pl.pallas_call(
        _filter_lat_kernel, grid=(DEPTH,), in_specs=in_specs, out_specs=[out, out],
        out_shape=[jax.ShapeDtypeStruct((DEPTH, n, HY_CH), F32)] * 2,
        compiler_params=_cparams("arbitrary"), name="hyena_filter_lat",
    )(*args)


def _short_conv(u_ref, cw_ref, cb_ref):
    u = u_ref[...].astype(F32)
    n = u.shape[0]
    rowi = lax.broadcasted_iota(jnp.int32, u.shape, 0)
    up = jnp.where(rowi == 0, 0.0, pltpu.roll(u, 1, 0))
    un = jnp.where(rowi == n - 1, 0.0, pltpu.roll(u, n - 1, 0))
    w = cw_ref[...]
    return up * w[0:1] + u * w[1:2] + un * w[2:3] + cb_ref[...]


def _hyena_ctx_kernel(hu_ref, cw_ref, cb_ref, kr_ref, ki_ref, fwd_ref, inv_ref, skip_ref, sg_ref, o_ref):
    uc = _short_conv(hu_ref, cw_ref, cb_ref)
    x0, x1, v = uc[:, :HY_CH], uc[:, HY_CH:2 * HY_CH], uc[:, 2 * HY_CH:]
    z = x1 * v
    zf = _dot(fwd_ref[...], z.astype(BF16))
    big_l = zf.shape[0] // 2
    zr, zi = zf[:big_l], zf[big_l:]
    kr, ki = kr_ref[...], ki_ref[...]
    p = jnp.concatenate([zr * kr - zi * ki, zr * ki + zi * kr], axis=0).astype(BF16)
    y = _dot(inv_ref[...], p)
    out = x0 * (y + z * skip_ref[...])
    o_ref[...] = (out * sg_ref[...].astype(F32)).astype(BF16)


def _pitch(rows):
    return rows + PITCH_PAD


def _to_pitched(dst, src, blk):
    for i in range(src.shape[0] // blk):
        dst[i * _pitch(blk):i * _pitch(blk) + blk, :] = src[i * blk:(i + 1) * blk, :]


def _block_diag_rows(a, b):
    za, zb = jnp.zeros_like(a), jnp.zeros_like(b)
    return jnp.concatenate([jnp.concatenate([a, zb], axis=1), jnp.concatenate([za, b], axis=1)], axis=0)


def _fft_stage1(src, m1_ref, s_re, s_im, n1, n2):
    c = src.shape[1]
    for i in range(n2 // 2):
        xs = [src[pl.ds(2 * i + t, n1 // 2, stride=_pitch(n2)), :].astype(BF16) for t in range(2)]
        a = _dot(m1_ref[i], _block_diag_rows(*xs))
        for t in range(2):
            r0 = (2 * i + t) * _pitch(n1)
            s_re[r0:r0 + n1, :] = a[:n1, t * c:(t + 1) * c]
            s_im[r0:r0 + n1, :] = a[n1:, t * c:(t + 1) * c]


def _fft_stage2(w2_ref, s_re, s_im, i, n1, n2):
    re = [s_re[pl.ds(2 * i + t, n2, stride=_pitch(n1)), :] for t in range(2)]
    im = [s_im[pl.ds(2 * i + t, n2, stride=_pitch(n1)), :] for t in range(2)]
    rhs = jnp.concatenate([jnp.concatenate(re, axis=1), jnp.concatenate(im, axis=1)], axis=0)
    x = _dot(w2_ref[...], rhs.astype(BF16))
    return x[:n2], x[n2:]


def _filter_fft_kernel(hf_ref, hb_ref, m1_ref, w2_ref, kr_ref, ki_ref, x_scr, s_re, s_im, *, n1, n2):
    c = kr_ref.shape[1]
    for src_ref, sign in ((hf_ref, 1.0), (hb_ref, -1.0)):
        _to_pitched(x_scr, src_ref, n2)
        _fft_stage1(x_scr, m1_ref, s_re, s_im, n1, n2)
        for i in range(n1 // 2):
            xr, xi = _fft_stage2(w2_ref, s_re, s_im, i, n1, n2)
            for t in range(2):
                rows = slice((2 * i + t) * n2, (2 * i + t + 1) * n2)
                if sign > 0:
                    kr_ref[rows, :] = xr[:, t * c:(t + 1) * c]
                    ki_ref[rows, :] = xi[:, t * c:(t + 1) * c]
                else:
                    kr_ref[rows, :] += xr[:, t * c:(t + 1) * c]
                    ki_ref[rows, :] -= xi[:, t * c:(t + 1) * c]


def _filter_fft(hf, hb0, m1, w2, n):
    big_l = 2 * n
    n2 = HY_N2
    n1 = big_l // n2
    inp = pl.BlockSpec((None, n, LANES), lambda l, c: (l, 0, c))
    out = pl.BlockSpec((None, big_l, LANES), lambda l, c: (l, 0, c))
    return pl.pallas_call(
        functools.partial(_filter_fft_kernel, n1=n1, n2=n2),
        grid=(DEPTH, HY_CH // LANES),
        in_specs=[inp, inp,
                  pl.BlockSpec(m1.shape, lambda l, c: (0, 0, 0)),
                  pl.BlockSpec(w2.shape, lambda l, c: (0, 0))],
        out_specs=[out, out],
        out_shape=[jax.ShapeDtypeStruct((DEPTH, big_l, HY_CH), F32)] * 2,
        scratch_shapes=[pltpu.VMEM((n1 // 2 * _pitch(n2), LANES), F32)]
                       + [pltpu.VMEM((n2 * _pitch(n1), LANES), F32)] * 2,
        compiler_params=_cparams("arbitrary", "arbitrary"), name="hyena_filter_fft",
    )(hf, hb0, m1, w2)


def _hyena_lat_kernel(x0_ref, x1_ref, v_ref, cw0_ref, cw1_ref, cw2_ref, cb0_ref, cb1_ref, cb2_ref,
                      kr_ref, ki_ref, m1_ref, w2_ref, ma_ref, mb_ref, skip_ref, sg_ref, o_ref,
                      z_scr, y_scr, s_re, s_im, *, n1, n2):
    n1h = n1 // 2
    c = o_ref.shape[1]
    z = _short_conv(x1_ref, cw1_ref, cb1_ref) * _short_conv(v_ref, cw2_ref, cb2_ref)
    _to_pitched(z_scr, z, n2)
    _fft_stage1(z_scr, m1_ref, s_re, s_im, n1, n2)
    for i in range(n1 // 2):
        xr, xi = _fft_stage2(w2_ref, s_re, s_im, i, n1, n2)
        kr = jnp.concatenate([kr_ref[(2 * i + t) * n2:(2 * i + t + 1) * n2, :] for t in range(2)], axis=1)
        ki = jnp.concatenate([ki_ref[(2 * i + t) * n2:(2 * i + t + 1) * n2, :] for t in range(2)], axis=1)
        pr = (xr * kr - xi * ki).astype(BF16)
        pi = (xr * ki + xi * kr).astype(BF16)
        ps = [jnp.concatenate([pr[:, t * c:(t + 1) * c], pi[:, t * c:(t + 1) * c]], axis=0) for t in range(2)]
        b = _dot(ma_ref[i], _block_diag_rows(*ps))
        for t in range(2):
            s_re[pl.ds(2 * i + t, n2, stride=_pitch(n1)), :] = b[:n2, t * c:(t + 1) * c]
            s_im[pl.ds(2 * i + t, n2, stride=_pitch(n1)), :] = b[n2:, t * c:(t + 1) * c]
    for i in range(n2 // 2):
        blks = []
        for t in range(2):
            r0 = (2 * i + t) * _pitch(n1)
            blks.append(jnp.concatenate([s_re[r0:r0 + n1, :], s_im[r0:r0 + n1, :]], axis=0))
        y = _dot(mb_ref[...], jnp.concatenate(blks, axis=1).astype(BF16))
        for t in range(2):
            y_scr[pl.ds(2 * i + t, n1h, stride=_pitch(n2)), :] = y[:, t * c:(t + 1) * c]
    x0 = _short_conv(x0_ref, cw0_ref, cb0_ref)
    skip = skip_ref[...]
    for j1 in range(n1h):
        rows = slice(j1 * n2, (j1 + 1) * n2)
        prow = slice(j1 * _pitch(n2), j1 * _pitch(n2) + n2)
        out = x0[rows] * (y_scr[prow, :] + z_scr[prow, :] * skip)
        o_ref[rows, :] = (out * sg_ref[rows, :].astype(F32)).astype(BF16)


def _hyena_lat(hu, sg, conv_w, conv_b, skip, kr, ki, tables, bsz, n, layer):
    m1, w2, ma, mb = tables
    big_l = 2 * n
    n2 = HY_N2
    n1 = big_l // n2
    nch = HY_CH // LANES
    ucol = lambda s: pl.BlockSpec((n, LANES), lambda c, b: (b, s * nch + c))
    wcol = lambda s: pl.BlockSpec((None, 3, LANES), lambda c, b: (layer, 0, s * nch + c))
    bcol = lambda s: pl.BlockSpec((None, 1, LANES), lambda c, b: (layer, 0, s * nch + c))
    kspec = pl.BlockSpec((None, big_l, LANES), lambda c, b: (layer, 0, c))
    const = lambda a: pl.BlockSpec(a.shape, lambda c, b: (0,) * a.ndim)
    cb3 = conv_b.reshape(DEPTH, 1, 768)
    return pl.pallas_call(
        functools.partial(_hyena_lat_kernel, n1=n1, n2=n2),
        grid=(nch, bsz),
        in_specs=[ucol(0), ucol(1), ucol(2), wcol(0), wcol(1), wcol(2), bcol(0), bcol(1), bcol(2),
                  kspec, kspec, const(m1), const(w2), const(ma), const(mb),
                  pl.BlockSpec((None, 1, LANES), lambda c, b: (layer, 0, c)),
                  pl.BlockSpec((n, LANES), lambda c, b: (b, 2 * nch + c))],
        out_specs=pl.BlockSpec((n, LANES), lambda c, b: (b, c)),
        out_shape=jax.ShapeDtypeStruct((bsz * n, 256), BF16),
        scratch_shapes=[pltpu.VMEM((n1 // 2 * _pitch(n2), LANES), F32)] * 2
                       + [pltpu.VMEM((n2 * _pitch(n1), LANES), F32)] * 2,
        compiler_params=_cparams("arbitrary", "arbitrary"), name="hyena_lat",
    )(hu, hu, hu, conv_w, conv_w, conv_w, cb3, cb3, cb3, kr, ki, m1, w2, ma, mb,
      skip.reshape(DEPTH, 1, HY_CH), sg)


def _fnet_ctx_kernel(u_ref, cs_ref, dn_ref, fw_ref, fb_ref, sg_ref, o_ref):
    ab = _dot(u_ref[...], cs_ref[...])
    stack = jnp.concatenate([ab[:, :GROUP_W], ab[:, GROUP_W:]], axis=0).astype(BF16)
    f = _dot(dn_ref[...], stack)
    out = _bdot(f, fw_ref[...]) + fb_ref[...]
    o_ref[...] = (out * sg_ref[...].astype(F32)).astype(BF16)


def _fnet_lat_kernel(u_ref, cs_ref, f1_ref, f2_ref, fw_ref, fb_ref, sg_ref, o_ref,
                     a_scr, b_scr, s_re, s_im, o_scr, *, n1, n2):
    halves = GROUP_W // LANES

    def put(scr, rows, val):
        for hh in range(halves):
            scr[hh, rows, :] = val[:, hh * LANES:(hh + 1) * LANES]

    def get(scr, rows):
        return jnp.concatenate([scr[hh, rows, :] for hh in range(halves)], axis=1)

    def block(i, blk):
        return slice(i * _pitch(blk), i * _pitch(blk) + blk)

    ab = _dot(u_ref[...], cs_ref[...])
    for j1 in range(n1):
        put(a_scr, block(j1, n2), ab[j1 * n2:(j1 + 1) * n2, :GROUP_W])
        put(b_scr, block(j1, n2), ab[j1 * n2:(j1 + 1) * n2, GROUP_W:])
    for j2 in range(n2):
        rows = pl.ds(j2, n1, stride=_pitch(n2))
        g = jnp.concatenate([get(a_scr, rows), get(b_scr, rows)], axis=0)
        t = _dot(f1_ref[j2], g.astype(BF16))
        put(s_re, block(j2, n1), t[:n1])
        put(s_im, block(j2, n1), t[n1:])
    for k1 in range(n1):
        rows = pl.ds(k1, n2, stride=_pitch(n1))
        g = jnp.concatenate([get(s_re, rows), get(s_im, rows)], axis=0)
        put(o_scr, rows, _dot(f2_ref[...], g.astype(BF16)))
    fw = fw_ref[...].astype(BF16)
    fb = fb_ref[...]
    for k2 in range(n2):
        rows = slice(k2 * n1, (k2 + 1) * n1)
        out = _dot(get(o_scr, block(k2, n1)).astype(BF16), fw) + fb
        o_ref[rows, :] = (out * sg_ref[rows, :].astype(F32)).astype(BF16)


def _fnet_lat(fu, sg, fn_w, fn_b, cs, f1, f2, bsz, n, layer):
    n2 = FN_N2
    n1 = n // n2
    return pl.pallas_call(
        functools.partial(_fnet_lat_kernel, n1=n1, n2=n2), grid=(bsz,),
        in_specs=[pl.BlockSpec((n, 256), lambda i: (i, 0)),
                  pl.BlockSpec(cs.shape, lambda i: (0, 0)),
                  pl.BlockSpec(f1.shape, lambda i: (0, 0, 0)),
                  pl.BlockSpec(f2.shape, lambda i: (0, 0)),
                  pl.BlockSpec((None, GROUP_W, GROUP_W), lambda i: (layer, 0, 0)),
                  pl.BlockSpec((None, 1, GROUP_W), lambda i: (layer, 0, 0)),
                  pl.BlockSpec((n, 256), lambda i: (i, 3))],
        out_specs=pl.BlockSpec((n, 256), lambda i: (i, 0)),
        out_shape=jax.ShapeDtypeStruct((bsz * n, 256), BF16),
        scratch_shapes=[pltpu.VMEM((GROUP_W // LANES, n1 * _pitch(n2), LANES), F32)] * 2
                       + [pltpu.VMEM((GROUP_W // LANES, n2 * _pitch(n1), LANES), F32)] * 3,
        compiler_params=_cparams("arbitrary"), name="fnet_lat",
    )(fu, cs, f1, f2, fn_w, fn_b.reshape(DEPTH, 1, GROUP_W), sg)


def _ctx_mixers_kernel(sink_ref, aq_ref, akt_ref, av_ref, lam_ref, subln_ref, dq_ref, dkt_ref, dv_ref,
                       hu_ref, cw_ref, cb_ref, kr_ref, ki_ref, fwd_ref, inv_ref, skip_ref,
                       fu_ref, cs_ref, dn_ref, fw_ref, fb_ref, sg_ref, o_ref, s_scr, p_scr, *, layer, lam_init, kc):
    sg = [sg_ref.at[:, j * GROUP_W:(j + 1) * GROUP_W] for j in range(4)]
    out = [o_ref.at[:, j * GROUP_W:(j + 1) * GROUP_W] for j in range(4)]
    _attn_a_kernel(sink_ref, aq_ref, akt_ref, av_ref, sg[0], out[0], layer=layer)
    _diff_kernel(lam_ref, subln_ref, dq_ref, dkt_ref, dv_ref, sg[1], out[1], s_scr, p_scr,
                 lam_init=lam_init, npieces=1, tq_sub=dq_ref.shape[0], kc=kc)
    _hyena_ctx_kernel(hu_ref, cw_ref, cb_ref, kr_ref, ki_ref, fwd_ref, inv_ref, skip_ref, sg[2], out[2])
    _fnet_ctx_kernel(fu_ref, cs_ref, dn_ref, fw_ref, fb_ref, sg[3], out[3])


def _exit_kernel(x_ref, *refs, latent, tiles_per_b):
    mix_refs, (w_ref, g_ref, mod_ref, o_ref) = refs[:-4], refs[-4:]
    row = (pl.program_id(0) // tiles_per_b) if latent else CTX_ROW
    gate = mod_ref[pl.ds(row, 1), :][:, 2 * D_MODEL:]
    mixed = mix_refs[0][...] if len(mix_refs) == 1 else jnp.concatenate([r[...] for r in mix_refs], axis=1)
    y = _dot(mixed, w_ref[...])
    y = y * lax.rsqrt(jnp.mean(y * y, axis=-1, keepdims=True) + EPS) * g_ref[...]
    o_ref[...] = x_ref[...] + gate * y


def _layer_exit(x2d, outs, w_out_bf, g_post, mod_all, n, layer):
    t = x2d.shape[0]
    tm = ROW_TILE
    row = lambda i: (i, 0)
    return pl.pallas_call(
        functools.partial(_exit_kernel, latent=True, tiles_per_b=n // tm),
        grid=(t // tm,),
        in_specs=[pl.BlockSpec((tm, D_MODEL), row)] + [pl.BlockSpec((tm, o.shape[1]), row) for o in outs]
                 + [pl.BlockSpec((None, D_MODEL, D_MODEL), lambda i: (layer, 0, 0)),
                    pl.BlockSpec((None, 1, D_MODEL), lambda i: (layer, 0, 0)),
                    pl.BlockSpec((None, COND_ROWS, 3 * D_MODEL), lambda i: (layer, 0, 0))],
        out_specs=pl.BlockSpec((tm, D_MODEL), row),
        out_shape=jax.ShapeDtypeStruct((t, D_MODEL), F32),
        compiler_params=_cparams("arbitrary"),
        name="exit_latent",
    )(x2d, *outs, w_out_bf, g_post.reshape(DEPTH, 1, D_MODEL), mod_all)


def _ctx_layer_kernel(x_ref, mod_ref, gpre_ref, win_ref, *refs, layer, lam_init, kc, aliased):
    if aliased:
        refs = refs[4:]
    (sink_ref, lam_ref, subln_ref, cw_ref, cb_ref, kr_ref, ki_ref, fwd_ref, inv_ref, skip_ref, cs_ref, dn_ref,
     fw_ref, fb_ref, wout_ref, gpost_ref, xo_ref, *cache_refs,
     aq, akt, av, dq, dkt, dv, hu, fu, sg, mixed, s_scr, p_scr) = refs
    if not aliased:
        for r in cache_refs:
            for l in range(layer + 1, DEPTH):
                r[l] = jnp.zeros(r.shape[1:], r.dtype)
        cache_refs = [r.at[layer] for r in cache_refs]
    _entry_kernel(x_ref, mod_ref, gpre_ref, win_ref, aq, akt, av, dq, dkt, dv, hu, fu, sg, *cache_refs,
                  latent=False, tiles_per_b=1)
    _ctx_mixers_kernel(sink_ref, aq, akt, av, lam_ref, subln_ref, dq, dkt, dv, hu, cw_ref, cb_ref, kr_ref, ki_ref,
                       fwd_ref, inv_ref, skip_ref, fu, cs_ref, dn_ref, fw_ref, fb_ref, sg, mixed, s_scr, p_scr,
                       layer=layer, lam_init=lam_init, kc=kc)
    _exit_kernel(x_ref, mixed, wout_ref, gpost_ref, mod_ref, xo_ref, latent=False, tiles_per_b=1)


def _ctx_layer(x2d, caches, mod_all, g_pre, w_in_bf, sink, diff_lambda, diff_subln, conv_w, conv_b, skip, kr, ki,
               fwd_t, inv_t, fn_w, fn_b, cs, dn, w_out_bf, g_post, bsz, n, layer):
    lam_init = 0.8 - 0.6 * math.exp(-0.3 * layer)
    big_l = 2 * n
    aliased = caches is not None
    assert aliased == (layer > 0)
    rows = lambda c: pl.BlockSpec((n, c), lambda b: (b, 0))
    lay = lambda a, c: pl.BlockSpec((None, a, c), lambda b: (layer, 0, 0))
    const = lambda a: pl.BlockSpec(a.shape, lambda b: (0,) * a.ndim)
    widths = (ATT_KV_HEADS * ATT_HEAD_DIM, ATT_KV_HEADS * ATT_HEAD_DIM, 2 * DIF_HEADS * DIF_QK_DIM, DIF_HEADS * DIF_V_DIM)
    if aliased:
        cache_specs = [pl.BlockSpec((None, None, n, w), lambda b: (b, layer, 0, 0)) for w in widths]
    else:
        cache_specs = [pl.BlockSpec((None, DEPTH, n, w), lambda b: (b, 0, 0, 0)) for w in widths]
    nslot = DIF_HEADS * 2 + 1
    vm = lambda shape, dt=BF16: pltpu.VMEM(shape, dt)
    res = pl.pallas_call(
        functools.partial(_ctx_layer_kernel, layer=layer, lam_init=lam_init, kc=DIFF_KEY_CHUNK, aliased=aliased),
        grid=(bsz,),
        in_specs=[rows(D_MODEL), lay(COND_ROWS, 3 * D_MODEL), lay(1, D_MODEL), lay(D_MODEL, D_IN)]
                 + [pl.BlockSpec(memory_space=pl.ANY)] * (4 if aliased else 0)
                 + [pl.BlockSpec(memory_space=pltpu.SMEM),
                    lay(4, DIF_QK_DIM), lay(1, DIF_V_DIM), lay(3, 768), lay(1, 768), lay(big_l, HY_CH), lay(big_l, HY_CH),
                    const(fwd_t), const(inv_t), lay(1, HY_CH), const(cs), const(dn), lay(GROUP_W, GROUP_W),
                    lay(1, GROUP_W), lay(D_MODEL, D_MODEL), lay(1, D_MODEL)],
        out_specs=[rows(D_MODEL)] + cache_specs,
        out_shape=[jax.ShapeDtypeStruct(x2d.shape, F32)]
                  + [jax.ShapeDtypeStruct((bsz, DEPTH, n, w), F32) for w in widths],
        input_output_aliases={4 + j: 1 + j for j in range(4)} if aliased else {},
        scratch_shapes=[vm((n, 256)), vm((128, n)), vm((n, 128)), vm((n, 256)), vm((256, n)), vm((n, 512)),
                        vm((n, 768)), vm((n, 256)), vm((n, 1024)), vm((n, 1024)),
                        vm((nslot, n, n), F32), vm((nslot, n, n))],
        compiler_params=_cparams("arbitrary"), name="ctx_layer",
    )(x2d, mod_all, g_pre.reshape(DEPTH, 1, D_MODEL), w_in_bf, *(caches or ()), sink, diff_lambda,
      diff_subln.reshape(DEPTH, 1, DIF_V_DIM), conv_w, conv_b.reshape(DEPTH, 1, 768), kr, ki, fwd_t, inv_t,
      skip.reshape(DEPTH, 1, HY_CH), cs, dn, fn_w, fn_b.reshape(DEPTH, 1, GROUP_W), w_out_bf,
      g_post.reshape(DEPTH, 1, D_MODEL))
    return res[0], list(res[1:])


def kernel(x_prompt, x_sample, cache_attn_k, cache_attn_v, cache_diff_k, cache_diff_v, c, c_ctx, w_ada, b_ada, norm_pre, norm_post, w_in, w_out, attn_sink, diff_lambda, diff_subln, hy_conv_w, hy_conv_b, hy_filt_w1, hy_filt_b1, hy_filt_w2, hy_filt_b2, hy_filt_w3, hy_filt_freq, hy_skip, fn_w, fn_b):
    bp, lp, _ = x_prompt.shape
    bs, ls, _ = x_sample.shape
    past = cache_attn_k.shape[2]
    assert bs < CTX_ROW + 1 <= COND_ROWS

    cond = jnp.concatenate([c, c_ctx[None, :], jnp.zeros((COND_ROWS - bs - 1, D_MODEL), F32)], axis=0)
    mod_all = _modulation(cond, w_ada, b_ada)
    w_in_bf = w_in.astype(BF16)
    w_out_bf = w_out.astype(BF16)
    sink = attn_sink.reshape(DEPTH * ATT_HEADS)

    rope = _rope_tables(ls, ATT_HEAD_DIM) + _rope_tables(ls, DIF_QK_DIM)
    filt = (jnp.pad(hy_filt_w1, ((0, 0), (0, LANES - FILT_EMB), (0, 0))), hy_filt_b1, hy_filt_w2, hy_filt_b2,
            hy_filt_w3, hy_filt_freq)
    cos_c, sin_c, fwd_c, inv_c = _dense_conv_tables(lp)
    kr_c, ki_c = _hyena_filters_ctx(lp, filt, cos_c, sin_c)
    hy_tables = _hyena_fft_tables(ls)
    hf_l, hb_l = _hyena_filters_lat(ls, filt)
    kr_l, ki_l = _filter_fft(hf_l, hb_l, hy_tables[0], hy_tables[1], ls)
    cs_c, dn_c = _fnet_channel_table(lp), _fnet_dense_table(lp)
    cs_l = _fnet_channel_table(ls)
    f1_l, f2_l = _fnet_fft_tables(ls)

    cak = cache_attn_k.reshape(bs, DEPTH, past, 128).transpose(0, 1, 3, 2).astype(BF16)
    cav = cache_attn_v.reshape(bs, DEPTH, past, 128).astype(BF16)
    cdk = cache_diff_k.reshape(bs, DEPTH, past, 256).transpose(0, 1, 3, 2).astype(BF16)
    cdv = jnp.concatenate([cache_diff_v, jnp.ones_like(cache_diff_v)], axis=-1)
    cdv = cdv.reshape(bs, DEPTH, past, 2 * GROUP_W).astype(BF16)

    xp = x_prompt.reshape(bp * lp, D_MODEL)
    xs = x_sample.reshape(bs * ls, D_MODEL)
    new_caches = None
    for l in range(DEPTH):
        xp, new_caches = _ctx_layer(xp, new_caches, mod_all, norm_pre, w_in_bf, sink, diff_lambda, diff_subln,
                                    hy_conv_w, hy_conv_b, hy_skip, kr_c, ki_c, fwd_c, inv_c, fn_w, fn_b, cs_c, dn_c,
                                    w_out_bf, norm_post, bp, lp, l)

        (aq, akt, av, dq, dkt, dv, hu, fu, sg) = _layer_entry(xs, bs, ls, mod_all, norm_pre, w_in_bf, l, rope)
        outs = (_attn_a_latent(aq, akt, av, cak[:, l], cav[:, l], sg, sink, bs, ls, l),
                _diff_attention(dq, [(dkt, dv), (cdk[:, l], cdv[:, l])], sg, diff_lambda, diff_subln, bs, ls, l),
                _hyena_lat(hu, sg, hy_conv_w, hy_conv_b, hy_skip, kr_l, ki_l, hy_tables, bs, ls, l),
                _fnet_lat(fu, sg, fn_w, fn_b, cs_l, f1_l, f2_l, bs, ls, l))
        xs = _layer_exit(xs, outs, w_out_bf, norm_post, mod_all, ls, l)

    nak, nav, ndk, ndv = new_caches
    return (xp.reshape(bp, lp, D_MODEL), xs.reshape(bs, ls, D_MODEL),
            nak.reshape(bp, DEPTH, lp, ATT_KV_HEADS, ATT_HEAD_DIM), nav.reshape(bp, DEPTH, lp, ATT_KV_HEADS, ATT_HEAD_DIM),
            ndk.reshape(bp, DEPTH, lp, 2, DIF_HEADS, DIF_QK_DIM), ndv.reshape(bp, DEPTH, lp, DIF_HEADS, DIF_V_DIM))
```

```python
import functools
import math

import numpy as np
import jax
import jax.numpy as jnp
from jax import lax
from jax.experimental import pallas as pl
from jax.experimental.pallas import tpu as pltpu

F32 = jnp.float32
BF16 = jnp.bfloat16

D_MODEL = 1024
DEPTH = 2
GRID_W = 64
GROUP_W = 256
ATT_HEADS = 4
ATT_KV_HEADS = 2
ATT_HEAD_DIM = 64
WINDOW = 128
BLOCK = 128
DIF_HEADS = 4
DIF_V_DIM = 64
DIF_QK_DIM = 32
HY_CH = 256
FILT_BANDS = 16
FILT_EMB = 1 + 2 * FILT_BANDS
FILT_HIDDEN = 64
HY_MIN_DECAY = math.log(1e-2) / 1.5
HY_MAX_DECAY = math.log(1e-2) / 0.3
FN_GROUP_CH = 64
ROPE_BASE = 10000.0
EPS = 1e-6
NEG_INF = -1e30

C_AQ, C_AK, C_AV, C_AG = 0, 256, 384, 512
C_DQ, C_DK, C_DV, C_DG = 768, 1024, 1280, 1536
C_HU, C_HG, C_FU, C_FG, D_IN = 1792, 2560, 2816, 3072, 3328
LOG2_E = math.log2(math.e)

LANES = 128
COND_ROWS = 8
CTX_ROW = 4
VMEM_LIMIT = 56 * 1024 * 1024
PITCH_PAD = 8

ROW_TILE = 1024
EXIT_ROW_TILE = 2048
ATTN_A_QBLOCKS = 8
DIFF_Q_ROWS = 256
DIFF_KEY_CHUNK = 512

HY_N2 = 64
FN_N2 = 64


def _cparams(*sem):
    return pltpu.CompilerParams(dimension_semantics=sem, vmem_limit_bytes=VMEM_LIMIT)


def _dot(a, b):
    return jnp.dot(a, b, preferred_element_type=F32)


def _bdot(a, b):
    return jnp.dot(a.astype(BF16), b.astype(BF16), preferred_element_type=F32)


def _dot3(a, b):
    ah = a.astype(BF16)
    al = (a - ah.astype(F32)).astype(BF16)
    bh = b.astype(BF16)
    bl = (b - bh.astype(F32)).astype(BF16)
    return _dot(ah, bh) + _dot(ah, bl) + _dot(al, bh)


def _bf16_table(a):
    return jnp.asarray(a, F32).astype(BF16)


def _rope_tables(n, head_dim):
    pos = np.arange(n)
    row = (pos // GRID_W).astype(np.float64)
    col = (pos % GRID_W).astype(np.float64)
    n_freq = head_dim // 4
    inv = ROPE_BASE ** (-np.arange(n_freq, dtype=np.float64) / n_freq)
    ang = np.concatenate([row[:, None] * inv, col[:, None] * inv], axis=-1)
    reps = LANES // head_dim
    cos = np.tile(np.concatenate([np.cos(ang), np.cos(ang)], axis=-1), (1, reps))
    sin = np.tile(np.concatenate([-np.sin(ang), np.sin(ang)], axis=-1), (1, reps))
    return jnp.asarray(cos, F32), jnp.asarray(sin, F32)


def _filter_feats(n):
    t = np.linspace(0.0, 1.0, n)[:, None]
    w = (2.0 * math.pi / n) * np.arange(n)[:, None]
    f = np.linspace(1e-4, FILT_BANDS - 1, FILT_BANDS)[None, :]
    feats = np.concatenate([t, np.cos(f * w), -np.sin(f * w)], axis=-1)
    feats = np.pad(feats, ((0, 0), (0, LANES - FILT_EMB)))
    feats = np.concatenate([feats[:n // 2], feats[n // 2:]], axis=1)
    deltas = np.abs(np.linspace(HY_MIN_DECAY, HY_MAX_DECAY, HY_CH))
    decay = np.exp(-t * deltas[None, :])
    return jnp.asarray(feats, F32), jnp.asarray(decay, F32)


def _dense_conv_tables(n):
    big_l = 2 * n
    k = np.arange(big_l)[:, None]
    t = np.arange(n)[None, :]
    th = 2.0 * math.pi * k * t / big_l
    fwd = np.concatenate([np.cos(th), -np.sin(th)], axis=0)
    inv = np.concatenate([np.cos(th).T, -np.sin(th).T], axis=1) / big_l
    return (jnp.asarray(np.cos(th), F32), jnp.asarray(np.sin(th), F32), _bf16_table(fwd), _bf16_table(inv))


def _hyena_fft_tables(n):
    big_l = 2 * n
    n2 = HY_N2
    n1 = big_l // n2
    n1h = n1 // 2
    k1 = np.arange(n1)
    j1 = np.arange(n1h)
    j2 = np.arange(n2)
    k2 = np.arange(n2)
    th = 2.0 * math.pi * k1[None, :, None] * (j1[None, None, :] * n2 + j2[:, None, None]) / big_l
    m1 = np.concatenate([np.cos(th), -np.sin(th)], axis=1)
    th2 = 2.0 * math.pi * k2[:, None] * j2[None, :] / n2
    c2, s2 = np.cos(th2), np.sin(th2)
    w2 = np.block([[c2, s2], [-s2, c2]])
    tha = 2.0 * math.pi * j2[None, :, None] * (k1[:, None, None] + n1 * k2[None, None, :]) / big_l
    ca, sa = np.cos(tha), np.sin(tha)
    ma = np.concatenate([np.concatenate([ca, -sa], axis=2), np.concatenate([sa, ca], axis=2)], axis=1)
    thb = 2.0 * math.pi * j1[:, None] * k1[None, :] / n1
    mb = np.concatenate([np.cos(thb), -np.sin(thb)], axis=1) / big_l
    m1 = np.concatenate([m1[0::2], m1[1::2]], axis=2)
    ma = np.concatenate([ma[0::2], ma[1::2]], axis=2)
    return _bf16_table(m1), _bf16_table(w2), _bf16_table(ma), _bf16_table(mb)


def _fnet_channel_table(n):
    m = np.arange(FN_GROUP_CH)
    th = 2.0 * math.pi * m[:, None] * m[None, :] / FN_GROUP_CH
    eye = np.eye(GROUP_W // FN_GROUP_CH)
    sc = 1.0 / math.sqrt(n * FN_GROUP_CH)
    cs = np.concatenate([np.kron(eye, np.cos(th)), np.kron(eye, np.sin(th))], axis=1) * sc
    return _bf16_table(cs)


def _fnet_dense_table(n):
    k = np.arange(n)
    th = 2.0 * math.pi * k[:, None] * k[None, :] / n
    return _bf16_table(np.concatenate([np.cos(th), -np.sin(th)], axis=1))


def _fnet_fft_tables(n):
    n2 = FN_N2
    n1 = n // n2
    k1 = np.arange(n1)
    j1 = np.arange(n1)
    j2 = np.arange(n2)
    k2 = np.arange(n2)
    th = 2.0 * math.pi * k1[None, :, None] * (j1[None, None, :] * n2 + j2[:, None, None]) / n
    c, s = np.cos(th), np.sin(th)
    f1 = np.concatenate([np.concatenate([c, -s], axis=2), np.concatenate([s, c], axis=2)], axis=1)
    th2 = 2.0 * math.pi * k2[:, None] * j2[None, :] / n2
    f2 = np.concatenate([np.cos(th2), -np.sin(th2)], axis=1)
    return _bf16_table(f1), _bf16_table(f2)


def _mod_kernel(cond_ref, w_ref, b_ref, o_ref):
    c = cond_ref[...]
    s = c * jax.nn.sigmoid(c)
    o_ref[...] = _dot3(s, w_ref[...]) + b_ref[...]


def _modulation(cond, w_ada, b_ada):
    tn = 1024
    return pl.pallas_call(
        _mod_kernel,
        grid=(DEPTH, 3 * D_MODEL // tn),
        in_specs=[pl.BlockSpec((COND_ROWS, D_MODEL), lambda l, j: (0, 0)),
                  pl.BlockSpec((None, D_MODEL, tn), lambda l, j: (l, 0, j)),
                  pl.BlockSpec((None, 1, tn), lambda l, j: (l, 0, j))],
        out_specs=pl.BlockSpec((None, COND_ROWS, tn), lambda l, j: (l, 0, j)),
        out_shape=jax.ShapeDtypeStruct((DEPTH, COND_ROWS, 3 * D_MODEL), F32),
        compiler_params=_cparams("arbitrary", "arbitrary"),
        name="modulation",
    )(cond, w_ada, b_ada.reshape(DEPTH, 1, 3 * D_MODEL))


def _rope(x, cos, sin, half):
    lane = lax.broadcasted_iota(jnp.int32, x.shape, 1)
    first = (lane % (2 * half)) < half
    partner = jnp.where(first, pltpu.roll(x, LANES - half, 1), pltpu.roll(x, half, 1))
    return x * cos + partner * sin


def _entry_kernel(*refs, latent, tiles_per_b):
    if latent:
        (x_ref, mod_ref, g_ref, w_ref, ca_ref, sa_ref, cd_ref, sd_ref,
         aq_ref, akt_ref, av_ref, dq_ref, dkt_ref, dv_ref, hu_ref, fu_ref, sg_ref) = refs
        row = pl.program_id(0) // tiles_per_b
    else:
        (x_ref, mod_ref, g_ref, w_ref,
         aq_ref, akt_ref, av_ref, dq_ref, dkt_ref, dv_ref, hu_ref, fu_ref, sg_ref,
         ak32_ref, av32_ref, dk32_ref, dv32_ref) = refs
        row = CTX_ROW
    m = mod_ref[pl.ds(row, 1), :]
    shift, scale = m[:, :D_MODEL], m[:, D_MODEL:2 * D_MODEL]
    x = x_ref[...]
    h = x * lax.rsqrt(jnp.mean(x * x, axis=-1, keepdims=True) + EPS) * g_ref[...]
    hb = (h * (1.0 + scale) + shift).astype(BF16)

    def proj(c0, c1):
        return _dot(hb, w_ref[:, c0:c1])

    def roped(p, cos_ref, sin_ref, half):
        if not latent:
            return p
        cos, sin = cos_ref[...], sin_ref[...]
        chunks = [_rope(p[:, j:j + LANES], cos, sin, half) for j in range(0, p.shape[1], LANES)]
        return chunks[0] if len(chunks) == 1 else jnp.concatenate(chunks, axis=1)

    ca = sa = cd = sd = None
    if latent:
        ca, sa, cd, sd = ca_ref, sa_ref, cd_ref, sd_ref

    aq_ref[...] = (roped(proj(C_AQ, C_AK), ca, sa, ATT_HEAD_DIM // 2) * (ATT_HEAD_DIM ** -0.5 * LOG2_E)).astype(BF16)
    akv = proj(C_AK, C_AG)
    ak = roped(akv[:, :C_AV - C_AK], ca, sa, ATT_HEAD_DIM // 2)
    akt_ref[...] = ak.T.astype(BF16)
    av = akv[:, C_AV - C_AK:]
    av_ref[...] = av.astype(BF16)
    dq_ref[...] = (roped(proj(C_DQ, C_DK), cd, sd, DIF_QK_DIM // 2) * (DIF_QK_DIM ** -0.5 * LOG2_E)).astype(BF16)
    dk = roped(proj(C_DK, C_DV), cd, sd, DIF_QK_DIM // 2)
    dkt_ref[...] = dk.T.astype(BF16)
    dv = proj(C_DV, C_DG)
    value_lane = lax.broadcasted_iota(jnp.int32, (dv.shape[0], LANES), 1) < DIF_V_DIM
    for h in range(DIF_HEADS):
        pair = dv[:, (h // 2) * LANES:(h // 2 + 1) * LANES]
        if h % 2:
            pair = pltpu.roll(pair, DIF_V_DIM, 1)
        dv_ref[:, h * LANES:(h + 1) * LANES] = jnp.where(value_lane, pair, 1.0).astype(BF16)
    if not latent:
        ak32_ref[...] = ak
        av32_ref[...] = av
        dk32_ref[...] = dk
        dv32_ref[...] = dv
    hu_ref[...] = proj(C_HU, C_HG).astype(BF16)
    fu_ref[...] = proj(C_FU, C_FG).astype(BF16)
    for j, c0 in enumerate((C_AG, C_DG, C_HG, C_FG)):
        g = proj(c0, c0 + GROUP_W)
        sg_ref[:, j * GROUP_W:(j + 1) * GROUP_W] = (g * jax.nn.sigmoid(g)).astype(BF16)


def _layer_entry(x2d, bsz, n, mod_all, g_pre, w_in_bf, layer, rope):
    t = bsz * n
    tm = ROW_TILE
    tiles_per_b = n // tm
    row = lambda i: (i, 0)
    pos = lambda i: (i % tiles_per_b, 0)
    kt_map = lambda i: (i // tiles_per_b, 0, i % tiles_per_b)
    in_specs = [pl.BlockSpec((tm, D_MODEL), row),
                pl.BlockSpec((None, COND_ROWS, 3 * D_MODEL), lambda i: (layer, 0, 0)),
                pl.BlockSpec((None, 1, D_MODEL), lambda i: (layer, 0, 0)),
                pl.BlockSpec((None, D_MODEL, D_IN), lambda i: (layer, 0, 0))] + [pl.BlockSpec((tm, LANES), pos)] * 4
    bf = lambda c: jax.ShapeDtypeStruct((t, c), BF16)
    rs = lambda c: pl.BlockSpec((tm, c), row)
    out_shape = [bf(256), jax.ShapeDtypeStruct((bsz, 128, n), BF16), bf(128),
                 bf(256), jax.ShapeDtypeStruct((bsz, 256, n), BF16), bf(512),
                 bf(768), bf(256), bf(1024)]
    out_specs = [rs(256), pl.BlockSpec((None, 128, tm), kt_map), rs(128),
                 rs(256), pl.BlockSpec((None, 256, tm), kt_map), rs(512),
                 rs(768), rs(256), rs(1024)]
    return pl.pallas_call(
        functools.partial(_entry_kernel, latent=True, tiles_per_b=tiles_per_b),
        grid=(t // tm,), in_specs=in_specs, out_specs=out_specs, out_shape=out_shape,
        compiler_params=_cparams("arbitrary"), name="entry_latent",
    )(x2d, mod_all, g_pre.reshape(DEPTH, 1, D_MODEL), w_in_bf, *rope)


def _sink_columns(sink_ref, layer, tq):
    grp = ATT_HEADS // ATT_KV_HEADS
    head_row = lax.broadcasted_iota(jnp.int32, (grp * tq, 1), 0) // tq
    cols = []
    for g in range(ATT_KV_HEADS):
        col = jnp.zeros((grp * tq, 1), F32)
        for j in range(grp):
            col = jnp.where(head_row == j, sink_ref[layer * ATT_HEADS + g * grp + j] * LOG2_E, col)
        cols.append(col)
    return cols


def _stack_heads(q_ref, rows, g):
    grp = ATT_HEADS // ATT_KV_HEADS
    return jnp.concatenate([q_ref[rows, (g * grp + t) * ATT_HEAD_DIM:(g * grp + t + 1) * ATT_HEAD_DIM]
                            for t in range(grp)], axis=0)


def _attn_a_kernel(sink_ref, q_ref, k_ref, v_ref, sg_ref, o_ref, *, layer):
    tq = q_ref.shape[0]
    grp = ATT_HEADS // ATT_KV_HEADS
    sink_cols = _sink_columns(sink_ref, layer, tq)
    outs = []
    for g in range(ATT_KV_HEADS):
        d0 = g * ATT_HEAD_DIM
        s = _dot(_stack_heads(q_ref, slice(None), g), k_ref[d0:d0 + ATT_HEAD_DIM, :])
        m = jnp.maximum(jnp.max(s, axis=-1, keepdims=True), sink_cols[g])
        p = jnp.exp2(s - m)
        l = jnp.sum(p, axis=-1, keepdims=True) + jnp.exp2(sink_cols[g] - m)
        o = _dot(p.astype(BF16), v_ref[:, d0:d0 + ATT_HEAD_DIM]) / l
        outs += [o[j * tq:(j + 1) * tq] for j in range(grp)]
    o_full = jnp.concatenate(outs, axis=1)
    o_ref[...] = (o_full * sg_ref[...].astype(F32)).astype(BF16)


def _attn_a_win_kernel(sink_ref, q_ref, kp_ref, kc_ref, kn_ref, vp_ref, vc_ref, vn_ref, ck_ref, cv_ref, sg_ref,
                       o_ref, *, layer, nsteps, qb):
    i = pl.program_id(1)
    grp = ATT_HEADS // ATT_KV_HEADS
    rows = grp * BLOCK
    kw = jnp.concatenate([kp_ref[...], kc_ref[...], kn_ref[...]], axis=1)
    vw = jnp.concatenate([vp_ref[...], vc_ref[...], vn_ref[...]], axis=0)
    r = lax.broadcasted_iota(jnp.int32, (rows, 3 * BLOCK), 0) % BLOCK
    c = lax.broadcasted_iota(jnp.int32, (rows, 3 * BLOCK), 1)
    band = (c >= r) & (c <= r + 2 * WINDOW)
    masks = {0: band & ((c >= BLOCK) | (i > 0)), qb - 1: band & ((c < 2 * BLOCK) | (i < nsteps - 1))}
    sink_cols = _sink_columns(sink_ref, layer, BLOCK)
    ck = ck_ref[...]
    cv = cv_ref[...]
    items = [(j, g) for j in range(qb) for g in range(ATT_KV_HEADS)]
    st_a, st_b, outs = {}, {}, {}
    for step in range(len(items) + 2):
        a, b, cc = step, step - 1, step - 2
        if a < len(items):
            j, g = items[a]
            d0 = g * ATT_HEAD_DIM
            q2 = _stack_heads(q_ref, slice(j * BLOCK, (j + 1) * BLOCK), g)
            s_w = _dot(q2, kw[d0:d0 + ATT_HEAD_DIM, j * BLOCK:(j + 3) * BLOCK])
            s_w = jnp.where(masks.get(j, band), s_w, NEG_INF)
            s_c = _dot(q2, ck[d0:d0 + ATT_HEAD_DIM, :])
            m = jnp.maximum(jnp.maximum(jnp.max(s_w, axis=-1, keepdims=True),
                                        jnp.max(s_c, axis=-1, keepdims=True)), sink_cols[g])
            st_a[a] = (s_w, s_c, m)
        if 0 <= b < len(items):
            s_w, s_c, m = st_a.pop(b)
            p_w = jnp.exp2(s_w - m)
            p_c = jnp.exp2(s_c - m)
            l = (jnp.sum(p_w, axis=-1, keepdims=True) + jnp.sum(p_c, axis=-1, keepdims=True)
                 + jnp.exp2(sink_cols[items[b][1]] - m))
            st_b[b] = (p_w.astype(BF16), p_c.astype(BF16), l)
        if 0 <= cc < len(items):
            j, g = items[cc]
            d0 = g * ATT_HEAD_DIM
            p_w, p_c, l = st_b.pop(cc)
            o = (_dot(p_w, vw[j * BLOCK:(j + 3) * BLOCK, d0:d0 + ATT_HEAD_DIM])
                 + _dot(p_c, cv[:, d0:d0 + ATT_HEAD_DIM])) / l
            outs.setdefault(j, []).extend([o[t * BLOCK:(t + 1) * BLOCK] for t in range(grp)])
            if g == ATT_KV_HEADS - 1:
                rs = slice(j * BLOCK, (j + 1) * BLOCK)
                o_full = jnp.concatenate(outs.pop(j), axis=1)
                o_ref[rs, :] = (o_full * sg_ref[rs, :].astype(F32)).astype(BF16)


def _attn_a_latent(aq, akt, av, ck_t, cv, sg, sink, bsz, n, layer):
    qb = ATTN_A_QBLOCKS
    nb = n // BLOCK
    nsteps = nb // qb
    qmap = lambda b, i: (b * nsteps + i, 0)
    prev = lambda i: jnp.maximum(qb * i - 1, 0)
    nxt = lambda i: jnp.minimum(qb * i + qb, nb - 1)
    kedge = lambda f: pl.BlockSpec((None, 128, BLOCK), lambda b, i: (b, 0, f(i)))
    vedge = lambda f: pl.BlockSpec((BLOCK, 128), lambda b, i: (b * nb + f(i), 0))
    return pl.pallas_call(
        functools.partial(_attn_a_win_kernel, layer=layer, nsteps=nsteps, qb=qb),
        grid=(bsz, nsteps),
        in_specs=[pl.BlockSpec(memory_space=pltpu.SMEM),
                  pl.BlockSpec((qb * BLOCK, 256), qmap),
                  kedge(prev), pl.BlockSpec((None, 128, qb * BLOCK), lambda b, i: (b, 0, i)), kedge(nxt),
                  vedge(prev), pl.BlockSpec((qb * BLOCK, 128), qmap), vedge(nxt),
                  pl.BlockSpec((None, 128, ck_t.shape[2]), lambda b, i: (b, 0, 0)),
                  pl.BlockSpec((None, cv.shape[1], 128), lambda b, i: (b, 0, 0)),
                  pl.BlockSpec((qb * BLOCK, 256), qmap)],
        out_specs=pl.BlockSpec((qb * BLOCK, 256), qmap),
        out_shape=jax.ShapeDtypeStruct((bsz * n, 256), BF16),
        compiler_params=_cparams("arbitrary", "arbitrary"),
        name="attn_a_latent",
    )(sink, aq, akt, akt, akt, av, av, av, ck_t, cv, sg)


def _diff_kernel(lam_ref, subln_ref, q_ref, *refs, lam_init, npieces, tq_sub, kc):
    kts = refs[0:2 * npieces:2]
    vs = refs[1:2 * npieces:2]
    sg_ref, o_ref, s_scr, p_scr = refs[2 * npieces:2 * npieces + 4]
    lp = lam_ref[...]
    lam = (jnp.exp(jnp.sum(lp[0:1] * lp[1:2], axis=-1, keepdims=True))
           - jnp.exp(jnp.sum(lp[2:3] * lp[3:4], axis=-1, keepdims=True)) + lam_init)
    subln = subln_ref[...]
    tq = q_ref.shape[0]
    chunks, off = [], 0
    for kt, v in zip(kts, vs):
        for c0 in range(0, kt.shape[1], kc):
            w = min(kc, kt.shape[1] - c0)
            chunks.append((kt, v, c0, w, off))
            off += w
    items = [(r0, h, mp) for r0 in range(0, tq, tq_sub) for h in range(DIF_HEADS) for mp in range(2)]
    n_items = len(items)
    mx, acc, om, outs = {}, {}, {}, {}

    def fold(s):
        return functools.reduce(jnp.maximum, [s[:, j:j + LANES] for j in range(0, s.shape[1], LANES)])

    nslot = s_scr.shape[0]
    lag = nslot - 1
    for step in range(n_items + 2 * lag):
        a = step if step < n_items else None
        b = step - lag if 0 <= step - lag < n_items else None
        c = step - 2 * lag if 0 <= step - 2 * lag < n_items else None
        if a is not None:
            r0, h, mp = items[a]
            f0 = (mp * DIF_HEADS + h) * DIF_QK_DIM
            q_a = q_ref[r0:r0 + tq_sub, f0:f0 + DIF_QK_DIM]
            mx[a] = jnp.full((tq_sub, LANES), NEG_INF, F32)
        if b is not None:
            m_b = jnp.max(mx.pop(b), axis=-1, keepdims=True)
        if c is not None:
            acc[c] = jnp.zeros((tq_sub, 2 * DIF_V_DIM), F32)
            hc = items[c][1]
        for kt, v, c0, w, o0 in chunks:
            if a is not None:
                s = _dot(q_a, kt[f0:f0 + DIF_QK_DIM, c0:c0 + w])
                s_scr[a % nslot, :, o0:o0 + w] = s
                mx[a] = jnp.maximum(mx[a], fold(s))
            if b is not None:
                p_scr[b % nslot, :, o0:o0 + w] = jnp.exp2(s_scr[b % nslot, :, o0:o0 + w] - m_b).astype(BF16)
            if c is not None:
                acc[c] = acc[c] + _dot(p_scr[c % nslot, :, o0:o0 + w],
                                       v[c0:c0 + w, 2 * DIF_V_DIM * hc:2 * DIF_V_DIM * (hc + 1)])
        if c is not None:
            r0, h, mp = items[c]
            o = acc.pop(c)
            om[(r0, h, mp)] = o[:, :DIF_V_DIM] / o[:, DIF_V_DIM:DIF_V_DIM + 1]
            if mp == 1:
                av = om.pop((r0, h, 0)) - lam * om.pop((r0, h, 1))
                y = av * lax.rsqrt(jnp.mean(av * av, axis=-1, keepdims=True) + EPS) * subln * (1.0 - lam_init)
                outs.setdefault(r0, []).append(y)
                if h == DIF_HEADS - 1:
                    o_full = jnp.concatenate(outs.pop(r0), axis=1)
                    o_ref[r0:r0 + tq_sub, :] = (o_full * sg_ref[r0:r0 + tq_sub, :].astype(F32)).astype(BF16)


def _diff_attention(dq, pieces, sg, diff_lambda, diff_subln, bsz, n, layer):
    tq, kc = DIFF_Q_ROWS, DIFF_KEY_CHUNK
    lam_init = 0.8 - 0.6 * math.exp(-0.3 * layer)
    nt = n // tq
    nk_all = sum(kt.shape[2] for kt, _ in pieces)
    assert nk_all > kc
    nslot = 2
    in_specs = [pl.BlockSpec((None, 4, DIF_QK_DIM), lambda b, i: (layer, 0, 0)),
                pl.BlockSpec((None, 1, DIF_V_DIM), lambda b, i: (layer, 0, 0)),
                pl.BlockSpec((tq, 256), lambda b, i: (b * nt + i, 0))]
    args = [diff_lambda, diff_subln.reshape(DEPTH, 1, DIF_V_DIM), dq]
    for kt, v in pieces:
        nk = kt.shape[2]
        in_specs.append(pl.BlockSpec((None, 256, nk), lambda b, i: (b, 0, 0)))
        if v.ndim == 3:
            in_specs.append(pl.BlockSpec((None, nk, 512), lambda b, i: (b, 0, 0)))
        else:
            in_specs.append(pl.BlockSpec((nk, 512), lambda b, i: (b, 0)))
        args += [kt, v]
    in_specs.append(pl.BlockSpec((tq, 256), lambda b, i: (b * nt + i, 1)))
    args.append(sg)
    return pl.pallas_call(
        functools.partial(_diff_kernel, lam_init=lam_init, npieces=len(pieces), tq_sub=tq, kc=kc),
        grid=(bsz, nt), in_specs=in_specs,
        out_specs=pl.BlockSpec((tq, 256), lambda b, i: (b * nt + i, 0)),
        out_shape=jax.ShapeDtypeStruct((bsz * n, 256), BF16),
        scratch_shapes=[pltpu.VMEM((nslot, tq, nk_all), F32), pltpu.VMEM((nslot, tq, nk_all), BF16)],
        compiler_params=_cparams("arbitrary", "arbitrary"),
        name="diff_attn_latent",
    )(*args)


def _filter_core(feats_ref, w1_ref, b1_ref, w2_ref, b2_ref, w3_ref, fr_ref, decay_ref):
    fr = fr_ref[...]
    h = jnp.sin(fr * (_dot3(feats_ref[...], w1_ref[...]) + b1_ref[...]))
    h = jnp.sin(fr * (_dot3(h, w2_ref[...]) + b2_ref[...]))
    h = _dot3(h, w3_ref[...])
    h = jnp.concatenate([h[:, :2 * HY_CH], h[:, 2 * HY_CH:]], axis=0)
    decay = decay_ref[...]
    hf = h[:, :HY_CH] * decay
    hb = h[:, HY_CH:] * decay
    tot = (jnp.sum(jnp.abs(hf), axis=0, keepdims=True) + jnp.sum(jnp.abs(hb), axis=0, keepdims=True) + EPS)
    hf = hf / tot
    hb = hb / tot
    rowi = lax.broadcasted_iota(jnp.int32, hb.shape, 0)
    return hf, jnp.where(rowi == 0, 0.0, hb)


def _filter_ctx_kernel(feats_ref, w1_ref, b1_ref, w2_ref, b2_ref, w3_ref, fr_ref, decay_ref,
                       cos_ref, sin_ref, kr_ref, ki_ref):
    hf, hb0 = _filter_core(feats_ref, w1_ref, b1_ref, w2_ref, b2_ref, w3_ref, fr_ref, decay_ref)
    kr_ref[...] = _dot3(cos_ref[...], hf + hb0)
    ki_ref[...] = _dot3(sin_ref[...], hb0 - hf)


def _filter_lat_kernel(feats_ref, w1_ref, b1_ref, w2_ref, b2_ref, w3_ref, fr_ref, decay_ref, hf_ref, hb_ref):
    hf, hb0 = _filter_core(feats_ref, w1_ref, b1_ref, w2_ref, b2_ref, w3_ref, fr_ref, decay_ref)
    hf_ref[...] = hf
    hb_ref[...] = hb0


def _filter_specs(n, w1p, b1, w2, b2, w3, freq, feats, decay):
    const = lambda shape: pl.BlockSpec(shape, lambda l: (0,) * len(shape))
    lay = lambda a, b: pl.BlockSpec((None, a, b), lambda l: (l, 0, 0))
    hid = 2 * FILT_HIDDEN
    in_specs = [const((n // 2, 2 * LANES)), lay(2 * LANES, hid), lay(1, hid), lay(hid, hid), lay(1, hid),
                lay(hid, 4 * HY_CH), lay(1, hid), const((n, HY_CH))]
    row2 = lambda a: jnp.tile(a.reshape(DEPTH, 1, FILT_HIDDEN), (1, 1, 2))
    args = [feats, _block_diag2(w1p), row2(b1), _block_diag2(w2), row2(b2), _block_diag2(w3), row2(freq), decay]
    return in_specs, args


def _block_diag2(w):
    z = jnp.zeros_like(w)
    return jnp.concatenate([jnp.concatenate([w, z], axis=2), jnp.concatenate([z, w], axis=2)], axis=1)


def _hyena_filters_ctx(n, filt, cos_t, sin_t):
    feats, decay = _filter_feats(n)
    in_specs, args = _filter_specs(n, *filt, feats, decay)
    big_l = 2 * n
    in_specs += [pl.BlockSpec((big_l, n), lambda l: (0, 0))] * 2
    out = pl.BlockSpec((None, big_l, HY_CH), lambda l: (l, 0, 0))
    return pl.pallas_call(
        _filter_ctx_kernel, grid=(DEPTH,), in_specs=in_specs, out_specs=[out, out],
        out_shape=[jax.ShapeDtypeStruct((DEPTH, big_l, HY_CH), F32)] * 2,
        compiler_params=_cparams("arbitrary"), name="hyena_filter_ctx",
    )(*args, cos_t, sin_t)


def _hyena_filters_lat(n, filt):
    feats, decay = _filter_feats(n)
    in_specs, args = _filter_specs(n, *filt, feats, decay)
    out = pl.BlockSpec((None, n, HY_CH), lambda l: (l, 0, 0))
    return pl.pallas_call(
        _filter_lat_kernel, grid=(DEPTH,), in_specs=in_specs, out_specs=[out, out],
        out_shape=[jax.ShapeDtypeStruct((DEPTH, n, HY_CH), F32)] * 2,
        compiler_params=_cparams("arbitrary"), name="hyena_filter_lat",
    )(*args)


def _short_conv(u_ref, cw_ref, cb_ref):
    u = u_ref[...].astype(F32)
    n = u.shape[0]
    rowi = lax.broadcasted_iota(jnp.int32, u.shape, 0)
    up = jnp.where(rowi == 0, 0.0, pltpu.roll(u, 1, 0))
    un = jnp.where(rowi == n - 1, 0.0, pltpu.roll(u, n - 1, 0))
    w = cw_ref[...]
    return up * w[0:1] + u * w[1:2] + un * w[2:3] + cb_ref[...]


def _hyena_ctx_kernel(hu_ref, cw_ref, cb_ref, kr_ref, ki_ref, fwd_ref, inv_ref, skip_ref, sg_ref, o_ref):
    uc = _short_conv(hu_ref, cw_ref, cb_ref)
    x0, x1, v = uc[:, :HY_CH], uc[:, HY_CH:2 * HY_CH], uc[:, 2 * HY_CH:]
    z = x1 * v
    zf = _dot(fwd_ref[...], z.astype(BF16))
    big_l = zf.shape[0] // 2
    zr, zi = zf[:big_l], zf[big_l:]
    kr, ki = kr_ref[...], ki_ref[...]
    p = jnp.concatenate([zr * kr - zi * ki, zr * ki + zi * kr], axis=0).astype(BF16)
    y = _dot(inv_ref[...], p)
    out = x0 * (y + z * skip_ref[...])
    o_ref[...] = (out * sg_ref[...].astype(F32)).astype(BF16)


def _pitch(rows):
    return rows + PITCH_PAD


def _to_pitched(dst, src, blk):
    for i in range(src.shape[0] // blk):
        dst[i * _pitch(blk):i * _pitch(blk) + blk, :] = src[i * blk:(i + 1) * blk, :]


def _block_diag_rows(a, b):
    za, zb = jnp.zeros_like(a), jnp.zeros_like(b)
    return jnp.concatenate([jnp.concatenate([a, zb], axis=1), jnp.concatenate([za, b], axis=1)], axis=0)


def _fft_stage1(src, m1_ref, s_re, s_im, n1, n2):
    c = src.shape[1]
    for i in range(n2 // 2):
        xs = [src[pl.ds(2 * i + t, n1 // 2, stride=_pitch(n2)), :].astype(BF16) for t in range(2)]
        a = _dot(m1_ref[i], _block_diag_rows(*xs))
        for t in range(2):
            r0 = (2 * i + t) * _pitch(n1)
            s_re[r0:r0 + n1, :] = a[:n1, t * c:(t + 1) * c]
            s_im[r0:r0 + n1, :] = a[n1:, t * c:(t + 1) * c]


def _fft_stage2(w2_ref, s_re, s_im, i, n1, n2):
    re = [s_re[pl.ds(2 * i + t, n2, stride=_pitch(n1)), :] for t in range(2)]
    im = [s_im[pl.ds(2 * i + t, n2, stride=_pitch(n1)), :] for t in range(2)]
    rhs = jnp.concatenate([jnp.concatenate(re, axis=1), jnp.concatenate(im, axis=1)], axis=0)
    x = _dot(w2_ref[...], rhs.astype(BF16))
    return x[:n2], x[n2:]


def _filter_fft_kernel(hf_ref, hb_ref, m1_ref, w2_ref, kr_ref, ki_ref, x_scr, s_re, s_im, *, n1, n2):
    c = kr_ref.shape[1]
    for src_ref, sign in ((hf_ref, 1.0), (hb_ref, -1.0)):
        _to_pitched(x_scr, src_ref, n2)
        _fft_stage1(x_scr, m1_ref, s_re, s_im, n1, n2)
        for i in range(n1 // 2):
            xr, xi = _fft_stage2(w2_ref, s_re, s_im, i, n1, n2)
            for t in range(2):
                rows = slice((2 * i + t) * n2, (2 * i + t + 1) * n2)
                if sign > 0:
                    kr_ref[rows, :] = xr[:, t * c:(t + 1) * c]
                    ki_ref[rows, :] = xi[:, t * c:(t + 1) * c]
                else:
                    kr_ref[rows, :] += xr[:, t * c:(t + 1) * c]
                    ki_ref[rows, :] -= xi[:, t * c:(t + 1) * c]


def _filter_fft(hf, hb0, m1, w2, n):
    big_l = 2 * n
    n2 = HY_N2
    n1 = big_l // n2
    inp = pl.BlockSpec((None, n, LANES), lambda l, c: (l, 0, c))
    out = pl.BlockSpec((None, big_l, LANES), lambda l, c: (l, 0, c))
    return pl.pallas_call(
        functools.partial(_filter_fft_kernel, n1=n1, n2=n2),
        grid=(DEPTH, HY_CH // LANES),
        in_specs=[inp, inp,
                  pl.BlockSpec(m1.shape, lambda l, c: (0, 0, 0)),
                  pl.BlockSpec(w2.shape, lambda l, c: (0, 0))],
        out_specs=[out, out],
        out_shape=[jax.ShapeDtypeStruct((DEPTH, big_l, HY_CH), F32)] * 2,
        scratch_shapes=[pltpu.VMEM((n1 // 2 * _pitch(n2), LANES), F32)]
                       + [pltpu.VMEM((n2 * _pitch(n1), LANES), F32)] * 2,
        compiler_params=_cparams("arbitrary", "arbitrary"), name="hyena_filter_fft",
    )(hf, hb0, m1, w2)


def _hyena_lat_kernel(x0_ref, x1_ref, v_ref, cw0_ref, cw1_ref, cw2_ref, cb0_ref, cb1_ref, cb2_ref,
                      kr_ref, ki_ref, m1_ref, w2_ref, ma_ref, mb_ref, skip_ref, sg_ref, o_ref,
                      z_scr, y_scr, s_re, s_im, *, n1, n2):
    n1h = n1 // 2
    c = o_ref.shape[1]
    z = _short_conv(x1_ref, cw1_ref, cb1_ref) * _short_conv(v_ref, cw2_ref, cb2_ref)
    _to_pitched(z_scr, z, n2)
    _fft_stage1(z_scr, m1_ref, s_re, s_im, n1, n2)
    for i in range(n1 // 2):
        xr, xi = _fft_stage2(w2_ref, s_re, s_im, i, n1, n2)
        kr = jnp.concatenate([kr_ref[(2 * i + t) * n2:(2 * i + t + 1) * n2, :] for t in range(2)], axis=1)
        ki = jnp.concatenate([ki_ref[(2 * i + t) * n2:(2 * i + t + 1) * n2, :] for t in range(2)], axis=1)
        pr = (xr * kr - xi * ki).astype(BF16)
        pi = (xr * ki + xi * kr).astype(BF16)
        ps = [jnp.concatenate([pr[:, t * c:(t + 1) * c], pi[:, t * c:(t + 1) * c]], axis=0) for t in range(2)]
        b = _dot(ma_ref[i], _block_diag_rows(*ps))
        for t in range(2):
            s_re[pl.ds(2 * i + t, n2, stride=_pitch(n1)), :] = b[:n2, t * c:(t + 1) * c]
            s_im[pl.ds(2 * i + t, n2, stride=_pitch(n1)), :] = b[n2:, t * c:(t + 1) * c]
    for i in range(n2 // 2):
        blks = []
        for t in range(2):
            r0 = (2 * i + t) * _pitch(n1)
            blks.append(jnp.concatenate([s_re[r0:r0 + n1, :], s_im[r0:r0 + n1, :]], axis=0))
        y = _dot(mb_ref[...], jnp.concatenate(blks, axis=1).astype(BF16))
        for t in range(2):
            y_scr[pl.ds(2 * i + t, n1h, stride=_pitch(n2)), :] = y[:, t * c:(t + 1) * c]
    x0 = _short_conv(x0_ref, cw0_ref, cb0_ref)
    skip = skip_ref[...]
    for j1 in range(n1h):
        rows = slice(j1 * n2, (j1 + 1) * n2)
        prow = slice(j1 * _pitch(n2), j1 * _pitch(n2) + n2)
        out = x0[rows] * (y_scr[prow, :] + z_scr[prow, :] * skip)
        o_ref[rows, :] = (out * sg_ref[rows, :].astype(F32)).astype(BF16)


def _hyena_lat(hu, sg, conv_w, conv_b, skip, kr, ki, tables, bsz, n, layer):
    m1, w2, ma, mb = tables
    big_l = 2 * n
    n2 = HY_N2
    n1 = big_l // n2
    nch = HY_CH // LANES
    ucol = lambda s: pl.BlockSpec((n, LANES), lambda c, b: (b, s * nch + c))
    wcol = lambda s: pl.BlockSpec((None, 3, LANES), lambda c, b: (layer, 0, s * nch + c))
    bcol = lambda s: pl.BlockSpec((None, 1, LANES), lambda c, b: (layer, 0, s * nch + c))
    kspec = pl.BlockSpec((None, big_l, LANES), lambda c, b: (layer, 0, c))
    const = lambda a: pl.BlockSpec(a.shape, lambda c, b: (0,) * a.ndim)
    cb3 = conv_b.reshape(DEPTH, 1, 768)
    return pl.pallas_call(
        functools.partial(_hyena_lat_kernel, n1=n1, n2=n2),
        grid=(nch, bsz),
        in_specs=[ucol(0), ucol(1), ucol(2), wcol(0), wcol(1), wcol(2), bcol(0), bcol(1), bcol(2),
                  kspec, kspec, const(m1), const(w2), const(ma), const(mb),
                  pl.BlockSpec((None, 1, LANES), lambda c, b: (layer, 0, c)),
                  pl.BlockSpec((n, LANES), lambda c, b: (b, 2 * nch + c))],
        out_specs=pl.BlockSpec((n, LANES), lambda c, b: (b, c)),
        out_shape=jax.ShapeDtypeStruct((bsz * n, 256), BF16),
        scratch_shapes=[pltpu.VMEM((n1 // 2 * _pitch(n2), LANES), F32)] * 2
                       + [pltpu.VMEM((n2 * _pitch(n1), LANES), F32)] * 2,
        compiler_params=_cparams("arbitrary", "arbitrary"), name="hyena_lat",
    )(hu, hu, hu, conv_w, conv_w, conv_w, cb3, cb3, cb3, kr, ki, m1, w2, ma, mb,
      skip.reshape(DEPTH, 1, HY_CH), sg)


def _fnet_ctx_kernel(u_ref, cs_ref, dn_ref, fw_ref, fb_ref, sg_ref, o_ref):
    ab = _dot(u_ref[...], cs_ref[...])
    stack = jnp.concatenate([ab[:, :GROUP_W], ab[:, GROUP_W:]], axis=0).astype(BF16)
    f = _dot(dn_ref[...], stack)
    out = _bdot(f, fw_ref[...]) + fb_ref[...]
    o_ref[...] = (out * sg_ref[...].astype(F32)).astype(BF16)


def _fnet_lat_kernel(u_ref, cs_ref, f1_ref, f2_ref, fw_ref, fb_ref, sg_ref, o_ref,
                     a_scr, b_scr, s_re, s_im, o_scr, *, n1, n2):
    halves = GROUP_W // LANES

    def put(scr, rows, val):
        for hh in range(halves):
            scr[hh, rows, :] = val[:, hh * LANES:(hh + 1) * LANES]

    def get(scr, rows):
        return jnp.concatenate([scr[hh, rows, :] for hh in range(halves)], axis=1)

    def block(i, blk):
        return slice(i * _pitch(blk), i * _pitch(blk) + blk)

    ab = _dot(u_ref[...], cs_ref[...])
    for j1 in range(n1):
        put(a_scr, block(j1, n2), ab[j1 * n2:(j1 + 1) * n2, :GROUP_W])
        put(b_scr, block(j1, n2), ab[j1 * n2:(j1 + 1) * n2, GROUP_W:])
    for j2 in range(n2):
        rows = pl.ds(j2, n1, stride=_pitch(n2))
        g = jnp.concatenate([get(a_scr, rows), get(b_scr, rows)], axis=0)
        t = _dot(f1_ref[j2], g.astype(BF16))
        put(s_re, block(j2, n1), t[:n1])
        put(s_im, block(j2, n1), t[n1:])
    for k1 in range(n1):
        rows = pl.ds(k1, n2, stride=_pitch(n1))
        g = jnp.concatenate([get(s_re, rows), get(s_im, rows)], axis=0)
        put(o_scr, rows, _dot(f2_ref[...], g.astype(BF16)))
    fw = fw_ref[...].astype(BF16)
    fb = fb_ref[...]
    for k2 in range(n2):
        rows = slice(k2 * n1, (k2 + 1) * n1)
        out = _dot(get(o_scr, block(k2, n1)).astype(BF16), fw) + fb
        o_ref[rows, :] = (out * sg_ref[rows, :].astype(F32)).astype(BF16)


def _fnet_lat(fu, sg, fn_w, fn_b, cs, f1, f2, bsz, n, layer):
    n2 = FN_N2
    n1 = n // n2
    return pl.pallas_call(
        functools.partial(_fnet_lat_kernel, n1=n1, n2=n2), grid=(bsz,),
        in_specs=[pl.BlockSpec((n, 256), lambda i: (i, 0)),
                  pl.BlockSpec(cs.shape, lambda i: (0, 0)),
                  pl.BlockSpec(f1.shape, lambda i: (0, 0, 0)),
                  pl.BlockSpec(f2.shape, lambda i: (0, 0)),
                  pl.BlockSpec((None, GROUP_W, GROUP_W), lambda i: (layer, 0, 0)),
                  pl.BlockSpec((None, 1, GROUP_W), lambda i: (layer, 0, 0)),
                  pl.BlockSpec((n, 256), lambda i: (i, 3))],
        out_specs=pl.BlockSpec((n, 256), lambda i: (i, 0)),
        out_shape=jax.ShapeDtypeStruct((bsz * n, 256), BF16),
        scratch_shapes=[pltpu.VMEM((GROUP_W // LANES, n1 * _pitch(n2), LANES), F32)] * 2
                       + [pltpu.VMEM((GROUP_W // LANES, n2 * _pitch(n1), LANES), F32)] * 3,
        compiler_params=_cparams("arbitrary"), name="fnet_lat",
    )(fu, cs, f1, f2, fn_w, fn_b.reshape(DEPTH, 1, GROUP_W), sg)


def _ctx_mixers_kernel(sink_ref, aq_ref, akt_ref, av_ref, lam_ref, subln_ref, dq_ref, dkt_ref, dv_ref,
                       hu_ref, cw_ref, cb_ref, kr_ref, ki_ref, fwd_ref, inv_ref, skip_ref,
                       fu_ref, cs_ref, dn_ref, fw_ref, fb_ref, sg_ref, o_ref, s_scr, p_scr, *, layer, lam_init, kc):
    sg = [sg_ref.at[:, j * GROUP_W:(j + 1) * GROUP_W] for j in range(4)]
    out = [o_ref.at[:, j * GROUP_W:(j + 1) * GROUP_W] for j in range(4)]
    _attn_a_kernel(sink_ref, aq_ref, akt_ref, av_ref, sg[0], out[0], layer=layer)
    _diff_kernel(lam_ref, subln_ref, dq_ref, dkt_ref, dv_ref, sg[1], out[1], s_scr, p_scr,
                 lam_init=lam_init, npieces=1, tq_sub=dq_ref.shape[0], kc=kc)
    _hyena_ctx_kernel(hu_ref, cw_ref, cb_ref, kr_ref, ki_ref, fwd_ref, inv_ref, skip_ref, sg[2], out[2])
    _fnet_ctx_kernel(fu_ref, cs_ref, dn_ref, fw_ref, fb_ref, sg[3], out[3])


def _exit_kernel(x_ref, *refs, latent, tiles_per_b):
    mix_refs, (w_ref, g_ref, mod_ref, o_ref) = refs[:-4], refs[-4:]
    row = (pl.program_id(0) // tiles_per_b) if latent else CTX_ROW
    gate = mod_ref[pl.ds(row, 1), :][:, 2 * D_MODEL:]
    mixed = mix_refs[0][...] if len(mix_refs) == 1 else jnp.concatenate([r[...] for r in mix_refs], axis=1)
    y = _dot(mixed, w_ref[...])
    y = y * lax.rsqrt(jnp.mean(y * y, axis=-1, keepdims=True) + EPS) * g_ref[...]
    o_ref[...] = x_ref[...] + gate * y


def _layer_exit(x2d, outs, w_out_bf, g_post, mod_all, n, layer):
    t = x2d.shape[0]
    tm = EXIT_ROW_TILE
    row = lambda i: (i, 0)
    return pl.pallas_call(
        functools.partial(_exit_kernel, latent=True, tiles_per_b=n // tm),
        grid=(t // tm,),
        in_specs=[pl.BlockSpec((tm, D_MODEL), row)] + [pl.BlockSpec((tm, o.shape[1]), row) for o in outs]
                 + [pl.BlockSpec((None, D_MODEL, D_MODEL), lambda i: (layer, 0, 0)),
                    pl.BlockSpec((None, 1, D_MODEL), lambda i: (layer, 0, 0)),
                    pl.BlockSpec((None, COND_ROWS, 3 * D_MODEL), lambda i: (layer, 0, 0))],
        out_specs=pl.BlockSpec((tm, D_MODEL), row),
        out_shape=jax.ShapeDtypeStruct((t, D_MODEL), F32),
        compiler_params=_cparams("arbitrary"),
        name="exit_latent",
    )(x2d, *outs, w_out_bf, g_post.reshape(DEPTH, 1, D_MODEL), mod_all)


def _ctx_layer_kernel(x_ref, mod_ref, gpre_ref, win_ref, *refs, layer, lam_init, kc, aliased):
    if aliased:
        refs = refs[4:]
    (sink_ref, lam_ref, subln_ref, cw_ref, cb_ref, kr_ref, ki_ref, fwd_ref, inv_ref, skip_ref, cs_ref, dn_ref,
     fw_ref, fb_ref, wout_ref, gpost_ref, xo_ref, *cache_refs,
     aq, akt, av, dq, dkt, dv, hu, fu, sg, mixed, s_scr, p_scr) = refs
    if not aliased:
        for r in cache_refs:
            for l in range(layer + 1, DEPTH):
                r[l] = jnp.zeros(r.shape[1:], r.dtype)
        cache_refs = [r.at[layer] for r in cache_refs]
    _entry_kernel(x_ref, mod_ref, gpre_ref, win_ref, aq, akt, av, dq, dkt, dv, hu, fu, sg, *cache_refs,
                  latent=False, tiles_per_b=1)
    _ctx_mixers_kernel(sink_ref, aq, akt, av, lam_ref, subln_ref, dq, dkt, dv, hu, cw_ref, cb_ref, kr_ref, ki_ref,
                       fwd_ref, inv_ref, skip_ref, fu, cs_ref, dn_ref, fw_ref, fb_ref, sg, mixed, s_scr, p_scr,
                       layer=layer, lam_init=lam_init, kc=kc)
    _exit_kernel(x_ref, mixed, wout_ref, gpost_ref, mod_ref, xo_ref, latent=False, tiles_per_b=1)


def _ctx_layer(x2d, caches, mod_all, g_pre, w_in_bf, sink, diff_lambda, diff_subln, conv_w, conv_b, skip, kr, ki,
               fwd_t, inv_t, fn_w, fn_b, cs, dn, w_out_bf, g_post, bsz, n, layer):
    lam_init = 0.8 - 0.6 * math.exp(-0.3 * layer)
    big_l = 2 * n
    aliased = caches is not None
    assert aliased == (layer > 0)
    rows = lambda c: pl.BlockSpec((n, c), lambda b: (b, 0))
    lay = lambda a, c: pl.BlockSpec((None, a, c), lambda b: (layer, 0, 0))
    const = lambda a: pl.BlockSpec(a.shape, lambda b: (0,) * a.ndim)
    widths = (ATT_KV_HEADS * ATT_HEAD_DIM, ATT_KV_HEADS * ATT_HEAD_DIM, 2 * DIF_HEADS * DIF_QK_DIM, DIF_HEADS * DIF_V_DIM)
    if aliased:
        cache_specs = [pl.BlockSpec((None, None, n, w), lambda b: (b, layer, 0, 0)) for w in widths]
    else:
        cache_specs = [pl.BlockSpec((None, DEPTH, n, w), lambda b: (b, 0, 0, 0)) for w in widths]
    nslot = DIF_HEADS * 2 + 1
    vm = lambda shape, dt=BF16: pltpu.VMEM(shape, dt)
    res = pl.pallas_call(
        functools.partial(_ctx_layer_kernel, layer=layer, lam_init=lam_init, kc=DIFF_KEY_CHUNK, aliased=aliased),
        grid=(bsz,),
        in_specs=[rows(D_MODEL), lay(COND_ROWS, 3 * D_MODEL), lay(1, D_MODEL), lay(D_MODEL, D_IN)]
                 + [pl.BlockSpec(memory_space=pl.ANY)] * (4 if aliased else 0)
                 + [pl.BlockSpec(memory_space=pltpu.SMEM),
                    lay(4, DIF_QK_DIM), lay(1, DIF_V_DIM), lay(3, 768), lay(1, 768), lay(big_l, HY_CH), lay(big_l, HY_CH),
                    const(fwd_t), const(inv_t), lay(1, HY_CH), const(cs), const(dn), lay(GROUP_W, GROUP_W),
                    lay(1, GROUP_W), lay(D_MODEL, D_MODEL), lay(1, D_MODEL)],
        out_specs=[rows(D_MODEL)] + cache_specs,
        out_shape=[jax.ShapeDtypeStruct(x2d.shape, F32)]
                  + [jax.ShapeDtypeStruct((bsz, DEPTH, n, w), F32) for w in widths],
        input_output_aliases={4 + j: 1 + j for j in range(4)} if aliased else {},
        scratch_shapes=[vm((n, 256)), vm((128, n)), vm((n, 128)), vm((n, 256)), vm((256, n)), vm((n, 512)),
                        vm((n, 768)), vm((n, 256)), vm((n, 1024)), vm((n, 1024)),
                        vm((nslot, n, n), F32), vm((nslot, n, n))],
        compiler_params=_cparams("arbitrary"), name="ctx_layer",
    )(x2d, mod_all, g_pre.reshape(DEPTH, 1, D_MODEL), w_in_bf, *(caches or ()), sink, diff_lambda,
      diff_subln.reshape(DEPTH, 1, DIF_V_DIM), conv_w, conv_b.reshape(DEPTH, 1, 768), kr, ki, fwd_t, inv_t,
      skip.reshape(DEPTH, 1, HY_CH), cs, dn, fn_w, fn_b.reshape(DEPTH, 1, GROUP_W), w_out_bf,
      g_post.reshape(DEPTH, 1, D_MODEL))
    return res[0], list(res[1:])


def kernel(x_prompt, x_sample, cache_attn_k, cache_attn_v, cache_diff_k, cache_diff_v, c, c_ctx, w_ada, b_ada, norm_pre, norm_post, w_in, w_out, attn_sink, diff_lambda, diff_subln, hy_conv_w, hy_conv_b, hy_filt_w1, hy_filt_b1, hy_filt_w2, hy_filt_b2, hy_filt_w3, hy_filt_freq, hy_skip, fn_w, fn_b):
    bp, lp, _ = x_prompt.shape
    bs, ls, _ = x_sample.shape
    past = cache_attn_k.shape[2]
    assert bs < CTX_ROW + 1 <= COND_ROWS

    cond = jnp.concatenate([c, c_ctx[None, :], jnp.zeros((COND_ROWS - bs - 1, D_MODEL), F32)], axis=0)
    mod_all = _modulation(cond, w_ada, b_ada)
    w_in_bf = w_in.astype(BF16)
    w_out_bf = w_out.astype(BF16)
    sink = attn_sink.reshape(DEPTH * ATT_HEADS)

    rope = _rope_tables(ls, ATT_HEAD_DIM) + _rope_tables(ls, DIF_QK_DIM)
    filt = (jnp.pad(hy_filt_w1, ((0, 0), (0, LANES - FILT_EMB), (0, 0))), hy_filt_b1, hy_filt_w2, hy_filt_b2,
            hy_filt_w3, hy_filt_freq)
    cos_c, sin_c, fwd_c, inv_c = _dense_conv_tables(lp)
    kr_c, ki_c = _hyena_filters_ctx(lp, filt, cos_c, sin_c)
    hy_tables = _hyena_fft_tables(ls)
    hf_l, hb_l = _hyena_filters_lat(ls, filt)
    kr_l, ki_l = _filter_fft(hf_l, hb_l, hy_tables[0], hy_tables[1], ls)
    cs_c, dn_c = _fnet_channel_table(lp), _fnet_dense_table(lp)
    cs_l = _fnet_channel_table(ls)
    f1_l, f2_l = _fnet_fft_tables(ls)

    cak = cache_attn_k.reshape(bs, DEPTH, past, 128).transpose(0, 1, 3, 2).astype(BF16)
    cav = cache_attn_v.reshape(bs, DEPTH, past, 128).astype(BF16)
    cdk = cache_diff_k.reshape(bs, DEPTH, past, 256).transpose(0, 1, 3, 2).astype(BF16)
    cdv = jnp.concatenate([cache_diff_v, jnp.ones_like(cache_diff_v)], axis=-1)
    cdv = cdv.reshape(bs, DEPTH, past, 2 * GROUP_W).astype(BF16)

    xp = x_prompt.reshape(bp * lp, D_MODEL)
    xs = x_sample.reshape(bs * ls, D_MODEL)
    new_caches = None
    for l in range(DEPTH):
        xp, new_caches = _ctx_layer(xp, new_caches, mod_all, norm_pre, w_in_bf, sink, diff_lambda, diff_subln,
                                    hy_conv_w, hy_conv_b, hy_skip, kr_c, ki_c, fwd_c, inv_c, fn_w, fn_b, cs_c, dn_c,
                                    w_out_bf, norm_post, bp, lp, l)

        (aq, akt, av, dq, dkt, dv, hu, fu, sg) = _layer_entry(xs, bs, ls, mod_all, norm_pre, w_in_bf, l, rope)
        outs = (_attn_a_latent(aq, akt, av, cak[:, l], cav[:, l], sg, sink, bs, ls, l),
                _diff_attention(dq, [(dkt, dv), (cdk[:, l], cdv[:, l])], sg, diff_lambda, diff_subln, bs, ls, l),
                _hyena_lat(hu, sg, hy_conv_w, hy_conv_b, hy_skip, kr_l, ki_l, hy_tables, bs, ls, l),
                _fnet_lat(fu, sg, fn_w, fn_b, cs_l, f1_l, f2_l, bs, ls, l))
        xs = _layer_exit(xs, outs, w_out_bf, norm_post, mod_all, ls, l)

    nak, nav, ndk, ndv = new_caches
    return (xp.reshape(bp, lp, D_MODEL), xs.reshape(bs, ls, D_MODEL),
            nak.reshape(bp, DEPTH, lp, ATT_KV_HEADS, ATT_HEAD_DIM), nav.reshape(bp, DEPTH, lp, ATT_KV_HEADS, ATT_HEAD_DIM),
            ndk.reshape(bp, DEPTH, lp, 2, DIF_HEADS, DIF_QK_DIM), ndv.reshape(bp, DEPTH, lp, DIF_HEADS, DIF_V_DIM))
```

```python
import functools
import math

import numpy as np
import jax
import jax.numpy as jnp
from jax import lax
from jax.experimental import pallas as pl
from jax.experimental.pallas import tpu as pltpu

F32 = jnp.float32
BF16 = jnp.bfloat16

D_MODEL = 1024
DEPTH = 2
GRID_W = 64
GROUP_W = 256
ATT_HEADS = 4
ATT_KV_HEADS = 2
ATT_HEAD_DIM = 64
WINDOW = 128
BLOCK = 128
DIF_HEADS = 4
DIF_V_DIM = 64
DIF_QK_DIM = 32
HY_CH = 256
FILT_BANDS = 16
FILT_EMB = 1 + 2 * FILT_BANDS
FILT_HIDDEN = 64
HY_MIN_DECAY = math.log(1e-2) / 1.5
HY_MAX_DECAY = math.log(1e-2) / 0.3
FN_GROUP_CH = 64
ROPE_BASE = 10000.0
EPS = 1e-6
NEG_INF = -1e30

C_AQ, C_AK, C_AV, C_AG = 0, 256, 384, 512
C_DQ, C_DK, C_DV, C_DG = 768, 1024, 1280, 1536
C_HU, C_HG, C_FU, C_FG, D_IN = 1792, 2560, 2816, 3072, 3328
LOG2_E = math.log2(math.e)

LANES = 128
COND_ROWS = 8
CTX_ROW = 4
VMEM_LIMIT = 56 * 1024 * 1024
PITCH_PAD = 8

ROW_TILE = 1024
EXIT_ROW_TILE = 2048
ATTN_A_QBLOCKS = 8
DIFF_Q_ROWS = 256
DIFF_KEY_CHUNK = 512

HY_N2 = 64
FN_N2 = 64


def _cparams(*sem):
    return pltpu.CompilerParams(dimension_semantics=sem, vmem_limit_bytes=VMEM_LIMIT)


def _dot(a, b):
    return jnp.dot(a, b, preferred_element_type=F32)


def _bdot(a, b):
    return jnp.dot(a.astype(BF16), b.astype(BF16), preferred_element_type=F32)


def _dot3(a, b):
    ah = a.astype(BF16)
    al = (a - ah.astype(F32)).astype(BF16)
    bh = b.astype(BF16)
    bl = (b - bh.astype(F32)).astype(BF16)
    return _dot(ah, bh) + _dot(ah, bl) + _dot(al, bh)


def _bf16_table(a):
    return jnp.asarray(a, F32).astype(BF16)


def _rope_tables(n, head_dim):
    pos = np.arange(n)
    row = (pos // GRID_W).astype(np.float64)
    col = (pos % GRID_W).astype(np.float64)
    n_freq = head_dim // 4
    inv = ROPE_BASE ** (-np.arange(n_freq, dtype=np.float64) / n_freq)
    ang = np.concatenate([row[:, None] * inv, col[:, None] * inv], axis=-1)
    reps = LANES // head_dim
    cos = np.tile(np.concatenate([np.cos(ang), np.cos(ang)], axis=-1), (1, reps))
    sin = np.tile(np.concatenate([-np.sin(ang), np.sin(ang)], axis=-1), (1, reps))
    return jnp.asarray(cos, F32), jnp.asarray(sin, F32)


def _filter_feats(n):
    t = np.linspace(0.0, 1.0, n)[:, None]
    w = (2.0 * math.pi / n) * np.arange(n)[:, None]
    f = np.linspace(1e-4, FILT_BANDS - 1, FILT_BANDS)[None, :]
    feats = np.concatenate([t, np.cos(f * w), -np.sin(f * w)], axis=-1)
    feats = np.pad(feats, ((0, 0), (0, LANES - FILT_EMB)))
    feats = np.concatenate([feats[:n // 2], feats[n // 2:]], axis=1)
    deltas = np.abs(np.linspace(HY_MIN_DECAY, HY_MAX_DECAY, HY_CH))
    decay = np.exp(-t * deltas[None, :])
    return jnp.asarray(feats, F32), jnp.asarray(decay, F32)


def _dense_conv_tables(n):
    big_l = 2 * n
    k = np.arange(big_l)[:, None]
    t = np.arange(n)[None, :]
    th = 2.0 * math.pi * k * t / big_l
    fwd = np.concatenate([np.cos(th), -np.sin(th)], axis=0)
    inv = np.concatenate([np.cos(th).T, -np.sin(th).T], axis=1) / big_l
    return (jnp.asarray(np.cos(th), F32), jnp.asarray(np.sin(th), F32), _bf16_table(fwd), _bf16_table(inv))


def _hyena_fft_tables(n):
    big_l = 2 * n
    n2 = HY_N2
    n1 = big_l // n2
    n1h = n1 // 2
    k1 = np.arange(n1)
    j1 = np.arange(n1h)
    j2 = np.arange(n2)
    k2 = np.arange(n2)
    th = 2.0 * math.pi * k1[None, :, None] * (j1[None, None, :] * n2 + j2[:, None, None]) / big_l
    m1 = np.concatenate([np.cos(th), -np.sin(th)], axis=1)
    th2 = 2.0 * math.pi * k2[:, None] * j2[None, :] / n2
    c2, s2 = np.cos(th2), np.sin(th2)
    w2 = np.block([[c2, s2], [-s2, c2]])
    tha = 2.0 * math.pi * j2[None, :, None] * (k1[:, None, None] + n1 * k2[None, None, :]) / big_l
    ca, sa = np.cos(tha), np.sin(tha)
    ma = np.concatenate([np.concatenate([ca, -sa], axis=2), np.concatenate([sa, ca], axis=2)], axis=1)
    thb = 2.0 * math.pi * j1[:, None] * k1[None, :] / n1
    mb = np.concatenate([np.cos(thb), -np.sin(thb)], axis=1) / big_l
    m1 = np.concatenate([m1[0::2], m1[1::2]], axis=2)
    ma = np.concatenate([ma[0::2], ma[1::2]], axis=2)
    return _bf16_table(m1), _bf16_table(w2), _bf16_table(ma), _bf16_table(mb)


def _fnet_channel_table(n):
    m = np.arange(FN_GROUP_CH)
    th = 2.0 * math.pi * m[:, None] * m[None, :] / FN_GROUP_CH
    eye = np.eye(GROUP_W // FN_GROUP_CH)
    sc = 1.0 / math.sqrt(n * FN_GROUP_CH)
    cs = np.concatenate([np.kron(eye, np.cos(th)), np.kron(eye, np.sin(th))], axis=1) * sc
    return _bf16_table(cs)


def _fnet_dense_table(n):
    k = np.arange(n)
    th = 2.0 * math.pi * k[:, None] * k[None, :] / n
    return _bf16_table(np.concatenate([np.cos(th), -np.sin(th)], axis=1))


def _fnet_fft_tables(n):
    n2 = FN_N2
    n1 = n // n2
    k1 = np.arange(n1)
    j1 = np.arange(n1)
    j2 = np.arange(n2)
    k2 = np.arange(n2)
    th = 2.0 * math.pi * k1[None, :, None] * (j1[None, None, :] * n2 + j2[:, None, None]) / n
    c, s = np.cos(th), np.sin(th)
    f1 = np.concatenate([np.concatenate([c, -s], axis=2), np.concatenate([s, c], axis=2)], axis=1)
    th2 = 2.0 * math.pi * k2[:, None] * j2[None, :] / n2
    f2 = np.concatenate([np.cos(th2), -np.sin(th2)], axis=1)
    return _bf16_table(f1), _bf16_table(f2)


def _mod_kernel(cond_ref, w_ref, b_ref, o_ref):
    c = cond_ref[...]
    s = c * jax.nn.sigmoid(c)
    o_ref[...] = _dot3(s, w_ref[...]) + b_ref[...]


def _modulation(cond, w_ada, b_ada):
    tn = 1024
    return pl.pallas_call(
        _mod_kernel,
        grid=(DEPTH, 3 * D_MODEL // tn),
        in_specs=[pl.BlockSpec((COND_ROWS, D_MODEL), lambda l, j: (0, 0)),
                  pl.BlockSpec((None, D_MODEL, tn), lambda l, j: (l, 0, j)),
                  pl.BlockSpec((None, 1, tn), lambda l, j: (l, 0, j))],
        out_specs=pl.BlockSpec((None, COND_ROWS, tn), lambda l, j: (l, 0, j)),
        out_shape=jax.ShapeDtypeStruct((DEPTH, COND_ROWS, 3 * D_MODEL), F32),
        compiler_params=_cparams("arbitrary", "arbitrary"),
        name="modulation",
    )(cond, w_ada, b_ada.reshape(DEPTH, 1, 3 * D_MODEL))


def _rope(x, cos, sin, half):
    lane = lax.broadcasted_iota(jnp.int32, x.shape, 1)
    first = (lane % (2 * half)) < half
    partner = jnp.where(first, pltpu.roll(x, LANES - half, 1), pltpu.roll(x, half, 1))
    return x * cos + partner * sin


def _entry_kernel(*refs, latent, tiles_per_b):
    if latent:
        (x_ref, mod_ref, g_ref, w_ref, ca_ref, sa_ref, cd_ref, sd_ref,
         aq_ref, akt_ref, av_ref, dq_ref, dkt_ref, dv_ref, hu_ref, fu_ref, sg_ref) = refs
        row = pl.program_id(0) // tiles_per_b
    else:
        (x_ref, mod_ref, g_ref, w_ref,
         aq_ref, akt_ref, av_ref, dq_ref, dkt_ref, dv_ref, hu_ref, fu_ref, sg_ref,
         ak32_ref, av32_ref, dk32_ref, dv32_ref) = refs
        row = CTX_ROW
    m = mod_ref[pl.ds(row, 1), :]
    shift, scale = m[:, :D_MODEL], m[:, D_MODEL:2 * D_MODEL]
    x = x_ref[...]
    h = x * lax.rsqrt(jnp.mean(x * x, axis=-1, keepdims=True) + EPS) * g_ref[...]
    hb = (h * (1.0 + scale) + shift).astype(BF16)

    projected = _dot(hb, w_ref[...])

    def proj(c0, c1):
        return projected[:, c0:c1]

    def roped(p, cos_ref, sin_ref, half):
        if not latent:
            return p
        cos, sin = cos_ref[...], sin_ref[...]
        chunks = [_rope(p[:, j:j + LANES], cos, sin, half) for j in range(0, p.shape[1], LANES)]
        return chunks[0] if len(chunks) == 1 else jnp.concatenate(chunks, axis=1)

    ca = sa = cd = sd = None
    if latent:
        ca, sa, cd, sd = ca_ref, sa_ref, cd_ref, sd_ref

    aq_ref[...] = (roped(proj(C_AQ, C_AK), ca, sa, ATT_HEAD_DIM // 2) * (ATT_HEAD_DIM ** -0.5 * LOG2_E)).astype(BF16)
    akv = proj(C_AK, C_AG)
    ak = roped(akv[:, :C_AV - C_AK], ca, sa, ATT_HEAD_DIM // 2)
    akt_ref[...] = ak.T.astype(BF16)
    av = akv[:, C_AV - C_AK:]
    av_ref[...] = av.astype(BF16)
    dq_ref[...] = (roped(proj(C_DQ, C_DK), cd, sd, DIF_QK_DIM // 2) * (DIF_QK_DIM ** -0.5 * LOG2_E)).astype(BF16)
    dk = roped(proj(C_DK, C_DV), cd, sd, DIF_QK_DIM // 2)
    dkt_ref[...] = dk.T.astype(BF16)
    dv = proj(C_DV, C_DG)
    value_lane = lax.broadcasted_iota(jnp.int32, (dv.shape[0], LANES), 1) < DIF_V_DIM
    for h in range(DIF_HEADS):
        pair = dv[:, (h // 2) * LANES:(h // 2 + 1) * LANES]
        if h % 2:
            pair = pltpu.roll(pair, DIF_V_DIM, 1)
        dv_ref[:, h * LANES:(h + 1) * LANES] = jnp.where(value_lane, pair, 1.0).astype(BF16)
    if not latent:
        ak32_ref[...] = ak
        av32_ref[...] = av
        dk32_ref[...] = dk
        dv32_ref[...] = dv
    hu_ref[...] = proj(C_HU, C_HG).astype(BF16)
    fu_ref[...] = proj(C_FU, C_FG).astype(BF16)
    for j, c0 in enumerate((C_AG, C_DG, C_HG, C_FG)):
        g = proj(c0, c0 + GROUP_W)
        sg_ref[:, j * GROUP_W:(j + 1) * GROUP_W] = (g * jax.nn.sigmoid(g)).astype(BF16)


def _layer_entry(x2d, bsz, n, mod_all, g_pre, w_in_bf, layer, rope):
    t = bsz * n
    tm = ROW_TILE
    tiles_per_b = n // tm
    row = lambda i: (i, 0)
    pos = lambda i: (i % tiles_per_b, 0)
    kt_map = lambda i: (i // tiles_per_b, 0, i % tiles_per_b)
    in_specs = [pl.BlockSpec((tm, D_MODEL), row),
                pl.BlockSpec((None, COND_ROWS, 3 * D_MODEL), lambda i: (layer, 0, 0)),
                pl.BlockSpec((None, 1, D_MODEL), lambda i: (layer, 0, 0)),
                pl.BlockSpec((None, D_MODEL, D_IN), lambda i: (layer, 0, 0))] + [pl.BlockSpec((tm, LANES), pos)] * 4
    bf = lambda c: jax.ShapeDtypeStruct((t, c), BF16)
    rs = lambda c: pl.BlockSpec((tm, c), row)
    out_shape = [bf(256), jax.ShapeDtypeStruct((bsz, 128, n), BF16), bf(128),
                 bf(256), jax.ShapeDtypeStruct((bsz, 256, n), BF16), bf(512),
                 bf(768), bf(256), bf(1024)]
    out_specs = [rs(256), pl.BlockSpec((None, 128, tm), kt_map), rs(128),
                 rs(256), pl.BlockSpec((None, 256, tm), kt_map), rs(512),
                 rs(768), rs(256), rs(1024)]
    return pl.pallas_call(
        functools.partial(_entry_kernel, latent=True, tiles_per_b=tiles_per_b),
        grid=(t // tm,), in_specs=in_specs, out_specs=out_specs, out_shape=out_shape,
        compiler_params=_cparams("arbitrary"), name="entry_latent",
    )(x2d, mod_all, g_pre.reshape(DEPTH, 1, D_MODEL), w_in_bf, *rope)


def _sink_columns(sink_ref, layer, tq):
    grp = ATT_HEADS // ATT_KV_HEADS
    head_row = lax.broadcasted_iota(jnp.int32, (grp * tq, 1), 0) // tq
    cols = []
    for g in range(ATT_KV_HEADS):
        col = jnp.zeros((grp * tq, 1), F32)
        for j in range(grp):
            col = jnp.where(head_row == j, sink_ref[layer * ATT_HEADS + g * grp + j] * LOG2_E, col)
        cols.append(col)
    return cols


def _stack_heads(q_ref, rows, g):
    grp = ATT_HEADS // ATT_KV_HEADS
    return jnp.concatenate([q_ref[rows, (g * grp + t) * ATT_HEAD_DIM:(g * grp + t + 1) * ATT_HEAD_DIM]
                            for t in range(grp)], axis=0)


def _attn_a_kernel(sink_ref, q_ref, k_ref, v_ref, sg_ref, o_ref, *, layer):
    tq = q_ref.shape[0]
    grp = ATT_HEADS // ATT_KV_HEADS
    sink_cols = _sink_columns(sink_ref, layer, tq)
    outs = []
    for g in range(ATT_KV_HEADS):
        d0 = g * ATT_HEAD_DIM
        s = _dot(_stack_heads(q_ref, slice(None), g), k_ref[d0:d0 + ATT_HEAD_DIM, :])
        m = jnp.maximum(jnp.max(s, axis=-1, keepdims=True), sink_cols[g])
        p = jnp.exp2(s - m)
        l = jnp.sum(p, axis=-1, keepdims=True) + jnp.exp2(sink_cols[g] - m)
        o = _dot(p.astype(BF16), v_ref[:, d0:d0 + ATT_HEAD_DIM]) / l
        outs += [o[j * tq:(j + 1) * tq] for j in range(grp)]
    o_full = jnp.concatenate(outs, axis=1)
    o_ref[...] = (o_full * sg_ref[...].astype(F32)).astype(BF16)


def _attn_a_win_kernel(sink_ref, q_ref, kp_ref, kc_ref, kn_ref, vp_ref, vc_ref, vn_ref, ck_ref, cv_ref, sg_ref,
                       o_ref, *, layer, nsteps, qb):
    i = pl.program_id(1)
    grp = ATT_HEADS // ATT_KV_HEADS
    rows = grp * BLOCK
    kw = jnp.concatenate([kp_ref[...], kc_ref[...], kn_ref[...]], axis=1)
    vw = jnp.concatenate([vp_ref[...], vc_ref[...], vn_ref[...]], axis=0)
    r = lax.broadcasted_iota(jnp.int32, (rows, 3 * BLOCK), 0) % BLOCK
    c = lax.broadcasted_iota(jnp.int32, (rows, 3 * BLOCK), 1)
    band = (c >= r) & (c <= r + 2 * WINDOW)
    masks = {0: band & ((c >= BLOCK) | (i > 0)), qb - 1: band & ((c < 2 * BLOCK) | (i < nsteps - 1))}
    sink_cols = _sink_columns(sink_ref, layer, BLOCK)
    ck = ck_ref[...]
    cv = cv_ref[...]
    items = [(j, g) for j in range(qb) for g in range(ATT_KV_HEADS)]
    st_a, st_b, outs = {}, {}, {}
    for step in range(len(items) + 2):
        a, b, cc = step, step - 1, step - 2
        if a < len(items):
            j, g = items[a]
            d0 = g * ATT_HEAD_DIM
            q2 = _stack_heads(q_ref, slice(j * BLOCK, (j + 1) * BLOCK), g)
            s_w = _dot(q2, kw[d0:d0 + ATT_HEAD_DIM, j * BLOCK:(j + 3) * BLOCK])
            s_w = jnp.where(masks.get(j, band), s_w, NEG_INF)
            s_c = _dot(q2, ck[d0:d0 + ATT_HEAD_DIM, :])
            m = jnp.maximum(jnp.maximum(jnp.max(s_w, axis=-1, keepdims=True),
                                        jnp.max(s_c, axis=-1, keepdims=True)), sink_cols[g])
            st_a[a] = (s_w, s_c, m)
        if 0 <= b < len(items):
            s_w, s_c, m = st_a.pop(b)
            p_w = jnp.exp2(s_w - m)
            p_c = jnp.exp2(s_c - m)
            l = (jnp.sum(p_w, axis=-1, keepdims=True) + jnp.sum(p_c, axis=-1, keepdims=True)
                 + jnp.exp2(sink_cols[items[b][1]] - m))
            st_b[b] = (p_w.astype(BF16), p_c.astype(BF16), l)
        if 0 <= cc < len(items):
            j, g = items[cc]
            d0 = g * ATT_HEAD_DIM
            p_w, p_c, l = st_b.pop(cc)
            o = (_dot(p_w, vw[j * BLOCK:(j + 3) * BLOCK, d0:d0 + ATT_HEAD_DIM])
                 + _dot(p_c, cv[:, d0:d0 + ATT_HEAD_DIM])) / l
            outs.setdefault(j, []).extend([o[t * BLOCK:(t + 1) * BLOCK] for t in range(grp)])
            if g == ATT_KV_HEADS - 1:
                rs = slice(j * BLOCK, (j + 1) * BLOCK)
                o_full = jnp.concatenate(outs.pop(j), axis=1)
                o_ref[rs, :] = (o_full * sg_ref[rs, :].astype(F32)).astype(BF16)


def _attn_a_latent(aq, akt, av, ck_t, cv, sg, sink, bsz, n, layer):
    qb = ATTN_A_QBLOCKS
    nb = n // BLOCK
    nsteps = nb // qb
    qmap = lambda b, i: (b * nsteps + i, 0)
    prev = lambda i: jnp.maximum(qb * i - 1, 0)
    nxt = lambda i: jnp.minimum(qb * i + qb, nb - 1)
    kedge = lambda f: pl.BlockSpec((None, 128, BLOCK), lambda b, i: (b, 0, f(i)))
    vedge = lambda f: pl.BlockSpec((BLOCK, 128), lambda b, i: (b * nb + f(i), 0))
    return pl.pallas_call(
        functools.partial(_attn_a_win_kernel, layer=layer, nsteps=nsteps, qb=qb),
        grid=(bsz, nsteps),
        in_specs=[pl.BlockSpec(memory_space=pltpu.SMEM),
                  pl.BlockSpec((qb * BLOCK, 256), qmap),
                  kedge(prev), pl.BlockSpec((None, 128, qb * BLOCK), lambda b, i: (b, 0, i)), kedge(nxt),
                  vedge(prev), pl.BlockSpec((qb * BLOCK, 128), qmap), vedge(nxt),
                  pl.BlockSpec((None, 128, ck_t.shape[2]), lambda b, i: (b, 0, 0)),
                  pl.BlockSpec((None, cv.shape[1], 128), lambda b, i: (b, 0, 0)),
                  pl.BlockSpec((qb * BLOCK, 256), qmap)],
        out_specs=pl.BlockSpec((qb * BLOCK, 256), qmap),
        out_shape=jax.ShapeDtypeStruct((bsz * n, 256), BF16),
        compiler_params=_cparams("arbitrary", "arbitrary"),
        name="attn_a_latent",
    )(sink, aq, akt, akt, akt, av, av, av, ck_t, cv, sg)


def _diff_kernel(lam_ref, subln_ref, q_ref, *refs, lam_init, npieces, tq_sub, kc):
    kts = refs[0:2 * npieces:2]
    vs = refs[1:2 * npieces:2]
    sg_ref, o_ref, s_scr, p_scr = refs[2 * npieces:2 * npieces + 4]
    lp = lam_ref[...]
    lam = (jnp.exp(jnp.sum(lp[0:1] * lp[1:2], axis=-1, keepdims=True))
           - jnp.exp(jnp.sum(lp[2:3] * lp[3:4], axis=-1, keepdims=True)) + lam_init)
    subln = subln_ref[...]
    tq = q_ref.shape[0]
    chunks, off = [], 0
    for kt, v in zip(kts, vs):
        for c0 in range(0, kt.shape[1], kc):
            w = min(kc, kt.shape[1] - c0)
            chunks.append((kt, v, c0, w, off))
            off += w
    items = [(r0, h, mp) for r0 in range(0, tq, tq_sub) for h in range(DIF_HEADS) for mp in range(2)]
    n_items = len(items)
    mx, acc, om, outs = {}, {}, {}, {}

    def fold(s):
        return functools.reduce(jnp.maximum, [s[:, j:j + LANES] for j in range(0, s.shape[1], LANES)])

    nslot = s_scr.shape[0]
    lag = nslot - 1
    for step in range(n_items + 2 * lag):
        a = step if step < n_items else None
        b = step - lag if 0 <= step - lag < n_items else None
        c = step - 2 * lag if 0 <= step - 2 * lag < n_items else None
        if a is not None:
            r0, h, mp = items[a]
            f0 = (mp * DIF_HEADS + h) * DIF_QK_DIM
            q_a = q_ref[r0:r0 + tq_sub, f0:f0 + DIF_QK_DIM]
            mx[a] = jnp.full((tq_sub, LANES), NEG_INF, F32)
        if b is not None:
            m_b = jnp.max(mx.pop(b), axis=-1, keepdims=True)
        if c is not None:
            acc[c] = jnp.zeros((tq_sub, 2 * DIF_V_DIM), F32)
            hc = items[c][1]
        for kt, v, c0, w, o0 in chunks:
            if a is not None:
                s = _dot(q_a, kt[f0:f0 + DIF_QK_DIM, c0:c0 + w])
                s_scr[a % nslot, :, o0:o0 + w] = s
                mx[a] = jnp.maximum(mx[a], fold(s))
            if b is not None:
                p_scr[b % nslot, :, o0:o0 + w] = jnp.exp2(s_scr[b % nslot, :, o0:o0 + w] - m_b).astype(BF16)
            if c is not None:
                acc[c] = acc[c] + _dot(p_scr[c % nslot, :, o0:o0 + w],
                                       v[c0:c0 + w, 2 * DIF_V_DIM * hc:2 * DIF_V_DIM * (hc + 1)])
        if c is not None:
            r0, h, mp = items[c]
            o = acc.pop(c)
            om[(r0, h, mp)] = o[:, :DIF_V_DIM] / o[:, DIF_V_DIM:DIF_V_DIM + 1]
            if mp == 1:
                av = om.pop((r0, h, 0)) - lam * om.pop((r0, h, 1))
                y = av * lax.rsqrt(jnp.mean(av * av, axis=-1, keepdims=True) + EPS) * subln * (1.0 - lam_init)
                outs.setdefault(r0, []).append(y)
                if h == DIF_HEADS - 1:
                    o_full = jnp.concatenate(outs.pop(r0), axis=1)
                    o_ref[r0:r0 + tq_sub, :] = (o_full * sg_ref[r0:r0 + tq_sub, :].astype(F32)).astype(BF16)


def _diff_attention(dq, pieces, sg, diff_lambda, diff_subln, bsz, n, layer):
    tq, kc = DIFF_Q_ROWS, DIFF_KEY_CHUNK
    lam_init = 0.8 - 0.6 * math.exp(-0.3 * layer)
    nt = n // tq
    nk_all = sum(kt.shape[2] for kt, _ in pieces)
    assert nk_all > kc
    nslot = 2
    in_specs = [pl.BlockSpec((None, 4, DIF_QK_DIM), lambda b, i: (layer, 0, 0)),
                pl.BlockSpec((None, 1, DIF_V_DIM), lambda b, i: (layer, 0, 0)),
                pl.BlockSpec((tq, 256), lambda b, i: (b * nt + i, 0))]
    args = [diff_lambda, diff_subln.reshape(DEPTH, 1, DIF_V_DIM), dq]
    for kt, v in pieces:
        nk = kt.shape[2]
        in_specs.append(pl.BlockSpec((None, 256, nk), lambda b, i: (b, 0, 0)))
        if v.ndim == 3:
            in_specs.append(pl.BlockSpec((None, nk, 512), lambda b, i: (b, 0, 0)))
        else:
            in_specs.append(pl.BlockSpec((nk, 512), lambda b, i: (b, 0)))
        args += [kt, v]
    in_specs.append(pl.BlockSpec((tq, 256), lambda b, i: (b * nt + i, 1)))
    args.append(sg)
    return pl.pallas_call(
        functools.partial(_diff_kernel, lam_init=lam_init, npieces=len(pieces), tq_sub=tq, kc=kc),
        grid=(bsz, nt), in_specs=in_specs,
        out_specs=pl.BlockSpec((tq, 256), lambda b, i: (b * nt + i, 0)),
        out_shape=jax.ShapeDtypeStruct((bsz * n, 256), BF16),
        scratch_shapes=[pltpu.VMEM((nslot, tq, nk_all), F32), pltpu.VMEM((nslot, tq, nk_all), BF16)],
        compiler_params=_cparams("arbitrary", "arbitrary"),
        name="diff_attn_latent",
    )(*args)


def _filter_core(feats_ref, w1_ref, b1_ref, w2_ref, b2_ref, w3_ref, fr_ref, decay_ref):
    fr = fr_ref[...]
    h = jnp.sin(fr * (_dot3(feats_ref[...], w1_ref[...]) + b1_ref[...]))
    h = jnp.sin(fr * (_dot3(h, w2_ref[...]) + b2_ref[...]))
    h = _dot3(h, w3_ref[...])
    h = jnp.concatenate([h[:, :2 * HY_CH], h[:, 2 * HY_CH:]], axis=0)
    decay = decay_ref[...]
    hf = h[:, :HY_CH] * decay
    hb = h[:, HY_CH:] * decay
    tot = (jnp.sum(jnp.abs(hf), axis=0, keepdims=True) + jnp.sum(jnp.abs(hb), axis=0, keepdims=True) + EPS)
    hf = hf / tot
    hb = hb / tot
    rowi = lax.broadcasted_iota(jnp.int32, hb.shape, 0)
    return hf, jnp.where(rowi == 0, 0.0, hb)


def _filter_ctx_kernel(feats_ref, w1_ref, b1_ref, w2_ref, b2_ref, w3_ref, fr_ref, decay_ref,
                       cos_ref, sin_ref, kr_ref, ki_ref):
    hf, hb0 = _filter_core(feats_ref, w1_ref, b1_ref, w2_ref, b2_ref, w3_ref, fr_ref, decay_ref)
    kr_ref[...] = _dot3(cos_ref[...], hf + hb0)
    ki_ref[...] = _dot3(sin_ref[...], hb0 - hf)


def _filter_lat_kernel(feats_ref, w1_ref, b1_ref, w2_ref, b2_ref, w3_ref, fr_ref, decay_ref, hf_ref, hb_ref):
    hf, hb0 = _filter_core(feats_ref, w1_ref, b1_ref, w2_ref, b2_ref, w3_ref, fr_ref, decay_ref)
    hf_ref[...] = hf
    hb_ref[...] = hb0


def _filter_specs(n, w1p, b1, w2, b2, w3, freq, feats, decay):
    const = lambda shape: pl.BlockSpec(shape, lambda l: (0,) * len(shape))
    lay = lambda a, b: pl.BlockSpec((None, a, b), lambda l: (l, 0, 0))
    hid = 2 * FILT_HIDDEN
    in_specs = [const((n // 2, 2 * LANES)), lay(2 * LANES, hid), lay(1, hid), lay(hid, hid), lay(1, hid),
                lay(hid, 4 * HY_CH), lay(1, hid), const((n, HY_CH))]
    row2 = lambda a: jnp.tile(a.reshape(DEPTH, 1, FILT_HIDDEN), (1, 1, 2))
    args = [feats, _block_diag2(w1p), row2(b1), _block_diag2(w2), row2(b2), _block_diag2(w3), row2(freq), decay]
    return in_specs, args


def _block_diag2(w):
    z = jnp.zeros_like(w)
    return jnp.concatenate([jnp.concatenate([w, z], axis=2), jnp.concatenate([z, w], axis=2)], axis=1)


def _hyena_filters_ctx(n, filt, cos_t, sin_t):
    feats, decay = _filter_feats(n)
    in_specs, args = _filter_specs(n, *filt, feats, decay)
    big_l = 2 * n
    in_specs += [pl.BlockSpec((big_l, n), lambda l: (0, 0))] * 2
    out = pl.BlockSpec((None, big_l, HY_CH), lambda l: (l, 0, 0))
    return pl.pallas_call(
        _filter_ctx_kernel, grid=(DEPTH,), in_specs=in_specs, out_specs=[out, out],
        out_shape=[jax.ShapeDtypeStruct((DEPTH, big_l, HY_CH), F32)] * 2,
        compiler_params=_cparams("arbitrary"), name="hyena_filter_ctx",
    )(*args, cos_t, sin_t)


def _hyena_filters_lat(n, filt):
    feats, decay = _filter_feats(n)
    in_specs, args = _filter_specs(n, *filt, feats, decay)
    out = pl.BlockSpec((None, n, HY_CH), lambda l: (l, 0, 0))
    return pl.pallas_call(
        _filter_lat_kernel, grid=(DEPTH,), in_specs=in_specs, out_specs=[out, out],
        out_shape=[jax.ShapeDtypeStruct((DEPTH, n, HY_CH), F32)] * 2,
        compiler_params=_cparams("arbitrary"), name="hyena_filter_lat",
    )(*args)


def _short_conv(u_ref, cw_ref, cb_ref):
    u = u_ref[...].astype(F32)
    n = u.shape[0]
    rowi = lax.broadcasted_iota(jnp.int32, u.shape, 0)
    up = jnp.where(rowi == 0, 0.0, pltpu.roll(u, 1, 0))
    un = jnp.where(rowi == n - 1, 0.0, pltpu.roll(u, n - 1, 0))
    w = cw_ref[...]
    return up * w[0:1] + u * w[1:2] + un * w[2:3] + cb_ref[...]


def _hyena_ctx_kernel(hu_ref, cw_ref, cb_ref, kr_ref, ki_ref, fwd_ref, inv_ref, skip_ref, sg_ref, o_ref):
    uc = _short_conv(hu_ref, cw_ref, cb_ref)
    x0, x1, v = uc[:, :HY_CH], uc[:, HY_CH:2 * HY_CH], uc[:, 2 * HY_CH:]
    z = x1 * v
    zf = _dot(fwd_ref[...], z.astype(BF16))
    big_l = zf.shape[0] // 2
    zr, zi = zf[:big_l], zf[big_l:]
    kr, ki = kr_ref[...], ki_ref[...]
    p = jnp.concatenate([zr * kr - zi * ki, zr * ki + zi * kr], axis=0).astype(BF16)
    y = _dot(inv_ref[...], p)
    out = x0 * (y + z * skip_ref[...])
    o_ref[...] = (out * sg_ref[...].astype(F32)).astype(BF16)


def _pitch(rows):
    return rows + PITCH_PAD


def _to_pitched(dst, src, blk):
    for i in range(src.shape[0] // blk):
        dst[i * _pitch(blk):i * _pitch(blk) + blk, :] = src[i * blk:(i + 1) * blk, :]


def _block_diag_rows(a, b):
    za, zb = jnp.zeros_like(a), jnp.zeros_like(b)
    return jnp.concatenate([jnp.concatenate([a, zb], axis=1), jnp.concatenate([za, b], axis=1)], axis=0)


def _fft_stage1(src, m1_ref, s_re, s_im, n1, n2):
    c = src.shape[1]
    for i in range(n2 // 2):
        xs = [src[pl.ds(2 * i + t, n1 // 2, stride=_pitch(n2)), :].astype(BF16) for t in range(2)]
        a = _dot(m1_ref[i], _block_diag_rows(*xs))
        for t in range(2):
            r0 = (2 * i + t) * _pitch(n1)
            s_re[r0:r0 + n1, :] = a[:n1, t * c:(t + 1) * c]
            s_im[r0:r0 + n1, :] = a[n1:, t * c:(t + 1) * c]


def _fft_stage2(w2_ref, s_re, s_im, i, n1, n2):
    re = [s_re[pl.ds(2 * i + t, n2, stride=_pitch(n1)), :] for t in range(2)]
    im = [s_im[pl.ds(2 * i + t, n2, stride=_pitch(n1)), :] for t in range(2)]
    rhs = jnp.concatenate([jnp.concatenate(re, axis=1), jnp.concatenate(im, axis=1)], axis=0)
    x = _dot(w2_ref[...], rhs.astype(BF16))
    return x[:n2], x[n2:]


def _filter_fft_kernel(hf_ref, hb_ref, m1_ref, w2_ref, kr_ref, ki_ref, x_scr, s_re, s_im, *, n1, n2):
    c = kr_ref.shape[1]
    for src_ref, sign in ((hf_ref, 1.0), (hb_ref, -1.0)):
        _to_pitched(x_scr, src_ref, n2)
        _fft_stage1(x_scr, m1_ref, s_re, s_im, n1, n2)
        for i in range(n1 // 2):
            xr, xi = _fft_stage2(w2_ref, s_re, s_im, i, n1, n2)
            for t in range(2):
                rows = slice((2 * i + t) * n2, (2 * i + t + 1) * n2)
                if sign > 0:
                    kr_ref[rows, :] = xr[:, t * c:(t + 1) * c]
                    ki_ref[rows, :] = xi[:, t * c:(t + 1) * c]
                else:
                    kr_ref[rows, :] += xr[:, t * c:(t + 1) * c]
                    ki_ref[rows, :] -= xi[:, t * c:(t + 1) * c]


def _filter_fft(hf, hb0, m1, w2, n):
    big_l = 2 * n
    n2 = HY_N2
    n1 = big_l // n2
    inp = pl.BlockSpec((None, n, LANES), lambda l, c: (l, 0, c))
    out = pl.BlockSpec((None, big_l, LANES), lambda l, c: (l, 0, c))
    return pl.pallas_call(
        functools.partial(_filter_fft_kernel, n1=n1, n2=n2),
        grid=(DEPTH, HY_CH // LANES),
        in_specs=[inp, inp,
                  pl.BlockSpec(m1.shape, lambda l, c: (0, 0, 0)),
                  pl.BlockSpec(w2.shape, lambda l, c: (0, 0))],
        out_specs=[out, out],
        out_shape=[jax.ShapeDtypeStruct((DEPTH, big_l, HY_CH), F32)] * 2,
        scratch_shapes=[pltpu.VMEM((n1 // 2 * _pitch(n2), LANES), F32)]
                       + [pltpu.VMEM((n2 * _pitch(n1), LANES), F32)] * 2,
        compiler_params=_cparams("arbitrary", "arbitrary"), name="hyena_filter_fft",
    )(hf, hb0, m1, w2)


def _hyena_lat_kernel(x0_ref, x1_ref, v_ref, cw0_ref, cw1_ref, cw2_ref, cb0_ref, cb1_ref, cb2_ref,
                      kr_ref, ki_ref, m1_ref, w2_ref, ma_ref, mb_ref, skip_ref, sg_ref, o_ref,
                      z_scr, y_scr, s_re, s_im, *, n1, n2):
    n1h = n1 // 2
    c = o_ref.shape[1]
    z = _short_conv(x1_ref, cw1_ref, cb1_ref) * _short_conv(v_ref, cw2_ref, cb2_ref)
    _to_pitched(z_scr, z, n2)
    _fft_stage1(z_scr, m1_ref, s_re, s_im, n1, n2)
    for i in range(n1 // 2):
        xr, xi = _fft_stage2(w2_ref, s_re, s_im, i, n1, n2)
        kr = jnp.concatenate([kr_ref[(2 * i + t) * n2:(2 * i + t + 1) * n2, :] for t in range(2)], axis=1)
        ki = jnp.concatenate([ki_ref[(2 * i + t) * n2:(2 * i + t + 1) * n2, :] for t in range(2)], axis=1)
        pr = (xr * kr - xi * ki).astype(BF16)
        pi = (xr * ki + xi * kr).astype(BF16)
        ps = [jnp.concatenate([pr[:, t * c:(t + 1) * c], pi[:, t * c:(t + 1) * c]], axis=0) for t in range(2)]
        b = _dot(ma_ref[i], _block_diag_rows(*ps))
        for t in range(2):
            s_re[pl.ds(2 * i + t, n2, stride=_pitch(n1)), :] = b[:n2, t * c:(t + 1) * c]
            s_im[pl.ds(2 * i + t, n2, stride=_pitch(n1)), :] = b[n2:, t * c:(t + 1) * c]
    for i in range(n2 // 2):
        blks = []
        for t in range(2):
            r0 = (2 * i + t) * _pitch(n1)
            blks.append(jnp.concatenate([s_re[r0:r0 + n1, :], s_im[r0:r0 + n1, :]], axis=0))
        y = _dot(mb_ref[...], jnp.concatenate(blks, axis=1).astype(BF16))
        for t in range(2):
            y_scr[pl.ds(2 * i + t, n1h, stride=_pitch(n2)), :] = y[:, t * c:(t + 1) * c]
    x0 = _short_conv(x0_ref, cw0_ref, cb0_ref)
    skip = skip_ref[...]
    for j1 in range(n1h):
        rows = slice(j1 * n2, (j1 + 1) * n2)
        prow = slice(j1 * _pitch(n2), j1 * _pitch(n2) + n2)
        out = x0[rows] * (y_scr[prow, :] + z_scr[prow, :] * skip)
        o_ref[rows, :] = (out * sg_ref[rows, :].astype(F32)).astype(BF16)


def _hyena_lat(hu, sg, conv_w, conv_b, skip, kr, ki, tables, bsz, n, layer):
    m1, w2, ma, mb = tables
    big_l = 2 * n
    n2 = HY_N2
    n1 = big_l // n2
    nch = HY_CH // LANES
    ucol = lambda s: pl.BlockSpec((n, LANES), lambda c, b: (b, s * nch + c))
    wcol = lambda s: pl.BlockSpec((None, 3, LANES), lambda c, b: (layer, 0, s * nch + c))
    bcol = lambda s: pl.BlockSpec((None, 1, LANES), lambda c, b: (layer, 0, s * nch + c))
    kspec = pl.BlockSpec((None, big_l, LANES), lambda c, b: (layer, 0, c))
    const = lambda a: pl.BlockSpec(a.shape, lambda c, b: (0,) * a.ndim)
    cb3 = conv_b.reshape(DEPTH, 1, 768)
    return pl.pallas_call(
        functools.partial(_hyena_lat_kernel, n1=n1, n2=n2),
        grid=(nch, bsz),
        in_specs=[ucol(0), ucol(1), ucol(2), wcol(0), wcol(1), wcol(2), bcol(0), bcol(1), bcol(2),
                  kspec, kspec, const(m1), const(w2), const(ma), const(mb),
                  pl.BlockSpec((None, 1, LANES), lambda c, b: (layer, 0, c)),
                  pl.BlockSpec((n, LANES), lambda c, b: (b, 2 * nch + c))],
        out_specs=pl.BlockSpec((n, LANES), lambda c, b: (b, c)),
        out_shape=jax.ShapeDtypeStruct((bsz * n, 256), BF16),
        scratch_shapes=[pltpu.VMEM((n1 // 2 * _pitch(n2), LANES), F32)] * 2
                       + [pltpu.VMEM((n2 * _pitch(n1), LANES), F32)] * 2,
        compiler_params=_cparams("arbitrary", "arbitrary"), name="hyena_lat",
    )(hu, hu, hu, conv_w, conv_w, conv_w, cb3, cb3, cb3, kr, ki, m1, w2, ma, mb,
      skip.reshape(DEPTH, 1, HY_CH), sg)


def _fnet_ctx_kernel(u_ref, cs_ref, dn_ref, fw_ref, fb_ref, sg_ref, o_ref):
    ab = _dot(u_ref[...], cs_ref[...])
    stack = jnp.concatenate([ab[:, :GROUP_W], ab[:, GROUP_W:]], axis=0).astype(BF16)
    f = _dot(dn_ref[...], stack)
    out = _bdot(f, fw_ref[...]) + fb_ref[...]
    o_ref[...] = (out * sg_ref[...].astype(F32)).astype(BF16)


def _fnet_lat_kernel(u_ref, cs_ref, f1_ref, f2_ref, fw_ref, fb_ref, sg_ref, o_ref,
                     a_scr, b_scr, s_re, s_im, o_scr, *, n1, n2):
    halves = GROUP_W // LANES

    def put(scr, rows, val):
        for hh in range(halves):
            scr[hh, rows, :] = val[:, hh * LANES:(hh + 1) * LANES]

    def get(scr, rows):
        return jnp.concatenate([scr[hh, rows, :] for hh in range(halves)], axis=1)

    def block(i, blk):
        return slice(i * _pitch(blk), i * _pitch(blk) + blk)

    ab = _dot(u_ref[...], cs_ref[...])
    for j1 in range(n1):
        put(a_scr, block(j1, n2), ab[j1 * n2:(j1 + 1) * n2, :GROUP_W])
        put(b_scr, block(j1, n2), ab[j1 * n2:(j1 + 1) * n2, GROUP_W:])
    for j2 in range(n2):
        rows = pl.ds(j2, n1, stride=_pitch(n2))
        g = jnp.concatenate([get(a_scr, rows), get(b_scr, rows)], axis=0)
        t = _dot(f1_ref[j2], g.astype(BF16))
        put(s_re, block(j2, n1), t[:n1])
        put(s_im, block(j2, n1), t[n1:])
    for k1 in range(n1):
        rows = pl.ds(k1, n2, stride=_pitch(n1))
        g = jnp.concatenate([get(s_re, rows), get(s_im, rows)], axis=0)
        put(o_scr, rows, _dot(f2_ref[...], g.astype(BF16)))
    fw = fw_ref[...].astype(BF16)
    fb = fb_ref[...]
    for k2 in range(n2):
        rows = slice(k2 * n1, (k2 + 1) * n1)
        out = _dot(get(o_scr, block(k2, n1)).astype(BF16), fw) + fb
        o_ref[rows, :] = (out * sg_ref[rows, :].astype(F32)).astype(BF16)


def _fnet_lat(fu, sg, fn_w, fn_b, cs, f1, f2, bsz, n, layer):
    n2 = FN_N2
    n1 = n // n2
    return pl.pallas_call(
        functools.partial(_fnet_lat_kernel, n1=n1, n2=n2), grid=(bsz,),
        in_specs=[pl.BlockSpec((n, 256), lambda i: (i, 0)),
                  pl.BlockSpec(cs.shape, lambda i: (0, 0)),
                  pl.BlockSpec(f1.shape, lambda i: (0, 0, 0)),
                  pl.BlockSpec(f2.shape, lambda i: (0, 0)),
                  pl.BlockSpec((None, GROUP_W, GROUP_W), lambda i: (layer, 0, 0)),
                  pl.BlockSpec((None, 1, GROUP_W), lambda i: (layer, 0, 0)),
                  pl.BlockSpec((n, 256), lambda i: (i, 3))],
        out_specs=pl.BlockSpec((n, 256), lambda i: (i, 0)),
        out_shape=jax.ShapeDtypeStruct((bsz * n, 256), BF16),
        scratch_shapes=[pltpu.VMEM((GROUP_W // LANES, n1 * _pitch(n2), LANES), F32)] * 2
                       + [pltpu.VMEM((GROUP_W // LANES, n2 * _pitch(n1), LANES), F32)] * 3,
        compiler_params=_cparams("arbitrary"), name="fnet_lat",
    )(fu, cs, f1, f2, fn_w, fn_b.reshape(DEPTH, 1, GROUP_W), sg)


def _ctx_mixers_kernel(sink_ref, aq_ref, akt_ref, av_ref, lam_ref, subln_ref, dq_ref, dkt_ref, dv_ref,
                       hu_ref, cw_ref, cb_ref, kr_ref, ki_ref, fwd_ref, inv_ref, skip_ref,
                       fu_ref, cs_ref, dn_ref, fw_ref, fb_ref, sg_ref, o_ref, s_scr, p_scr, *, layer, lam_init, kc):
    sg = [sg_ref.at[:, j * GROUP_W:(j + 1) * GROUP_W] for j in range(4)]
    out = [o_ref.at[:, j * GROUP_W:(j + 1) * GROUP_W] for j in range(4)]
    _attn_a_kernel(sink_ref, aq_ref, akt_ref, av_ref, sg[0], out[0], layer=layer)
    _diff_kernel(lam_ref, subln_ref, dq_ref, dkt_ref, dv_ref, sg[1], out[1], s_scr, p_scr,
                 lam_init=lam_init, npieces=1, tq_sub=dq_ref.shape[0], kc=kc)
    _hyena_ctx_kernel(hu_ref, cw_ref, cb_ref, kr_ref, ki_ref, fwd_ref, inv_ref, skip_ref, sg[2], out[2])
    _fnet_ctx_kernel(fu_ref, cs_ref, dn_ref, fw_ref, fb_ref, sg[3], out[3])


def _exit_kernel(x_ref, *refs, latent, tiles_per_b):
    mix_refs, (w_ref, g_ref, mod_ref, o_ref) = refs[:-4], refs[-4:]
    row = (pl.program_id(0) // tiles_per_b) if latent else CTX_ROW
    gate = mod_ref[pl.ds(row, 1), :][:, 2 * D_MODEL:]
    mixed = mix_refs[0][...] if len(mix_refs) == 1 else jnp.concatenate([r[...] for r in mix_refs], axis=1)
    y = _dot(mixed, w_ref[...])
    y = y * lax.rsqrt(jnp.mean(y * y, axis=-1, keepdims=True) + EPS) * g_ref[...]
    o_ref[...] = x_ref[...] + gate * y


def _layer_exit(x2d, outs, w_out_bf, g_post, mod_all, n, layer):
    t = x2d.shape[0]
    tm = EXIT_ROW_TILE
    row = lambda i: (i, 0)
    return pl.pallas_call(
        functools.partial(_exit_kernel, latent=True, tiles_per_b=n // tm),
        grid=(t // tm,),
        in_specs=[pl.BlockSpec((tm, D_MODEL), row)] + [pl.BlockSpec((tm, o.shape[1]), row) for o in outs]
                 + [pl.BlockSpec((None, D_MODEL, D_MODEL), lambda i: (layer, 0, 0)),
                    pl.BlockSpec((None, 1, D_MODEL), lambda i: (layer, 0, 0)),
                    pl.BlockSpec((None, COND_ROWS, 3 * D_MODEL), lambda i: (layer, 0, 0))],
        out_specs=pl.BlockSpec((tm, D_MODEL), row),
        out_shape=jax.ShapeDtypeStruct((t, D_MODEL), F32),
        compiler_params=_cparams("arbitrary"),
        name="exit_latent",
    )(x2d, *outs, w_out_bf, g_post.reshape(DEPTH, 1, D_MODEL), mod_all)


def _ctx_layer_kernel(x_ref, mod_ref, gpre_ref, win_ref, *refs, layer, lam_init, kc, aliased):
    if aliased:
        refs = refs[4:]
    (sink_ref, lam_ref, subln_ref, cw_ref, cb_ref, kr_ref, ki_ref, fwd_ref, inv_ref, skip_ref, cs_ref, dn_ref,
     fw_ref, fb_ref, wout_ref, gpost_ref, xo_ref, *cache_refs,
     aq, akt, av, dq, dkt, dv, hu, fu, sg, mixed, s_scr, p_scr) = refs
    if not aliased:
        for r in cache_refs:
            for l in range(layer + 1, DEPTH):
                r[l] = jnp.zeros(r.shape[1:], r.dtype)
        cache_refs = [r.at[layer] for r in cache_refs]
    _entry_kernel(x_ref, mod_ref, gpre_ref, win_ref, aq, akt, av, dq, dkt, dv, hu, fu, sg, *cache_refs,
                  latent=False, tiles_per_b=1)
    _ctx_mixers_kernel(sink_ref, aq, akt, av, lam_ref, subln_ref, dq, dkt, dv, hu, cw_ref, cb_ref, kr_ref, ki_ref,
                       fwd_ref, inv_ref, skip_ref, fu, cs_ref, dn_ref, fw_ref, fb_ref, sg, mixed, s_scr, p_scr,
                       layer=layer, lam_init=lam_init, kc=kc)
    _exit_kernel(x_ref, mixed, wout_ref, gpost_ref, mod_ref, xo_ref, latent=False, tiles_per_b=1)


def _ctx_layer(x2d, caches, mod_all, g_pre, w_in_bf, sink, diff_lambda, diff_subln, conv_w, conv_b, skip, kr, ki,
               fwd_t, inv_t, fn_w, fn_b, cs, dn, w_out_bf, g_post, bsz, n, layer):
    lam_init = 0.8 - 0.6 * math.exp(-0.3 * layer)
    big_l = 2 * n
    aliased = caches is not None
    assert aliased == (layer > 0)
    rows = lambda c: pl.BlockSpec((n, c), lambda b: (b, 0))
    lay = lambda a, c: pl.BlockSpec((None, a, c), lambda b: (layer, 0, 0))
    const = lambda a: pl.BlockSpec(a.shape, lambda b: (0,) * a.ndim)
    widths = (ATT_KV_HEADS * ATT_HEAD_DIM, ATT_KV_HEADS * ATT_HEAD_DIM, 2 * DIF_HEADS * DIF_QK_DIM, DIF_HEADS * DIF_V_DIM)
    if aliased:
        cache_specs = [pl.BlockSpec((None, None, n, w), lambda b: (b, layer, 0, 0)) for w in widths]
    else:
        cache_specs = [pl.BlockSpec((None, DEPTH, n, w), lambda b: (b, 0, 0, 0)) for w in widths]
    nslot = DIF_HEADS * 2 + 1
    vm = lambda shape, dt=BF16: pltpu.VMEM(shape, dt)
    res = pl.pallas_call(
        functools.partial(_ctx_layer_kernel, layer=layer, lam_init=lam_init, kc=DIFF_KEY_CHUNK, aliased=aliased),
        grid=(bsz,),
        in_specs=[rows(D_MODEL), lay(COND_ROWS, 3 * D_MODEL), lay(1, D_MODEL), lay(D_MODEL, D_IN)]
                 + [pl.BlockSpec(memory_space=pl.ANY)] * (4 if aliased else 0)
                 + [pl.BlockSpec(memory_space=pltpu.SMEM),
                    lay(4, DIF_QK_DIM), lay(1, DIF_V_DIM), lay(3, 768), lay(1, 768), lay(big_l, HY_CH), lay(big_l, HY_CH),
                    const(fwd_t), const(inv_t), lay(1, HY_CH), const(cs), const(dn), lay(GROUP_W, GROUP_W),
                    lay(1, GROUP_W), lay(D_MODEL, D_MODEL), lay(1, D_MODEL)],
        out_specs=[rows(D_MODEL)] + cache_specs,
        out_shape=[jax.ShapeDtypeStruct(x2d.shape, F32)]
                  + [jax.ShapeDtypeStruct((bsz, DEPTH, n, w), F32) for w in widths],
        input_output_aliases={4 + j: 1 + j for j in range(4)} if aliased else {},
        scratch_shapes=[vm((n, 256)), vm((128, n)), vm((n, 128)), vm((n, 256)), vm((256, n)), vm((n, 512)),
                        vm((n, 768)), vm((n, 256)), vm((n, 1024)), vm((n, 1024)),
                        vm((nslot, n, n), F32), vm((nslot, n, n))],
        compiler_params=_cparams("arbitrary"), name="ctx_layer",
    )(x2d, mod_all, g_pre.reshape(DEPTH, 1, D_MODEL), w_in_bf, *(caches or ()), sink, diff_lambda,
      diff_subln.reshape(DEPTH, 1, DIF_V_DIM), conv_w, conv_b.reshape(DEPTH, 1, 768), kr, ki, fwd_t, inv_t,
      skip.reshape(DEPTH, 1, HY_CH), cs, dn, fn_w, fn_b.reshape(DEPTH, 1, GROUP_W), w_out_bf,
      g_post.reshape(DEPTH, 1, D_MODEL))
    return res[0], list(res[1:])


def kernel(x_prompt, x_sample, cache_attn_k, cache_attn_v, cache_diff_k, cache_diff_v, c, c_ctx, w_ada, b_ada, norm_pre, norm_post, w_in, w_out, attn_sink, diff_lambda, diff_subln, hy_conv_w, hy_conv_b, hy_filt_w1, hy_filt_b1, hy_filt_w2, hy_filt_b2, hy_filt_w3, hy_filt_freq, hy_skip, fn_w, fn_b):
    bp, lp, _ = x_prompt.shape
    bs, ls, _ = x_sample.shape
    past = cache_attn_k.shape[2]
    assert bs < CTX_ROW + 1 <= COND_ROWS

    cond = jnp.concatenate([c, c_ctx[None, :], jnp.zeros((COND_ROWS - bs - 1, D_MODEL), F32)], axis=0)
    mod_all = _modulation(cond, w_ada, b_ada)
    w_in_bf = w_in.astype(BF16)
    w_out_bf = w_out.astype(BF16)
    sink = attn_sink.reshape(DEPTH * ATT_HEADS)

    rope = _rope_tables(ls, ATT_HEAD_DIM) + _rope_tables(ls, DIF_QK_DIM)
    filt = (jnp.pad(hy_filt_w1, ((0, 0), (0, LANES - FILT_EMB), (0, 0))), hy_filt_b1, hy_filt_w2, hy_filt_b2,
            hy_filt_w3, hy_filt_freq)
    cos_c, sin_c, fwd_c, inv_c = _dense_conv_tables(lp)
    kr_c, ki_c = _hyena_filters_ctx(lp, filt, cos_c, sin_c)
    hy_tables = _hyena_fft_tables(ls)
    hf_l, hb_l = _hyena_filters_lat(ls, filt)
    kr_l, ki_l = _filter_fft(hf_l, hb_l, hy_tables[0], hy_tables[1], ls)
    cs_c, dn_c = _fnet_channel_table(lp), _fnet_dense_table(lp)
    cs_l = _fnet_channel_table(ls)
    f1_l, f2_l = _fnet_fft_tables(ls)

    cak = cache_attn_k.reshape(bs, DEPTH, past, 128).transpose(0, 1, 3, 2).astype(BF16)
    cav = cache_attn_v.reshape(bs, DEPTH, past, 128).astype(BF16)
    cdk = cache_diff_k.reshape(bs, DEPTH, past, 256).transpose(0, 1, 3, 2).astype(BF16)
    cdv = jnp.concatenate([cache_diff_v, jnp.ones_like(cache_diff_v)], axis=-1)
    cdv = cdv.reshape(bs, DEPTH, past, 2 * GROUP_W).astype(BF16)

    xp = x_prompt.reshape(bp * lp, D_MODEL)
    xs = x_sample.reshape(bs * ls, D_MODEL)
    new_caches = None
    for l in range(DEPTH):
        xp, new_caches = _ctx_layer(xp, new_caches, mod_all, norm_pre, w_in_bf, sink, diff_lambda, diff_subln,
                                    hy_conv_w, hy_conv_b, hy_skip, kr_c, ki_c, fwd_c, inv_c, fn_w, fn_b, cs_c, dn_c,
                                    w_out_bf, norm_post, bp, lp, l)

        (aq, akt, av, dq, dkt, dv, hu, fu, sg) = _layer_entry(xs, bs, ls, mod_all, norm_pre, w_in_bf, l, rope)
        outs = (_attn_a_latent(aq, akt, av, cak[:, l], cav[:, l], sg, sink, bs, ls, l),
                _diff_attention(dq, [(dkt, dv), (cdk[:, l], cdv[:, l])], sg, diff_lambda, diff_subln, bs, ls, l),
                _hyena_lat(hu, sg, hy_conv_w, hy_conv_b, hy_skip, kr_l, ki_l, hy_tables, bs, ls, l),
                _fnet_lat(fu, sg, fn_w, fn_b, cs_l, f1_l, f2_l, bs, ls, l))
        xs = _layer_exit(xs, outs, w_out_bf, norm_post, mod_all, ls, l)

    nak, nav, ndk, ndv = new_caches
    return (xp.reshape(bp, lp, D_MODEL), xs.reshape(bs, ls, D_MODEL),
            nak.reshape(bp, DEPTH, lp, ATT_KV_HEADS, ATT_HEAD_DIM), nav.reshape(bp, DEPTH, lp, ATT_KV_HEADS, ATT_HEAD_DIM),
            ndk.reshape(bp, DEPTH, lp, 2, DIF_HEADS, DIF_QK_DIM), ndv.reshape(bp, DEPTH, lp, DIF_HEADS, DIF_V_DIM))
```

```python
import functools
import math

import numpy as np
import jax
import jax.numpy as jnp
from jax import lax
from jax.experimental import pallas as pl
from jax.experimental.pallas import tpu as pltpu

F32 = jnp.float32
BF16 = jnp.bfloat16

D_MODEL = 1024
DEPTH = 2
GRID_W = 64
GROUP_W = 256
ATT_HEADS = 4
ATT_KV_HEADS = 2
ATT_HEAD_DIM = 64
WINDOW = 128
BLOCK = 128
DIF_HEADS = 4
DIF_V_DIM = 64
DIF_QK_DIM = 32
HY_CH = 256
FILT_BANDS = 16
FILT_EMB = 1 + 2 * FILT_BANDS
FILT_HIDDEN = 64
HY_MIN_DECAY = math.log(1e-2) / 1.5
HY_MAX_DECAY = math.log(1e-2) / 0.3
FN_GROUP_CH = 64
ROPE_BASE = 10000.0
EPS = 1e-6
NEG_INF = -1e30

C_AQ, C_AK, C_AV, C_AG = 0, 256, 384, 512
C_DQ, C_DK, C_DV, C_DG = 768, 1024, 1280, 1536
C_HU, C_HG, C_FU, C_FG, D_IN = 1792, 2560, 2816, 3072, 3328
LOG2_E = math.log2(math.e)

LANES = 128
COND_ROWS = 8
CTX_ROW = 4
VMEM_LIMIT = 56 * 1024 * 1024
PITCH_PAD = 8

ROW_TILE = 1024
EXIT_ROW_TILE = 2048
ATTN_A_QBLOCKS = 16
DIFF_Q_ROWS = 256
DIFF_KEY_CHUNK = 512

HY_N2 = 64
FN_N2 = 64


def _cparams(*sem):
    return pltpu.CompilerParams(dimension_semantics=sem, vmem_limit_bytes=VMEM_LIMIT)


def _dot(a, b):
    return jnp.dot(a, b, preferred_element_type=F32)


def _bdot(a, b):
    return jnp.dot(a.astype(BF16), b.astype(BF16), preferred_element_type=F32)


def _dot3(a, b):
    ah = a.astype(BF16)
    al = (a - ah.astype(F32)).astype(BF16)
    bh = b.astype(BF16)
    bl = (b - bh.astype(F32)).astype(BF16)
    return _dot(ah, bh) + _dot(ah, bl) + _dot(al, bh)


def _bf16_table(a):
    return jnp.asarray(a, F32).astype(BF16)


def _rope_tables(n, head_dim):
    pos = np.arange(n)
    row = (pos // GRID_W).astype(np.float64)
    col = (pos % GRID_W).astype(np.float64)
    n_freq = head_dim // 4
    inv = ROPE_BASE ** (-np.arange(n_freq, dtype=np.float64) / n_freq)
    ang = np.concatenate([row[:, None] * inv, col[:, None] * inv], axis=-1)
    reps = LANES // head_dim
    cos = np.tile(np.concatenate([np.cos(ang), np.cos(ang)], axis=-1), (1, reps))
    sin = np.tile(np.concatenate([-np.sin(ang), np.sin(ang)], axis=-1), (1, reps))
    return jnp.asarray(cos, F32), jnp.asarray(sin, F32)


def _filter_feats(n):
    t = np.linspace(0.0, 1.0, n)[:, None]
    w = (2.0 * math.pi / n) * np.arange(n)[:, None]
    f = np.linspace(1e-4, FILT_BANDS - 1, FILT_BANDS)[None, :]
    feats = np.concatenate([t, np.cos(f * w), -np.sin(f * w)], axis=-1)
    feats = np.pad(feats, ((0, 0), (0, LANES - FILT_EMB)))
    feats = np.concatenate([feats[:n // 2], feats[n // 2:]], axis=1)
    deltas = np.abs(np.linspace(HY_MIN_DECAY, HY_MAX_DECAY, HY_CH))
    decay = np.exp(-t * deltas[None, :])
    return jnp.asarray(feats, F32), jnp.asarray(decay, F32)


def _dense_conv_tables(n):
    big_l = 2 * n
    k = np.arange(big_l)[:, None]
    t = np.arange(n)[None, :]
    th = 2.0 * math.pi * k * t / big_l
    fwd = np.concatenate([np.cos(th), -np.sin(th)], axis=0)
    inv = np.concatenate([np.cos(th).T, -np.sin(th).T], axis=1) / big_l
    return (jnp.asarray(np.cos(th), F32), jnp.asarray(np.sin(th), F32), _bf16_table(fwd), _bf16_table(inv))


def _hyena_fft_tables(n):
    big_l = 2 * n
    n2 = HY_N2
    n1 = big_l // n2
    n1h = n1 // 2
    k1 = np.arange(n1)
    j1 = np.arange(n1h)
    j2 = np.arange(n2)
    k2 = np.arange(n2)
    th = 2.0 * math.pi * k1[None, :, None] * (j1[None, None, :] * n2 + j2[:, None, None]) / big_l
    m1 = np.concatenate([np.cos(th), -np.sin(th)], axis=1)
    th2 = 2.0 * math.pi * k2[:, None] * j2[None, :] / n2
    c2, s2 = np.cos(th2), np.sin(th2)
    w2 = np.block([[c2, s2], [-s2, c2]])
    tha = 2.0 * math.pi * j2[None, :, None] * (k1[:, None, None] + n1 * k2[None, None, :]) / big_l
    ca, sa = np.cos(tha), np.sin(tha)
    ma = np.concatenate([np.concatenate([ca, -sa], axis=2), np.concatenate([sa, ca], axis=2)], axis=1)
    thb = 2.0 * math.pi * j1[:, None] * k1[None, :] / n1
    mb = np.concatenate([np.cos(thb), -np.sin(thb)], axis=1) / big_l
    m1 = np.concatenate([m1[0::2], m1[1::2]], axis=2)
    ma = np.concatenate([ma[0::2], ma[1::2]], axis=2)
    return _bf16_table(m1), _bf16_table(w2), _bf16_table(ma), _bf16_table(mb)


def _fnet_channel_table(n):
    m = np.arange(FN_GROUP_CH)
    th = 2.0 * math.pi * m[:, None] * m[None, :] / FN_GROUP_CH
    eye = np.eye(GROUP_W // FN_GROUP_CH)
    sc = 1.0 / math.sqrt(n * FN_GROUP_CH)
    cs = np.concatenate([np.kron(eye, np.cos(th)), np.kron(eye, np.sin(th))], axis=1) * sc
    return _bf16_table(cs)


def _fnet_dense_table(n):
    k = np.arange(n)
    th = 2.0 * math.pi * k[:, None] * k[None, :] / n
    return _bf16_table(np.concatenate([np.cos(th), -np.sin(th)], axis=1))


def _fnet_fft_tables(n):
    n2 = FN_N2
    n1 = n // n2
    k1 = np.arange(n1)
    j1 = np.arange(n1)
    j2 = np.arange(n2)
    k2 = np.arange(n2)
    th = 2.0 * math.pi * k1[None, :, None] * (j1[None, None, :] * n2 + j2[:, None, None]) / n
    c, s = np.cos(th), np.sin(th)
    f1 = np.concatenate([np.concatenate([c, -s], axis=2), np.concatenate([s, c], axis=2)], axis=1)
    th2 = 2.0 * math.pi * k2[:, None] * j2[None, :] / n2
    f2 = np.concatenate([np.cos(th2), -np.sin(th2)], axis=1)
    return _bf16_table(f1), _bf16_table(f2)


def _mod_kernel(cond_ref, w_ref, b_ref, o_ref):
    c = cond_ref[...]
    s = c * jax.nn.sigmoid(c)
    o_ref[...] = _dot3(s, w_ref[...]) + b_ref[...]


def _modulation(cond, w_ada, b_ada):
    tn = 1024
    return pl.pallas_call(
        _mod_kernel,
        grid=(DEPTH, 3 * D_MODEL // tn),
        in_specs=[pl.BlockSpec((COND_ROWS, D_MODEL), lambda l, j: (0, 0)),
                  pl.BlockSpec((None, D_MODEL, tn), lambda l, j: (l, 0, j)),
                  pl.BlockSpec((None, 1, tn), lambda l, j: (l, 0, j))],
        out_specs=pl.BlockSpec((None, COND_ROWS, tn), lambda l, j: (l, 0, j)),
        out_shape=jax.ShapeDtypeStruct((DEPTH, COND_ROWS, 3 * D_MODEL), F32),
        compiler_params=_cparams("arbitrary", "arbitrary"),
        name="modulation",
    )(cond, w_ada, b_ada.reshape(DEPTH, 1, 3 * D_MODEL))


def _rope(x, cos, sin, half):
    lane = lax.broadcasted_iota(jnp.int32, x.shape, 1)
    first = (lane % (2 * half)) < half
    partner = jnp.where(first, pltpu.roll(x, LANES - half, 1), pltpu.roll(x, half, 1))
    return x * cos + partner * sin


def _entry_kernel(*refs, latent, tiles_per_b):
    if latent:
        (x_ref, mod_ref, g_ref, w_ref, ca_ref, sa_ref, cd_ref, sd_ref,
         aq_ref, akt_ref, av_ref, dq_ref, dkt_ref, dv_ref, hu_ref, fu_ref, sg_ref) = refs
        row = pl.program_id(0) // tiles_per_b
    else:
        (x_ref, mod_ref, g_ref, w_ref,
         aq_ref, akt_ref, av_ref, dq_ref, dkt_ref, dv_ref, hu_ref, fu_ref, sg_ref,
         ak32_ref, av32_ref, dk32_ref, dv32_ref) = refs
        row = CTX_ROW
    m = mod_ref[pl.ds(row, 1), :]
    shift, scale = m[:, :D_MODEL], m[:, D_MODEL:2 * D_MODEL]
    x = x_ref[...]
    h = x * lax.rsqrt(jnp.mean(x * x, axis=-1, keepdims=True) + EPS) * g_ref[...]
    hb = (h * (1.0 + scale) + shift).astype(BF16)

    projected = _dot(hb, w_ref[...])

    def proj(c0, c1):
        return projected[:, c0:c1]

    def roped(p, cos_ref, sin_ref, half):
        if not latent:
            return p
        cos, sin = cos_ref[...], sin_ref[...]
        chunks = [_rope(p[:, j:j + LANES], cos, sin, half) for j in range(0, p.shape[1], LANES)]
        return chunks[0] if len(chunks) == 1 else jnp.concatenate(chunks, axis=1)

    ca = sa = cd = sd = None
    if latent:
        ca, sa, cd, sd = ca_ref, sa_ref, cd_ref, sd_ref

    aq_ref[...] = (roped(proj(C_AQ, C_AK), ca, sa, ATT_HEAD_DIM // 2) * (ATT_HEAD_DIM ** -0.5 * LOG2_E)).astype(BF16)
    akv = proj(C_AK, C_AG)
    ak = roped(akv[:, :C_AV - C_AK], ca, sa, ATT_HEAD_DIM // 2)
    akt_ref[...] = ak.T.astype(BF16)
    av = akv[:, C_AV - C_AK:]
    av_ref[...] = av.astype(BF16)
    dq_ref[...] = (roped(proj(C_DQ, C_DK), cd, sd, DIF_QK_DIM // 2) * (DIF_QK_DIM ** -0.5 * LOG2_E)).astype(BF16)
    dk = roped(proj(C_DK, C_DV), cd, sd, DIF_QK_DIM // 2)
    dkt_ref[...] = dk.T.astype(BF16)
    dv = proj(C_DV, C_DG)
    value_lane = lax.broadcasted_iota(jnp.int32, (dv.shape[0], LANES), 1) < DIF_V_DIM
    for h in range(DIF_HEADS):
        pair = dv[:, (h // 2) * LANES:(h // 2 + 1) * LANES]
        if h % 2:
            pair = pltpu.roll(pair, DIF_V_DIM, 1)
        dv_ref[:, h * LANES:(h + 1) * LANES] = jnp.where(value_lane, pair, 1.0).astype(BF16)
    if not latent:
        ak32_ref[...] = ak
        av32_ref[...] = av
        dk32_ref[...] = dk
        dv32_ref[...] = dv
    hu_ref[...] = proj(C_HU, C_HG).astype(BF16)
    fu_ref[...] = proj(C_FU, C_FG).astype(BF16)
    for j, c0 in enumerate((C_AG, C_DG, C_HG, C_FG)):
        g = proj(c0, c0 + GROUP_W)
        sg_ref[:, j * GROUP_W:(j + 1) * GROUP_W] = (g * jax.nn.sigmoid(g)).astype(BF16)


def _layer_entry(x2d, bsz, n, mod_all, g_pre, w_in_bf, layer, rope):
    t = bsz * n
    tm = ROW_TILE
    tiles_per_b = n // tm
    row = lambda i: (i, 0)
    pos = lambda i: (i % tiles_per_b, 0)
    kt_map = lambda i: (i // tiles_per_b, 0, i % tiles_per_b)
    in_specs = [pl.BlockSpec((tm, D_MODEL), row),
                pl.BlockSpec((None, COND_ROWS, 3 * D_MODEL), lambda i: (layer, 0, 0)),
                pl.BlockSpec((None, 1, D_MODEL), lambda i: (layer, 0, 0)),
                pl.BlockSpec((None, D_MODEL, D_IN), lambda i: (layer, 0, 0))] + [pl.BlockSpec((tm, LANES), pos)] * 4
    bf = lambda c: jax.ShapeDtypeStruct((t, c), BF16)
    rs = lambda c: pl.BlockSpec((tm, c), row)
    out_shape = [bf(256), jax.ShapeDtypeStruct((bsz, 128, n), BF16), bf(128),
                 bf(256), jax.ShapeDtypeStruct((bsz, 256, n), BF16), bf(512),
                 bf(768), bf(256), bf(1024)]
    out_specs = [rs(256), pl.BlockSpec((None, 128, tm), kt_map), rs(128),
                 rs(256), pl.BlockSpec((None, 256, tm), kt_map), rs(512),
                 rs(768), rs(256), rs(1024)]
    return pl.pallas_call(
        functools.partial(_entry_kernel, latent=True, tiles_per_b=tiles_per_b),
        grid=(t // tm,), in_specs=in_specs, out_specs=out_specs, out_shape=out_shape,
        compiler_params=_cparams("arbitrary"), name="entry_latent",
    )(x2d, mod_all, g_pre.reshape(DEPTH, 1, D_MODEL), w_in_bf, *rope)


def _sink_columns(sink_ref, layer, tq):
    grp = ATT_HEADS // ATT_KV_HEADS
    head_row = lax.broadcasted_iota(jnp.int32, (grp * tq, 1), 0) // tq
    cols = []
    for g in range(ATT_KV_HEADS):
        col = jnp.zeros((grp * tq, 1), F32)
        for j in range(grp):
            col = jnp.where(head_row == j, sink_ref[layer * ATT_HEADS + g * grp + j] * LOG2_E, col)
        cols.append(col)
    return cols


def _stack_heads(q_ref, rows, g):
    grp = ATT_HEADS // ATT_KV_HEADS
    return jnp.concatenate([q_ref[rows, (g * grp + t) * ATT_HEAD_DIM:(g * grp + t + 1) * ATT_HEAD_DIM]
                            for t in range(grp)], axis=0)


def _attn_a_kernel(sink_ref, q_ref, k_ref, v_ref, sg_ref, o_ref, *, layer):
    tq = q_ref.shape[0]
    grp = ATT_HEADS // ATT_KV_HEADS
    sink_cols = _sink_columns(sink_ref, layer, tq)
    outs = []
    for g in range(ATT_KV_HEADS):
        d0 = g * ATT_HEAD_DIM
        s = _dot(_stack_heads(q_ref, slice(None), g), k_ref[d0:d0 + ATT_HEAD_DIM, :])
        m = jnp.maximum(jnp.max(s, axis=-1, keepdims=True), sink_cols[g])
        p = jnp.exp2(s - m)
        l = jnp.sum(p, axis=-1, keepdims=True) + jnp.exp2(sink_cols[g] - m)
        o = _dot(p.astype(BF16), v_ref[:, d0:d0 + ATT_HEAD_DIM]) / l
        outs += [o[j * tq:(j + 1) * tq] for j in range(grp)]
    o_full = jnp.concatenate(outs, axis=1)
    o_ref[...] = (o_full * sg_ref[...].astype(F32)).astype(BF16)


def _attn_a_win_kernel(sink_ref, q_ref, kp_ref, kc_ref, kn_ref, vp_ref, vc_ref, vn_ref, ck_ref, cv_ref, sg_ref,
                       o_ref, *, layer, nsteps, qb):
    i = pl.program_id(1)
    grp = ATT_HEADS // ATT_KV_HEADS
    rows = grp * BLOCK
    kw = jnp.concatenate([kp_ref[...], kc_ref[...], kn_ref[...]], axis=1)
    vw = jnp.concatenate([vp_ref[...], vc_ref[...], vn_ref[...]], axis=0)
    r = lax.broadcasted_iota(jnp.int32, (rows, 3 * BLOCK), 0) % BLOCK
    c = lax.broadcasted_iota(jnp.int32, (rows, 3 * BLOCK), 1)
    band = (c >= r) & (c <= r + 2 * WINDOW)
    masks = {0: band & ((c >= BLOCK) | (i > 0)), qb - 1: band & ((c < 2 * BLOCK) | (i < nsteps - 1))}
    sink_cols = _sink_columns(sink_ref, layer, BLOCK)
    ck = ck_ref[...]
    cv = cv_ref[...]
    items = [(j, g) for j in range(qb) for g in range(ATT_KV_HEADS)]
    st_a, st_b, outs = {}, {}, {}
    for step in range(len(items) + 2):
        a, b, cc = step, step - 1, step - 2
        if a < len(items):
            j, g = items[a]
            d0 = g * ATT_HEAD_DIM
            q2 = _stack_heads(q_ref, slice(j * BLOCK, (j + 1) * BLOCK), g)
            s_w = _dot(q2, kw[d0:d0 + ATT_HEAD_DIM, j * BLOCK:(j + 3) * BLOCK])
            s_w = jnp.where(masks.get(j, band), s_w, NEG_INF)
            s_c = _dot(q2, ck[d0:d0 + ATT_HEAD_DIM, :])
            m = jnp.maximum(jnp.maximum(jnp.max(s_w, axis=-1, keepdims=True),
                                        jnp.max(s_c, axis=-1, keepdims=True)), sink_cols[g])
            st_a[a] = (s_w, s_c, m)
        if 0 <= b < len(items):
            s_w, s_c, m = st_a.pop(b)
            p_w = jnp.exp2(s_w - m)
            p_c = jnp.exp2(s_c - m)
            l = (jnp.sum(p_w, axis=-1, keepdims=True) + jnp.sum(p_c, axis=-1, keepdims=True)
                 + jnp.exp2(sink_cols[items[b][1]] - m))
            st_b[b] = (p_w.astype(BF16), p_c.astype(BF16), l)
        if 0 <= cc < len(items):
            j, g = items[cc]
            d0 = g * ATT_HEAD_DIM
            p_w, p_c, l = st_b.pop(cc)
            o = (_dot(p_w, vw[j * BLOCK:(j + 3) * BLOCK, d0:d0 + ATT_HEAD_DIM])
                 + _dot(p_c, cv[:, d0:d0 + ATT_HEAD_DIM])) / l
            outs.setdefault(j, []).extend([o[t * BLOCK:(t + 1) * BLOCK] for t in range(grp)])
            if g == ATT_KV_HEADS - 1:
                rs = slice(j * BLOCK, (j + 1) * BLOCK)
                o_full = jnp.concatenate(outs.pop(j), axis=1)
                o_ref[rs, :] = (o_full * sg_ref[rs, :].astype(F32)).astype(BF16)


def _attn_a_latent(aq, akt, av, ck_t, cv, sg, sink, bsz, n, layer):
    qb = ATTN_A_QBLOCKS
    nb = n // BLOCK
    nsteps = nb // qb
    qmap = lambda b, i: (b * nsteps + i, 0)
    prev = lambda i: jnp.maximum(qb * i - 1, 0)
    nxt = lambda i: jnp.minimum(qb * i + qb, nb - 1)
    kedge = lambda f: pl.BlockSpec((None, 128, BLOCK), lambda b, i: (b, 0, f(i)))
    vedge = lambda f: pl.BlockSpec((BLOCK, 128), lambda b, i: (b * nb + f(i), 0))
    return pl.pallas_call(
        functools.partial(_attn_a_win_kernel, layer=layer, nsteps=nsteps, qb=qb),
        grid=(bsz, nsteps),
        in_specs=[pl.BlockSpec(memory_space=pltpu.SMEM),
                  pl.BlockSpec((qb * BLOCK, 256), qmap),
                  kedge(prev), pl.BlockSpec((None, 128, qb * BLOCK), lambda b, i: (b, 0, i)), kedge(nxt),
                  vedge(prev), pl.BlockSpec((qb * BLOCK, 128), qmap), vedge(nxt),
                  pl.BlockSpec((None, 128, ck_t.shape[2]), lambda b, i: (b, 0, 0)),
                  pl.BlockSpec((None, cv.shape[1], 128), lambda b, i: (b, 0, 0)),
                  pl.BlockSpec((qb * BLOCK, 256), qmap)],
        out_specs=pl.BlockSpec((qb * BLOCK, 256), qmap),
        out_shape=jax.ShapeDtypeStruct((bsz * n, 256), BF16),
        compiler_params=_cparams("arbitrary", "arbitrary"),
        name="attn_a_latent",
    )(sink, aq, akt, akt, akt, av, av, av, ck_t, cv, sg)


def _diff_kernel(lam_ref, subln_ref, q_ref, *refs, lam_init, npieces, tq_sub, kc):
    kts = refs[0:2 * npieces:2]
    vs = refs[1:2 * npieces:2]
    sg_ref, o_ref, s_scr, p_scr = refs[2 * npieces:2 * npieces + 4]
    lp = lam_ref[...]
    lam = (jnp.exp(jnp.sum(lp[0:1] * lp[1:2], axis=-1, keepdims=True))
           - jnp.exp(jnp.sum(lp[2:3] * lp[3:4], axis=-1, keepdims=True)) + lam_init)
    subln = subln_ref[...]
    tq = q_ref.shape[0]
    chunks, off = [], 0
    for kt, v in zip(kts, vs):
        for c0 in range(0, kt.shape[1], kc):
            w = min(kc, kt.shape[1] - c0)
            chunks.append((kt, v, c0, w, off))
            off += w
    items = [(r0, h, mp) for r0 in range(0, tq, tq_sub) for h in range(DIF_HEADS) for mp in range(2)]
    n_items = len(items)
    mx, acc, om, outs = {}, {}, {}, {}

    def fold(s):
        return functools.reduce(jnp.maximum, [s[:, j:j + LANES] for j in range(0, s.shape[1], LANES)])

    nslot = s_scr.shape[0]
    lag = nslot - 1
    for step in range(n_items + 2 * lag):
        a = step if step < n_items else None
        b = step - lag if 0 <= step - lag < n_items else None
        c = step - 2 * lag if 0 <= step - 2 * lag < n_items else None
        if a is not None:
            r0, h, mp = items[a]
            f0 = (mp * DIF_HEADS + h) * DIF_QK_DIM
            q_a = q_ref[r0:r0 + tq_sub, f0:f0 + DIF_QK_DIM]
            mx[a] = jnp.full((tq_sub, LANES), NEG_INF, F32)
        if b is not None:
            m_b = jnp.max(mx.pop(b), axis=-1, keepdims=True)
        if c is not None:
            acc[c] = jnp.zeros((tq_sub, 2 * DIF_V_DIM), F32)
            hc = items[c][1]
        for kt, v, c0, w, o0 in chunks:
            if a is not None:
                s = _dot(q_a, kt[f0:f0 + DIF_QK_DIM, c0:c0 + w])
                s_scr[a % nslot, :, o0:o0 + w] = s
                mx[a] = jnp.maximum(mx[a], fold(s))
            if b is not None:
                p_scr[b % nslot, :, o0:o0 + w] = jnp.exp2(s_scr[b % nslot, :, o0:o0 + w] - m_b).astype(BF16)
            if c is not None:
                acc[c] = acc[c] + _dot(p_scr[c % nslot, :, o0:o0 + w],
                                       v[c0:c0 + w, 2 * DIF_V_DIM * hc:2 * DIF_V_DIM * (hc + 1)])
        if c is not None:
            r0, h, mp = items[c]
            o = acc.pop(c)
            om[(r0, h, mp)] = o[:, :DIF_V_DIM] / o[:, DIF_V_DIM:DIF_V_DIM + 1]
            if mp == 1:
                av = om.pop((r0, h, 0)) - lam * om.pop((r0, h, 1))
                y = av * lax.rsqrt(jnp.mean(av * av, axis=-1, keepdims=True) + EPS) * subln * (1.0 - lam_init)
                outs.setdefault(r0, []).append(y)
                if h == DIF_HEADS - 1:
                    o_full = jnp.concatenate(outs.pop(r0), axis=1)
                    o_ref[r0:r0 + tq_sub, :] = (o_full * sg_ref[r0:r0 + tq_sub, :].astype(F32)).astype(BF16)


def _diff_attention(dq, pieces, sg, diff_lambda, diff_subln, bsz, n, layer):
    tq, kc = DIFF_Q_ROWS, DIFF_KEY_CHUNK
    lam_init = 0.8 - 0.6 * math.exp(-0.3 * layer)
    nt = n // tq
    nk_all = sum(kt.shape[2] for kt, _ in pieces)
    assert nk_all > kc
    nslot = 2
    in_specs = [pl.BlockSpec((None, 4, DIF_QK_DIM), lambda b, i: (layer, 0, 0)),
                pl.BlockSpec((None, 1, DIF_V_DIM), lambda b, i: (layer, 0, 0)),
                pl.BlockSpec((tq, 256), lambda b, i: (b * nt + i, 0))]
    args = [diff_lambda, diff_subln.reshape(DEPTH, 1, DIF_V_DIM), dq]
    for kt, v in pieces:
        nk = kt.shape[2]
        in_specs.append(pl.BlockSpec((None, 256, nk), lambda b, i: (b, 0, 0)))
        if v.ndim == 3:
            in_specs.append(pl.BlockSpec((None, nk, 512), lambda b, i: (b, 0, 0)))
        else:
            in_specs.append(pl.BlockSpec((nk, 512), lambda b, i: (b, 0)))
        args += [kt, v]
    in_specs.append(pl.BlockSpec((tq, 256), lambda b, i: (b * nt + i, 1)))
    args.append(sg)
    return pl.pallas_call(
        functools.partial(_diff_kernel, lam_init=lam_init, npieces=len(pieces), tq_sub=tq, kc=kc),
        grid=(bsz, nt), in_specs=in_specs,
        out_specs=pl.BlockSpec((tq, 256), lambda b, i: (b * nt + i, 0)),
        out_shape=jax.ShapeDtypeStruct((bsz * n, 256), BF16),
        scratch_shapes=[pltpu.VMEM((nslot, tq, nk_all), F32), pltpu.VMEM((nslot, tq, nk_all), BF16)],
        compiler_params=_cparams("arbitrary", "arbitrary"),
        name="diff_attn_latent",
    )(*args)


def _filter_core(feats_ref, w1_ref, b1_ref, w2_ref, b2_ref, w3_ref, fr_ref, decay_ref):
    fr = fr_ref[...]
    h = jnp.sin(fr * (_dot3(feats_ref[...], w1_ref[...]) + b1_ref[...]))
    h = jnp.sin(fr * (_dot3(h, w2_ref[...]) + b2_ref[...]))
    h = _dot3(h, w3_ref[...])
    h = jnp.concatenate([h[:, :2 * HY_CH], h[:, 2 * HY_CH:]], axis=0)
    decay = decay_ref[...]
    hf = h[:, :HY_CH] * decay
    hb = h[:, HY_CH:] * decay
    tot = (jnp.sum(jnp.abs(hf), axis=0, keepdims=True) + jnp.sum(jnp.abs(hb), axis=0, keepdims=True) + EPS)
    hf = hf / tot
    hb = hb / tot
    rowi = lax.broadcasted_iota(jnp.int32, hb.shape, 0)
    return hf, jnp.where(rowi == 0, 0.0, hb)


def _filter_ctx_kernel(feats_ref, w1_ref, b1_ref, w2_ref, b2_ref, w3_ref, fr_ref, decay_ref,
                       cos_ref, sin_ref, kr_ref, ki_ref):
    hf, hb0 = _filter_core(feats_ref, w1_ref, b1_ref, w2_ref, b2_ref, w3_ref, fr_ref, decay_ref)
    kr_ref[...] = _dot3(cos_ref[...], hf + hb0)
    ki_ref[...] = _dot3(sin_ref[...], hb0 - hf)


def _filter_lat_kernel(feats_ref, w1_ref, b1_ref, w2_ref, b2_ref, w3_ref, fr_ref, decay_ref, hf_ref, hb_ref):
    hf, hb0 = _filter_core(feats_ref, w1_ref, b1_ref, w2_ref, b2_ref, w3_ref, fr_ref, decay_ref)
    hf_ref[...] = hf
    hb_ref[...] = hb0


def _filter_specs(n, w1p, b1, w2, b2, w3, freq, feats, decay):
    const = lambda shape: pl.BlockSpec(shape, lambda l: (0,) * len(shape))
    lay = lambda a, b: pl.BlockSpec((None, a, b), lambda l: (l, 0, 0))
    hid = 2 * FILT_HIDDEN
    in_specs = [const((n // 2, 2 * LANES)), lay(2 * LANES, hid), lay(1, hid), lay(hid, hid), lay(1, hid),
                lay(hid, 4 * HY_CH), lay(1, hid), const((n, HY_CH))]
    row2 = lambda a: jnp.tile(a.reshape(DEPTH, 1, FILT_HIDDEN), (1, 1, 2))
    args = [feats, _block_diag2(w1p), row2(b1), _block_diag2(w2), row2(b2), _block_diag2(w3), row2(freq), decay]
    return in_specs, args


def _block_diag2(w):
    z = jnp.zeros_like(w)
    return jnp.concatenate([jnp.concatenate([w, z], axis=2), jnp.concatenate([z, w], axis=2)], axis=1)


def _hyena_filters_ctx(n, filt, cos_t, sin_t):
    feats, decay = _filter_feats(n)
    in_specs, args = _filter_specs(n, *filt, feats, decay)
    big_l = 2 * n
    in_specs += [pl.BlockSpec((big_l, n), lambda l: (0, 0))] * 2
    out = pl.BlockSpec((None, big_l, HY_CH), lambda l: (l, 0, 0))
    return pl.pallas_call(
        _filter_ctx_kernel, grid=(DEPTH,), in_specs=in_specs, out_specs=[out, out],
        out_shape=[jax.ShapeDtypeStruct((DEPTH, big_l, HY_CH), F32)] * 2,
        compiler_params=_cparams("arbitrary"), name="hyena_filter_ctx",
    )(*args, cos_t, sin_t)


def _hyena_filters_lat(n, filt):
    feats, decay = _filter_feats(n)
    in_specs, args = _filter_specs(n, *filt, feats, decay)
    out = pl.BlockSpec((None, n, HY_CH), lambda l: (l, 0, 0))
    return pl.pallas_call(
        _filter_lat_kernel, grid=(DEPTH,), in_specs=in_specs, out_specs=[out, out],
        out_shape=[jax.ShapeDtypeStruct((DEPTH, n, HY_CH), F32)] * 2,
        compiler_params=_cparams("arbitrary"), name="hyena_filter_lat",
    )(*args)


def _short_conv(u_ref, cw_ref, cb_ref):
    u = u_ref[...].astype(F32)
    n = u.shape[0]
    rowi = lax.broadcasted_iota(jnp.int32, u.shape, 0)
    up = jnp.where(rowi == 0, 0.0, pltpu.roll(u, 1, 0))
    un = jnp.where(rowi == n - 1, 0.0, pltpu.roll(u, n - 1, 0))
    w = cw_ref[...]
    return up * w[0:1] + u * w[1:2] + un * w[2:3] + cb_ref[...]


def _hyena_ctx_kernel(hu_ref, cw_ref, cb_ref, kr_ref, ki_ref, fwd_ref, inv_ref, skip_ref, sg_ref, o_ref):
    uc = _short_conv(hu_ref, cw_ref, cb_ref)
    x0, x1, v = uc[:, :HY_CH], uc[:, HY_CH:2 * HY_CH], uc[:, 2 * HY_CH:]
    z = x1 * v
    zf = _dot(fwd_ref[...], z.astype(BF16))
    big_l = zf.shape[0] // 2
    zr, zi = zf[:big_l], zf[big_l:]
    kr, ki = kr_ref[...], ki_ref[...]
    p = jnp.concatenate([zr * kr - zi * ki, zr * ki + zi * kr], axis=0).astype(BF16)
    y = _dot(inv_ref[...], p)
    out = x0 * (y + z * skip_ref[...])
    o_ref[...] = (out * sg_ref[...].astype(F32)).astype(BF16)


def _pitch(rows):
    return rows + PITCH_PAD


def _to_pitched(dst, src, blk):
    for i in range(src.shape[0] // blk):
        dst[i * _pitch(blk):i * _pitch(blk) + blk, :] = src[i * blk:(i + 1) * blk, :]


def _block_diag_rows(a, b):
    za, zb = jnp.zeros_like(a), jnp.zeros_like(b)
    return jnp.concatenate([jnp.concatenate([a, zb], axis=1), jnp.concatenate([za, b], axis=1)], axis=0)


def _fft_stage1(src, m1_ref, s_re, s_im, n1, n2):
    c = src.shape[1]
    for i in range(n2 // 2):
        xs = [src[pl.ds(2 * i + t, n1 // 2, stride=_pitch(n2)), :].astype(BF16) for t in range(2)]
        a = _dot(m1_ref[i], _block_diag_rows(*xs))
        for t in range(2):
            r0 = (2 * i + t) * _pitch(n1)
            s_re[r0:r0 + n1, :] = a[:n1, t * c:(t + 1) * c]
            s_im[r0:r0 + n1, :] = a[n1:, t * c:(t + 1) * c]


def _fft_stage2(w2_ref, s_re, s_im, i, n1, n2):
    re = [s_re[pl.ds(2 * i + t, n2, stride=_pitch(n1)), :] for t in range(2)]
    im = [s_im[pl.ds(2 * i + t, n2, stride=_pitch(n1)), :] for t in range(2)]
    rhs = jnp.concatenate([jnp.concatenate(re, axis=1), jnp.concatenate(im, axis=1)], axis=0)
    x = _dot(w2_ref[...], rhs.astype(BF16))
    return x[:n2], x[n2:]


def _filter_fft_kernel(hf_ref, hb_ref, m1_ref, w2_ref, kr_ref, ki_ref, x_scr, s_re, s_im, *, n1, n2):
    c = kr_ref.shape[1]
    for src_ref, sign in ((hf_ref, 1.0), (hb_ref, -1.0)):
        _to_pitched(x_scr, src_ref, n2)
        _fft_stage1(x_scr, m1_ref, s_re, s_im, n1, n2)
        for i in range(n1 // 2):
            xr, xi = _fft_stage2(w2_ref, s_re, s_im, i, n1, n2)
            for t in range(2):
                rows = slice((2 * i + t) * n2, (2 * i + t + 1) * n2)
                if sign > 0:
                    kr_ref[rows, :] = xr[:, t * c:(t + 1) * c]
                    ki_ref[rows, :] = xi[:, t * c:(t + 1) * c]
                else:
                    kr_ref[rows, :] += xr[:, t * c:(t + 1) * c]
                    ki_ref[rows, :] -= xi[:, t * c:(t + 1) * c]


def _filter_fft(hf, hb0, m1, w2, n):
    big_l = 2 * n
    n2 = HY_N2
    n1 = big_l // n2
    inp = pl.BlockSpec((None, n, LANES), lambda l, c: (l, 0, c))
    out = pl.BlockSpec((None, big_l, LANES), lambda l, c: (l, 0, c))
    return pl.pallas_call(
        functools.partial(_filter_fft_kernel, n1=n1, n2=n2),
        grid=(DEPTH, HY_CH // LANES),
        in_specs=[inp, inp,
                  pl.BlockSpec(m1.shape, lambda l, c: (0, 0, 0)),
                  pl.BlockSpec(w2.shape, lambda l, c: (0, 0))],
        out_specs=[out, out],
        out_shape=[jax.ShapeDtypeStruct((DEPTH, big_l, HY_CH), F32)] * 2,
        scratch_shapes=[pltpu.VMEM((n1 // 2 * _pitch(n2), LANES), F32)]
                       + [pltpu.VMEM((n2 * _pitch(n1), LANES), F32)] * 2,
        compiler_params=_cparams("arbitrary", "arbitrary"), name="hyena_filter_fft",
    )(hf, hb0, m1, w2)


def _hyena_lat_kernel(x0_ref, x1_ref, v_ref, cw0_ref, cw1_ref, cw2_ref, cb0_ref, cb1_ref, cb2_ref,
                      kr_ref, ki_ref, m1_ref, w2_ref, ma_ref, mb_ref, skip_ref, sg_ref, o_ref,
                      z_scr, y_scr, s_re, s_im, *, n1, n2):
    n1h = n1 // 2
    c = o_ref.shape[1]
    z = _short_conv(x1_ref, cw1_ref, cb1_ref) * _short_conv(v_ref, cw2_ref, cb2_ref)
    _to_pitched(z_scr, z, n2)
    _fft_stage1(z_scr, m1_ref, s_re, s_im, n1, n2)
    for i in range(n1 // 2):
        xr, xi = _fft_stage2(w2_ref, s_re, s_im, i, n1, n2)
        kr = jnp.concatenate([kr_ref[(2 * i + t) * n2:(2 * i + t + 1) * n2, :] for t in range(2)], axis=1)
        ki = jnp.concatenate([ki_ref[(2 * i + t) * n2:(2 * i + t + 1) * n2, :] for t in range(2)], axis=1)
        pr = (xr * kr - xi * ki).astype(BF16)
        pi = (xr * ki + xi * kr).astype(BF16)
        ps = [jnp.concatenate([pr[:, t * c:(t + 1) * c], pi[:, t * c:(t + 1) * c]], axis=0) for t in range(2)]
        b = _dot(ma_ref[i], _block_diag_rows(*ps))
        for t in range(2):
            s_re[pl.ds(2 * i + t, n2, stride=_pitch(n1)), :] = b[:n2, t * c:(t + 1) * c]
            s_im[pl.ds(2 * i + t, n2, stride=_pitch(n1)), :] = b[n2:, t * c:(t + 1) * c]
    for i in range(n2 // 2):
        blks = []
        for t in range(2):
            r0 = (2 * i + t) * _pitch(n1)
            blks.append(jnp.concatenate([s_re[r0:r0 + n1, :], s_im[r0:r0 + n1, :]], axis=0))
        y = _dot(mb_ref[...], jnp.concatenate(blks, axis=1).astype(BF16))
        for t in range(2):
            y_scr[pl.ds(2 * i + t, n1h, stride=_pitch(n2)), :] = y[:, t * c:(t + 1) * c]
    x0 = _short_conv(x0_ref, cw0_ref, cb0_ref)
    skip = skip_ref[...]
    for j1 in range(n1h):
        rows = slice(j1 * n2, (j1 + 1) * n2)
        prow = slice(j1 * _pitch(n2), j1 * _pitch(n2) + n2)
        out = x0[rows] * (y_scr[prow, :] + z_scr[prow, :] * skip)
        o_ref[rows, :] = (out * sg_ref[rows, :].astype(F32)).astype(BF16)


def _hyena_lat(hu, sg, conv_w, conv_b, skip, kr, ki, tables, bsz, n, layer):
    m1, w2, ma, mb = tables
    big_l = 2 * n
    n2 = HY_N2
    n1 = big_l // n2
    nch = HY_CH // LANES
    ucol = lambda s: pl.BlockSpec((n, LANES), lambda c, b: (b, s * nch + c))
    wcol = lambda s: pl.BlockSpec((None, 3, LANES), lambda c, b: (layer, 0, s * nch + c))
    bcol = lambda s: pl.BlockSpec((None, 1, LANES), lambda c, b: (layer, 0, s * nch + c))
    kspec = pl.BlockSpec((None, big_l, LANES), lambda c, b: (layer, 0, c))
    const = lambda a: pl.BlockSpec(a.shape, lambda c, b: (0,) * a.ndim)
    cb3 = conv_b.reshape(DEPTH, 1, 768)
    return pl.pallas_call(
        functools.partial(_hyena_lat_kernel, n1=n1, n2=n2),
        grid=(nch, bsz),
        in_specs=[ucol(0), ucol(1), ucol(2), wcol(0), wcol(1), wcol(2), bcol(0), bcol(1), bcol(2),
                  kspec, kspec, const(m1), const(w2), const(ma), const(mb),
                  pl.BlockSpec((None, 1, LANES), lambda c, b: (layer, 0, c)),
                  pl.BlockSpec((n, LANES), lambda c, b: (b, 2 * nch + c))],
        out_specs=pl.BlockSpec((n, LANES), lambda c, b: (b, c)),
        out_shape=jax.ShapeDtypeStruct((bsz * n, 256), BF16),
        scratch_shapes=[pltpu.VMEM((n1 // 2 * _pitch(n2), LANES), F32)] * 2
                       + [pltpu.VMEM((n2 * _pitch(n1), LANES), F32)] * 2,
        compiler_params=_cparams("arbitrary", "arbitrary"), name="hyena_lat",
    )(hu, hu, hu, conv_w, conv_w, conv_w, cb3, cb3, cb3, kr, ki, m1, w2, ma, mb,
      skip.reshape(DEPTH, 1, HY_CH), sg)


def _fnet_ctx_kernel(u_ref, cs_ref, dn_ref, fw_ref, fb_ref, sg_ref, o_ref):
    ab = _dot(u_ref[...], cs_ref[...])
    stack = jnp.concatenate([ab[:, :GROUP_W], ab[:, GROUP_W:]], axis=0).astype(BF16)
    f = _dot(dn_ref[...], stack)
    out = _bdot(f, fw_ref[...]) + fb_ref[...]
    o_ref[...] = (out * sg_ref[...].astype(F32)).astype(BF16)


def _fnet_lat_kernel(u_ref, cs_ref, f1_ref, f2_ref, fw_ref, fb_ref, sg_ref, o_ref,
                     a_scr, b_scr, s_re, s_im, o_scr, *, n1, n2):
    halves = GROUP_W // LANES

    def put(scr, rows, val):
        for hh in range(halves):
            scr[hh, rows, :] = val[:, hh * LANES:(hh + 1) * LANES]

    def get(scr, rows):
        return jnp.concatenate([scr[hh, rows, :] for hh in range(halves)], axis=1)

    def block(i, blk):
        return slice(i * _pitch(blk), i * _pitch(blk) + blk)

    ab = _dot(u_ref[...], cs_ref[...])
    for j1 in range(n1):
        put(a_scr, block(j1, n2), ab[j1 * n2:(j1 + 1) * n2, :GROUP_W])
        put(b_scr, block(j1, n2), ab[j1 * n2:(j1 + 1) * n2, GROUP_W:])
    for j2 in range(n2):
        rows = pl.ds(j2, n1, stride=_pitch(n2))
        g = jnp.concatenate([get(a_scr, rows), get(b_scr, rows)], axis=0)
        t = _dot(f1_ref[j2], g.astype(BF16))
        put(s_re, block(j2, n1), t[:n1])
        put(s_im, block(j2, n1), t[n1:])
    for k1 in range(n1):
        rows = pl.ds(k1, n2, stride=_pitch(n1))
        g = jnp.concatenate([get(s_re, rows), get(s_im, rows)], axis=0)
        put(o_scr, rows, _dot(f2_ref[...], g.astype(BF16)))
    fw = fw_ref[...].astype(BF16)
    fb = fb_ref[...]
    for k2 in range(n2):
        rows = slice(k2 * n1, (k2 + 1) * n1)
        out = _dot(get(o_scr, block(k2, n1)).astype(BF16), fw) + fb
        o_ref[rows, :] = (out * sg_ref[rows, :].astype(F32)).astype(BF16)


def _fnet_lat(fu, sg, fn_w, fn_b, cs, f1, f2, bsz, n, layer):
    n2 = FN_N2
    n1 = n // n2
    return pl.pallas_call(
        functools.partial(_fnet_lat_kernel, n1=n1, n2=n2), grid=(bsz,),
        in_specs=[pl.BlockSpec((n, 256), lambda i: (i, 0)),
                  pl.BlockSpec(cs.shape, lambda i: (0, 0)),
                  pl.BlockSpec(f1.shape, lambda i: (0, 0, 0)),
                  pl.BlockSpec(f2.shape, lambda i: (0, 0)),
                  pl.BlockSpec((None, GROUP_W, GROUP_W), lambda i: (layer, 0, 0)),
                  pl.BlockSpec((None, 1, GROUP_W), lambda i: (layer, 0, 0)),
                  pl.BlockSpec((n, 256), lambda i: (i, 3))],
        out_specs=pl.BlockSpec((n, 256), lambda i: (i, 0)),
        out_shape=jax.ShapeDtypeStruct((bsz * n, 256), BF16),
        scratch_shapes=[pltpu.VMEM((GROUP_W // LANES, n1 * _pitch(n2), LANES), F32)] * 2
                       + [pltpu.VMEM((GROUP_W // LANES, n2 * _pitch(n1), LANES), F32)] * 3,
        compiler_params=_cparams("arbitrary"), name="fnet_lat",
    )(fu, cs, f1, f2, fn_w, fn_b.reshape(DEPTH, 1, GROUP_W), sg)


def _ctx_mixers_kernel(sink_ref, aq_ref, akt_ref, av_ref, lam_ref, subln_ref, dq_ref, dkt_ref, dv_ref,
                       hu_ref, cw_ref, cb_ref, kr_ref, ki_ref, fwd_ref, inv_ref, skip_ref,
                       fu_ref, cs_ref, dn_ref, fw_ref, fb_ref, sg_ref, o_ref, s_scr, p_scr, *, layer, lam_init, kc):
    sg = [sg_ref.at[:, j * GROUP_W:(j + 1) * GROUP_W] for j in range(4)]
    out = [o_ref.at[:, j * GROUP_W:(j + 1) * GROUP_W] for j in range(4)]
    _attn_a_kernel(sink_ref, aq_ref, akt_ref, av_ref, sg[0], out[0], layer=layer)
    _diff_kernel(lam_ref, subln_ref, dq_ref, dkt_ref, dv_ref, sg[1], out[1], s_scr, p_scr,
                 lam_init=lam_init, npieces=1, tq_sub=dq_ref.shape[0], kc=kc)
    _hyena_ctx_kernel(hu_ref, cw_ref, cb_ref, kr_ref, ki_ref, fwd_ref, inv_ref, skip_ref, sg[2], out[2])
    _fnet_ctx_kernel(fu_ref, cs_ref, dn_ref, fw_ref, fb_ref, sg[3], out[3])


def _exit_kernel(x_ref, *refs, latent, tiles_per_b):
    mix_refs, (w_ref, g_ref, mod_ref, o_ref) = refs[:-4], refs[-4:]
    row = (pl.program_id(0) // tiles_per_b) if latent else CTX_ROW
    gate = mod_ref[pl.ds(row, 1), :][:, 2 * D_MODEL:]
    mixed = mix_refs[0][...] if len(mix_refs) == 1 else jnp.concatenate([r[...] for r in mix_refs], axis=1)
    y = _dot(mixed, w_ref[...])
    y = y * lax.rsqrt(jnp.mean(y * y, axis=-1, keepdims=True) + EPS) * g_ref[...]
    o_ref[...] = x_ref[...] + gate * y


def _layer_exit(x2d, outs, w_out_bf, g_post, mod_all, n, layer):
    t = x2d.shape[0]
    tm = EXIT_ROW_TILE
    row = lambda i: (i, 0)
    return pl.pallas_call(
        functools.partial(_exit_kernel, latent=True, tiles_per_b=n // tm),
        grid=(t // tm,),
        in_specs=[pl.BlockSpec((tm, D_MODEL), row)] + [pl.BlockSpec((tm, o.shape[1]), row) for o in outs]
                 + [pl.BlockSpec((None, D_MODEL, D_MODEL), lambda i: (layer, 0, 0)),
                    pl.BlockSpec((None, 1, D_MODEL), lambda i: (layer, 0, 0)),
                    pl.BlockSpec((None, COND_ROWS, 3 * D_MODEL), lambda i: (layer, 0, 0))],
        out_specs=pl.BlockSpec((tm, D_MODEL), row),
        out_shape=jax.ShapeDtypeStruct((t, D_MODEL), F32),
        compiler_params=_cparams("arbitrary"),
        name="exit_latent",
    )(x2d, *outs, w_out_bf, g_post.reshape(DEPTH, 1, D_MODEL), mod_all)


def _ctx_layer_kernel(x_ref, mod_ref, gpre_ref, win_ref, *refs, layer, lam_init, kc, aliased):
    if aliased:
        refs = refs[4:]
    (sink_ref, lam_ref, subln_ref, cw_ref, cb_ref, kr_ref, ki_ref, fwd_ref, inv_ref, skip_ref, cs_ref, dn_ref,
     fw_ref, fb_ref, wout_ref, gpost_ref, xo_ref, *cache_refs,
     aq, akt, av, dq, dkt, dv, hu, fu, sg, mixed, s_scr, p_scr) = refs
    if not aliased:
        for r in cache_refs:
            for l in range(layer + 1, DEPTH):
                r[l] = jnp.zeros(r.shape[1:], r.dtype)
        cache_refs = [r.at[layer] for r in cache_refs]
    _entry_kernel(x_ref, mod_ref, gpre_ref, win_ref, aq, akt, av, dq, dkt, dv, hu, fu, sg, *cache_refs,
                  latent=False, tiles_per_b=1)
    _ctx_mixers_kernel(sink_ref, aq, akt, av, lam_ref, subln_ref, dq, dkt, dv, hu, cw_ref, cb_ref, kr_ref, ki_ref,
                       fwd_ref, inv_ref, skip_ref, fu, cs_ref, dn_ref, fw_ref, fb_ref, sg, mixed, s_scr, p_scr,
                       layer=layer, lam_init=lam_init, kc=kc)
    _exit_kernel(x_ref, mixed, wout_ref, gpost_ref, mod_ref, xo_ref, latent=False, tiles_per_b=1)


def _ctx_layer(x2d, caches, mod_all, g_pre, w_in_bf, sink, diff_lambda, diff_subln, conv_w, conv_b, skip, kr, ki,
               fwd_t, inv_t, fn_w, fn_b, cs, dn, w_out_bf, g_post, bsz, n, layer):
    lam_init = 0.8 - 0.6 * math.exp(-0.3 * layer)
    big_l = 2 * n
    aliased = caches is not None
    assert aliased == (layer > 0)
    rows = lambda c: pl.BlockSpec((n, c), lambda b: (b, 0))
    lay = lambda a, c: pl.BlockSpec((None, a, c), lambda b: (layer, 0, 0))
    const = lambda a: pl.BlockSpec(a.shape, lambda b: (0,) * a.ndim)
    widths = (ATT_KV_HEADS * ATT_HEAD_DIM, ATT_KV_HEADS * ATT_HEAD_DIM, 2 * DIF_HEADS * DIF_QK_DIM, DIF_HEADS * DIF_V_DIM)
    if aliased:
        cache_specs = [pl.BlockSpec((None, None, n, w), lambda b: (b, layer, 0, 0)) for w in widths]
    else:
        cache_specs = [pl.BlockSpec((None, DEPTH, n, w), lambda b: (b, 0, 0, 0)) for w in widths]
    nslot = DIF_HEADS * 2 + 1
    vm = lambda shape, dt=BF16: pltpu.VMEM(shape, dt)
    res = pl.pallas_call(
        functools.partial(_ctx_layer_kernel, layer=layer, lam_init=lam_init, kc=DIFF_KEY_CHUNK, aliased=aliased),
        grid=(bsz,),
        in_specs=[rows(D_MODEL), lay(COND_ROWS, 3 * D_MODEL), lay(1, D_MODEL), lay(D_MODEL, D_IN)]
                 + [pl.BlockSpec(memory_space=pl.ANY)] * (4 if aliased else 0)
                 + [pl.BlockSpec(memory_space=pltpu.SMEM),
                    lay(4, DIF_QK_DIM), lay(1, DIF_V_DIM), lay(3, 768), lay(1, 768), lay(big_l, HY_CH), lay(big_l, HY_CH),
                    const(fwd_t), const(inv_t), lay(1, HY_CH), const(cs), const(dn), lay(GROUP_W, GROUP_W),
                    lay(1, GROUP_W), lay(D_MODEL, D_MODEL), lay(1, D_MODEL)],
        out_specs=[rows(D_MODEL)] + cache_specs,
        out_shape=[jax.ShapeDtypeStruct(x2d.shape, F32)]
                  + [jax.ShapeDtypeStruct((bsz, DEPTH, n, w), F32) for w in widths],
        input_output_aliases={4 + j: 1 + j for j in range(4)} if aliased else {},
        scratch_shapes=[vm((n, 256)), vm((128, n)), vm((n, 128)), vm((n, 256)), vm((256, n)), vm((n, 512)),
                        vm((n, 768)), vm((n, 256)), vm((n, 1024)), vm((n, 1024)),
                        vm((nslot, n, n), F32), vm((nslot, n, n))],
        compiler_params=_cparams("arbitrary"), name="ctx_layer",
    )(x2d, mod_all, g_pre.reshape(DEPTH, 1, D_MODEL), w_in_bf, *(caches or ()), sink, diff_lambda,
      diff_subln.reshape(DEPTH, 1, DIF_V_DIM), conv_w, conv_b.reshape(DEPTH, 1, 768), kr, ki, fwd_t, inv_t,
      skip.reshape(DEPTH, 1, HY_CH), cs, dn, fn_w, fn_b.reshape(DEPTH, 1, GROUP_W), w_out_bf,
      g_post.reshape(DEPTH, 1, D_MODEL))
    return res[0], list(res[1:])


def kernel(x_prompt, x_sample, cache_attn_k, cache_attn_v, cache_diff_k, cache_diff_v, c, c_ctx, w_ada, b_ada, norm_pre, norm_post, w_in, w_out, attn_sink, diff_lambda, diff_subln, hy_conv_w, hy_conv_b, hy_filt_w1, hy_filt_b1, hy_filt_w2, hy_filt_b2, hy_filt_w3, hy_filt_freq, hy_skip, fn_w, fn_b):
    bp, lp, _ = x_prompt.shape
    bs, ls, _ = x_sample.shape
    past = cache_attn_k.shape[2]
    assert bs < CTX_ROW + 1 <= COND_ROWS

    cond = jnp.concatenate([c, c_ctx[None, :], jnp.zeros((COND_ROWS - bs - 1, D_MODEL), F32)], axis=0)
    mod_all = _modulation(cond, w_ada, b_ada)
    w_in_bf = w_in.astype(BF16)
    w_out_bf = w_out.astype(BF16)
    sink = attn_sink.reshape(DEPTH * ATT_HEADS)

    rope = _rope_tables(ls, ATT_HEAD_DIM) + _rope_tables(ls, DIF_QK_DIM)
    filt = (jnp.pad(hy_filt_w1, ((0, 0), (0, LANES - FILT_EMB), (0, 0))), hy_filt_b1, hy_filt_w2, hy_filt_b2,
            hy_filt_w3, hy_filt_freq)
    cos_c, sin_c, fwd_c, inv_c = _dense_conv_tables(lp)
    kr_c, ki_c = _hyena_filters_ctx(lp, filt, cos_c, sin_c)
    hy_tables = _hyena_fft_tables(ls)
    hf_l, hb_l = _hyena_filters_lat(ls, filt)
    kr_l, ki_l = _filter_fft(hf_l, hb_l, hy_tables[0], hy_tables[1], ls)
    cs_c, dn_c = _fnet_channel_table(lp), _fnet_dense_table(lp)
    cs_l = _fnet_channel_table(ls)
    f1_l, f2_l = _fnet_fft_tables(ls)

    cak = cache_attn_k.reshape(bs, DEPTH, past, 128).transpose(0, 1, 3, 2).astype(BF16)
    cav = cache_attn_v.reshape(bs, DEPTH, past, 128).astype(BF16)
    cdk = cache_diff_k.reshape(bs, DEPTH, past, 256).transpose(0, 1, 3, 2).astype(BF16)
    cdv = jnp.concatenate([cache_diff_v, jnp.ones_like(cache_diff_v)], axis=-1)
    cdv = cdv.reshape(bs, DEPTH, past, 2 * GROUP_W).astype(BF16)

    xp = x_prompt.reshape(bp * lp, D_MODEL)
    xs = x_sample.reshape(bs * ls, D_MODEL)
    new_caches = None
    for l in range(DEPTH):
        xp, new_caches = _ctx_layer(xp, new_caches, mod_all, norm_pre, w_in_bf, sink, diff_lambda, diff_subln,
                                    hy_conv_w, hy_conv_b, hy_skip, kr_c, ki_c, fwd_c, inv_c, fn_w, fn_b, cs_c, dn_c,
                                    w_out_bf, norm_post, bp, lp, l)

        (aq, akt, av, dq, dkt, dv, hu, fu, sg) = _layer_entry(xs, bs, ls, mod_all, norm_pre, w_in_bf, l, rope)
        outs = (_attn_a_latent(aq, akt, av, cak[:, l], cav[:, l], sg, sink, bs, ls, l),
                _diff_attention(dq, [(dkt, dv), (cdk[:, l], cdv[:, l])], sg, diff_lambda, diff_subln, bs, ls, l),
                _hyena_lat(hu, sg, hy_conv_w, hy_conv_b, hy_skip, kr_l, ki_l, hy_tables, bs, ls, l),
                _fnet_lat(fu, sg, fn_w, fn_b, cs_l, f1_l, f2_l, bs, ls, l))
        xs = _layer_exit(xs, outs, w_out_bf, norm_post, mod_all, ls, l)

    nak, nav, ndk, ndv = new_caches
    return (xp.reshape(bp, lp, D_MODEL), xs.reshape(bs, ls, D_MODEL),
            nak.reshape(bp, DEPTH, lp, ATT_KV_HEADS, ATT_HEAD_DIM), nav.reshape(bp, DEPTH, lp, ATT_KV_HEADS, ATT_HEAD_DIM),
            ndk.reshape(bp, DEPTH, lp, 2, DIF_HEADS, DIF_QK_DIM), ndv.reshape(bp, DEPTH, lp, DIF_HEADS, DIF_V_DIM))
```
